```python
import jax, jax.numpy as jnp
from jax import lax
import numpy as np

D_MODEL = 2048
BATCH = 16
SEQ = 256
DEPTH = 2
DEC_BATCH = 2
DEC_SEQ = 1024
PAST_LEN = 512

GRID_W = 64
N_MOD = 9
D_FF = 256 * ((8 * D_MODEL // 3 + 255) // 256)
HG_DK = 128
HG_DV = 128
HG_WIDTH = D_MODEL // 4
HG_HEADS = HG_WIDTH // HG_DK
HG_CHUNK = 16
NA_HEAD_DIM = 128
NA_WIDTH = D_MODEL // 2
NA_HEADS = NA_WIDTH // NA_HEAD_DIM
NA_KH = 8
NA_KW = 16
NA_QBW = 8
NA_BAND = NA_QBW + NA_KW
NA_QBLOCK = 128
NA_SCALE = NA_HEAD_DIM ** -0.5
LRU_WIDTH = D_MODEL // 4
LRU_BLOCKS = 8
LRU_BW = LRU_WIDTH // LRU_BLOCKS
LRU_CONV = 4
LRU_C = 8.0
IN_SPLITS = (HG_WIDTH,) * 5 + (NA_WIDTH,) * 3 + (LRU_WIDTH,) * 2 + (3 * D_MODEL,)
D_IN = sum(IN_SPLITS)
EPS = 1e-6
NEG = -1e30

kernel_name = 'hybrid_flow_prefix_trunk'


def rmsnorm(x, g):
    xf = x.astype(jnp.float32)
    y = xf * lax.rsqrt(jnp.mean(xf * xf, axis=-1, keepdims=True) + EPS)
    return (y * g.astype(jnp.float32)).astype(x.dtype)


def adaln_params(cvec, w, b):
    m = jax.nn.silu(cvec) @ w + b
    return m.reshape(cvec.shape[0], N_MOD, D_MODEL)


def modulated_norm(x, mods, slot, g):
    shift = mods[:, None, 3 * slot]
    scale = mods[:, None, 3 * slot + 1]
    return rmsnorm(x, g) * (1 + scale) + shift


def gate_of(mods, slot):
    return mods[:, None, 3 * slot + 2]


def swiglu(h, w_up, w_down):
    a, u = jnp.split(h @ w_up, 2, axis=-1)
    return (jax.nn.silu(a) * u) @ w_down


def ffn_sublayer(x, mods, slot, g, w_up, w_down):
    h = modulated_norm(x, mods, slot, g)
    return x + 0.5 * gate_of(mods, slot) * swiglu(h, w_up, w_down)


def split_projection(z):
    parts, start = [], 0
    for width in IN_SPLITS:
        parts.append(z[..., start:start + width])
        start += width
    return parts


def hgrn_lower_bounds(logits):
    p = jax.nn.softmax(logits.astype(jnp.float32), axis=0)
    cum = jnp.cumsum(p, axis=0)
    return cum - cum[0:1]


def hgrn2_chunk_scan(q, log_f, k, v, s0):
    B, N, H, DK = q.shape
    DV = v.shape[-1]
    nC = N // HG_CHUNK
    q, log_f, k = [t.reshape(B, nC, HG_CHUNK, H, DK) for t in (q, log_f, k)]
    v = v.reshape(B, nC, HG_CHUNK, H, DV)
    b = jnp.cumsum(log_f, axis=2)
    causal = jnp.tril(jnp.ones((HG_CHUNK, HG_CHUNK), dtype=bool))[:, :, None, None]
    decay = jnp.exp(jnp.where(causal, b[:, :, :, None] - b[:, :, None, :], -jnp.inf))
    scores = jnp.einsum('bntshd,bnshd->bnhts', q[:, :, :, None] * decay, k)
    o_intra = jnp.einsum('bnhts,bnshe->bnthe', scores, v)
    b_last = b[:, :, -1]
    kv = jnp.einsum('bnshd,bnshe->bnhde', k * jnp.exp(b_last[:, :, None] - b), v)

    def step(S, inp):
        dec, kv_c = inp
        return dec[..., None] * S + kv_c, S

    s_final, s_start = lax.scan(step, s0, (jnp.moveaxis(jnp.exp(b_last), 1, 0), jnp.moveaxis(kv, 1, 0)))
    o_inter = jnp.einsum('bnthd,nbhde->bnthe', q * jnp.exp(b), s_start)
    return (o_intra + o_inter).reshape(B, N, H, DV), s_final


def hgrn2_mixer(zq, zf_fwd, zf_bwd, zi, zo, lb, norm_g, s0):
    B, N, _ = zq.shape
    f32 = jnp.float32
    heads = lambda t: t.reshape(B, N, HG_HEADS, -1)
    q = heads(jax.nn.silu(zq.astype(f32)))
    v = heads(zi.astype(f32))

    def forget(zf, lb_d):
        log_f = jnp.logaddexp(jnp.log(lb_d), jnp.log1p(-lb_d) + jax.nn.log_sigmoid(zf.astype(f32)))
        return heads(log_f), heads(-jnp.expm1(log_f))

    lf_f, k_f = forget(zf_fwd, lb[0])
    lf_b, k_b = forget(zf_bwd, lb[1])
    s0 = s0.astype(f32)
    rev = lambda t: jnp.flip(t, axis=1)
    o_f, s_f = hgrn2_chunk_scan(q, lf_f, k_f, v, s0[:, 0])
    o_b, s_b = hgrn2_chunk_scan(rev(q), rev(lf_b), rev(k_b), rev(v), s0[:, 1])
    o = o_f + rev(o_b)
    o = o * lax.rsqrt(jnp.mean(o * o, axis=-1, keepdims=True) + EPS) * norm_g.astype(f32)
    o = o.reshape(B, N, HG_WIDTH) * jax.nn.silu(zo.astype(f32))
    return o.astype(zq.dtype), jnp.stack([s_f, s_b], axis=1)


def context_attention(q, k, v):
    B, N, H, Dh = q.shape
    nb = N // NA_QBLOCK
    qb = jnp.moveaxis(q.reshape(B, nb, NA_QBLOCK, H, Dh), 1, 0)

    def block(qi):
        s = jnp.einsum('bqhd,bkhd->bhqk', qi, k).astype(jnp.float32) * NA_SCALE
        p = jax.nn.softmax(s, axis=-1).astype(v.dtype)
        return jnp.einsum('bhqk,bkhd->bqhd', p, v)

    o = lax.map(block, qb)
    return jnp.moveaxis(o, 0, 1).reshape(B, N, H * Dh)


def neighbourhood_attention(q, k, v, k_ctx, v_ctx, rpb):
    B, N, H, Dh = q.shape
    f32 = jnp.float32
    rows = N // GRID_W
    kh = min(NA_KH, rows)
    nqb = GRID_W // NA_QBW
    r = jnp.arange(rows)
    key_rows = jnp.clip(r - kh // 2, 0, rows - kh)[:, None] + jnp.arange(kh)[None]
    qcol = jnp.arange(GRID_W).reshape(nqb, NA_QBW)
    win_start = jnp.clip(qcol - NA_KW // 2, 0, GRID_W - NA_KW)
    band_start = jnp.clip(jnp.arange(nqb) * NA_QBW - NA_KW // 2, 0, GRID_W - NA_BAND)
    key_cols = band_start[:, None] + jnp.arange(NA_BAND)[None]
    in_win = (key_cols[:, None] >= win_start[..., None]) & (key_cols[:, None] < win_start[..., None] + NA_KW)
    idx = key_rows[:, None, :, None] * GRID_W + key_cols[None, :, None, :]
    dy = key_rows - r[:, None] + NA_KH - 1
    dx = jnp.clip(key_cols[:, None] - qcol[..., None] + NA_KW - 1, 0, 2 * NA_KW - 2)
    bias = rpb.astype(f32)[:, dy[:, None, None, :, None], dx[None, :, :, None, :]]
    bias = jnp.where(in_win[None, None, :, :, None, :], bias, NEG)
    bias = jnp.moveaxis(bias, 1, 0)
    q_rows = jnp.moveaxis(q.reshape(B, rows, nqb, NA_QBW, H, Dh), 1, 0)
    n_loc = kh * NA_BAND

    def row(args):
        q_r, idx_r, bias_r = args
        k_loc = k[:, idx_r]
        v_loc = v[:, idx_r]
        s_loc = jnp.einsum('bjqhd,bjkwhd->bhjqkw', q_r, k_loc).astype(f32) * NA_SCALE + bias_r
        s_ctx = jnp.einsum('bjqhd,bmhd->bhjqm', q_r, k_ctx).astype(f32) * NA_SCALE
        s = jnp.concatenate([s_loc.reshape(B, H, nqb, NA_QBW, n_loc), s_ctx], axis=-1)
        p = jax.nn.softmax(s, axis=-1).astype(v.dtype)
        p_loc = p[..., :n_loc].reshape(B, H, nqb, NA_QBW, kh, NA_BAND)
        o = (jnp.einsum('bhjqkw,bjkwhd->bjqhd', p_loc, v_loc)
             + jnp.einsum('bhjqm,bmhd->bjqhd', p[..., n_loc:], v_ctx))
        return o.reshape(B, GRID_W, H * Dh)

    o = lax.map(row, (q_rows, idx, bias))
    return jnp.moveaxis(o, 0, 1).reshape(B, N, H * Dh)


def centred_dwconv(x, w, b):
    N = x.shape[1]
    left = LRU_CONV // 2
    xp = jnp.pad(x, ((0, 0), (left, LRU_CONV - 1 - left), (0, 0)))
    out = b + w[0] * xp[:, 0:N]
    for j in range(1, LRU_CONV):
        out = out + w[j] * xp[:, j:j + N]
    return out


def rglru_mixer(zx, zg, conv_w, conv_b, w_a, b_a, w_x, b_x, lam, h0):
    B, N, _ = zx.shape
    f32 = jnp.float32
    x = centred_dwconv(zx, conv_w, conv_b).astype(f32)
    xb = x.reshape(B, N, LRU_BLOCKS, LRU_BW)

    def block_linear(w, bias):
        return jnp.einsum('bnki,kij->bnkj', xb, w.astype(f32)).reshape(B, N, LRU_WIDTH) + bias.astype(f32)

    def direction(d, reverse):
        r_gate = jax.nn.sigmoid(block_linear(w_a[d], b_a[d]))
        i_gate = jax.nn.sigmoid(block_linear(w_x[d], b_x[d]))
        log_a = -LRU_C * r_gate * jax.nn.softplus(-lam[d].astype(f32))
        a = jnp.exp(log_a)
        u = jnp.sqrt(-jnp.expm1(2.0 * log_a)) * (i_gate * x)

        def step(h, inp):
            a_t, u_t = inp
            h = a_t * h + u_t
            return h, h

        h_last, hs = lax.scan(step, h0[:, d].astype(f32), (jnp.moveaxis(a, 1, 0), jnp.moveaxis(u, 1, 0)),
                              reverse=reverse)
        return jnp.moveaxis(hs, 0, 1), h_last

    y_f, h_f = direction(0, False)
    y_b, h_b = direction(1, True)
    y = (y_f + y_b) * jax.nn.gelu(zg.astype(f32))
    return y.astype(zx.dtype), jnp.stack([h_f, h_b], axis=1)


def merge_branches(o_a, o_b, o_c, zgate, w_proj_a, w_proj_b, w_proj_c, w_out):
    g = jax.nn.sigmoid(zgate.astype(jnp.float32)).astype(o_a.dtype)
    g_a, g_b, g_c = jnp.split(g, 3, axis=-1)
    m = g_a * (o_a @ w_proj_a) + g_b * (o_b @ w_proj_b) + g_c * (o_c @ w_proj_c)
    return m @ w_out


def trunk_layer(x, mods, lb, norm_g, ffn1_w_up, ffn1_w_down, ffn2_w_up, ffn2_w_down, w_in,
                hgrn_norm_g, na_rpb, lru_conv_w, lru_conv_b, lru_w_a, lru_b_a, lru_w_x, lru_b_x,
                lru_lambda, w_proj_a, w_proj_b, w_proj_c, w_out, hg_state, lru_state, ctx_k, ctx_v):
    B, N, _ = x.shape
    x = ffn_sublayer(x, mods, 0, norm_g[0], ffn1_w_up, ffn1_w_down)
    h = modulated_norm(x, mods, 1, norm_g[1])
    zq, zf_fwd, zf_bwd, zi, zo, nq, nk, nv, lx, lg, zgate = split_projection(h @ w_in)
    o_a, s_hg = hgrn2_mixer(zq, zf_fwd, zf_bwd, zi, zo, lb, hgrn_norm_g, hg_state)
    heads = lambda t: t.reshape(B, N, NA_HEADS, NA_HEAD_DIM)
    q, k, v = heads(nq), heads(nk), heads(nv)
    if ctx_k is None:
        o_b = context_attention(q, k, v)
    else:
        o_b = neighbourhood_attention(q, k, v, ctx_k, ctx_v, na_rpb)
    o_c, s_lru = rglru_mixer(lx, lg, lru_conv_w, lru_conv_b, lru_w_a, lru_b_a, lru_w_x, lru_b_x,
                             lru_lambda, lru_state)
    x = x + gate_of(mods, 1) * merge_branches(o_a, o_b, o_c, zgate, w_proj_a, w_proj_b, w_proj_c, w_out)
    x = ffn_sublayer(x, mods, 2, norm_g[2], ffn2_w_up, ffn2_w_down)
    return x, k, v, s_hg, s_lru


def setup_inputs(seed: int = 0) -> dict:
    key = jax.random.key(seed)
    ks = iter(jax.random.split(key, 40))
    f32 = jnp.float32
    nrm = lambda shape, scale: jax.random.normal(next(ks), shape, f32) * scale
    a0 = jax.random.uniform(next(ks), (DEPTH, 2, LRU_WIDTH), f32, minval=0.9, maxval=0.999)
    return {
        'x_prompt': nrm((BATCH, SEQ, D_MODEL), 1.0),
        'x_sample': nrm((DEC_BATCH, DEC_SEQ, D_MODEL), 1.0),
        'cache_na_k': nrm((DEC_BATCH, DEPTH, PAST_LEN, NA_HEADS, NA_HEAD_DIM), 1.0),
        'cache_na_v': nrm((DEC_BATCH, DEPTH, PAST_LEN, NA_HEADS, NA_HEAD_DIM), 1.0),
        'state_hgrn': nrm((DEC_BATCH, DEPTH, 2, HG_HEADS, HG_DK, HG_DV), 1.0),
        'state_lru': nrm((DEC_BATCH, DEPTH, 2, LRU_WIDTH), 0.5),
        'c': nrm((DEC_BATCH, D_MODEL), 1.0),
        'c_ctx': nrm((D_MODEL,), 1.0),
        'mod_w': nrm((DEPTH, D_MODEL, N_MOD * D_MODEL), 0.5 * D_MODEL ** -0.5),
        'mod_b': nrm((DEPTH, N_MOD * D_MODEL), 0.02),
        'norm_g': 1.0 + nrm((DEPTH, 3, D_MODEL), 0.1),
        'ffn1_w_up': nrm((DEPTH, D_MODEL, 2 * D_FF), D_MODEL ** -0.5),
        'ffn1_w_down': nrm((DEPTH, D_FF, D_MODEL), D_FF ** -0.5),
        'ffn2_w_up': nrm((DEPTH, D_MODEL, 2 * D_FF), D_MODEL ** -0.5),
        'ffn2_w_down': nrm((DEPTH, D_FF, D_MODEL), D_FF ** -0.5),
        'w_in': nrm((DEPTH, D_MODEL, D_IN), D_MODEL ** -0.5),
        'hgrn_lb_logits': nrm((DEPTH, 2, HG_WIDTH), 1.0),
        'hgrn_norm_g': 1.0 + nrm((DEPTH, HG_HEADS, HG_DV), 0.1),
        'na_rpb': nrm((DEPTH, NA_HEADS, 2 * NA_KH - 1, 2 * NA_KW - 1), 0.1),
        'lru_conv_w': nrm((DEPTH, LRU_CONV, LRU_WIDTH), LRU_CONV ** -0.5),
        'lru_conv_b': nrm((DEPTH, LRU_WIDTH), 0.02),
        'lru_w_a': nrm((DEPTH, 2, LRU_BLOCKS, LRU_BW, LRU_BW), LRU_BW ** -0.5),
        'lru_b_a': nrm((DEPTH, 2, LRU_WIDTH), 0.1),
        'lru_w_x': nrm((DEPTH, 2, LRU_BLOCKS, LRU_BW, LRU_BW), LRU_BW ** -0.5),
        'lru_b_x': nrm((DEPTH, 2, LRU_WIDTH), 0.1),
        'lru_lambda': jnp.log(a0) - jnp.log1p(-a0),
        'w_proj_a': nrm((DEPTH, HG_WIDTH, D_MODEL), HG_WIDTH ** -0.5),
        'w_proj_b': nrm((DEPTH, NA_WIDTH, D_MODEL), NA_WIDTH ** -0.5),
        'w_proj_c': nrm((DEPTH, LRU_WIDTH, D_MODEL), LRU_WIDTH ** -0.5),
        'w_out': nrm((DEPTH, D_MODEL, D_MODEL), D_MODEL ** -0.5),
        'final_norm_g': 1.0 + nrm((D_MODEL,), 0.1),
    }


def reference(x_prompt, x_sample, cache_na_k, cache_na_v, state_hgrn, state_lru, c, c_ctx,
              mod_w, mod_b, norm_g, ffn1_w_up, ffn1_w_down, ffn2_w_up, ffn2_w_down, w_in,
              hgrn_lb_logits, hgrn_norm_g, na_rpb, lru_conv_w, lru_conv_b, lru_w_a, lru_b_a,
              lru_w_x, lru_b_x, lru_lambda, w_proj_a, w_proj_b, w_proj_c, w_out, final_norm_g):
    lb_all = hgrn_lower_bounds(hgrn_lb_logits)

    def weights(l):
        return (norm_g[l], ffn1_w_up[l], ffn1_w_down[l], ffn2_w_up[l], ffn2_w_down[l], w_in[l],
                hgrn_norm_g[l], na_rpb[l], lru_conv_w[l], lru_conv_b[l], lru_w_a[l], lru_b_a[l],
                lru_w_x[l], lru_b_x[l], lru_lambda[l], w_proj_a[l], w_proj_b[l], w_proj_c[l], w_out[l])

    b_ctx = x_prompt.shape[0]
    zero_hg = jnp.zeros((b_ctx, 2, HG_HEADS, HG_DK, HG_DV), jnp.float32)
    zero_lru = jnp.zeros((b_ctx, 2, LRU_WIDTH), jnp.float32)
    xc = x_prompt
    ks, vs, hgs, lrus = [], [], [], []
    for l in range(DEPTH):
        mods = adaln_params(c_ctx[None], mod_w[l], mod_b[l])
        xc, k_l, v_l, s_hg, s_lru = trunk_layer(xc, mods, lb_all[l], *weights(l), zero_hg, zero_lru, None, None)
        ks.append(k_l)
        vs.append(v_l)
        hgs.append(s_hg)
        lrus.append(s_lru)
    y_prompt = rmsnorm(xc, final_norm_g)

    xs = x_sample
    for l in range(DEPTH):
        mods = adaln_params(c, mod_w[l], mod_b[l])
        xs = trunk_layer(xs, mods, lb_all[l], *weights(l), state_hgrn[:, l], state_lru[:, l],
                         cache_na_k[:, l], cache_na_v[:, l])[0]
    y_sample = rmsnorm(xs, final_norm_g)

    new_cache_na_k = jnp.stack(ks, axis=1)
    new_cache_na_v = jnp.stack(vs, axis=1)
    new_state_hgrn = jnp.stack(hgs, axis=1)
    new_state_lru = jnp.stack(lrus, axis=1)
    return (y_prompt, y_sample, new_cache_na_k, new_cache_na_v, new_state_hgrn, new_state_lru)
```

```python
import functools

import jax
import jax.numpy as jnp
from jax import lax
from jax.experimental import pallas as pl
from jax.experimental.pallas import tpu as pltpu

F32 = jnp.float32
BF16 = jnp.bfloat16

EPS = 1e-6
NEG = -1e30
N_MOD = 9
N_COND_ROWS = 8
HG_HEADS = 4
HG_DK = 128
HG_CHUNK = 16
NA_HEADS = 8
NA_HEAD_DIM = 128
NA_KH = 8
NA_KW = 16
NA_QBW = 8
NA_BAND = NA_QBW + NA_KW
GRID_W = 64
LRU_BLOCKS = 8
LRU_CONV = 4
LRU_C = 8.0
LANE = 128
VMEM_LIMIT = 56 * 1024 * 1024


def _params(sem, vmem=VMEM_LIMIT):
    return pltpu.CompilerParams(dimension_semantics=sem, vmem_limit_bytes=vmem)


def _silu(x):
    return x * jax.nn.sigmoid(x)


def _expm1(x):
    u = jnp.exp(x)
    um1 = u - 1.0
    y = um1 * x / jnp.where(u == 1.0, 1.0, jnp.log(u))
    return jnp.where(u == 1.0, x, jnp.where(um1 == -1.0, -1.0, y))


def _dot(a, b):
    return jnp.dot(a, b, preferred_element_type=F32)


def _dot_nt(a, b):
    return lax.dot_general(a, b, (((1,), (1,)), ((), ())), preferred_element_type=F32)


def _dot_tn(a, b):
    return lax.dot_general(a, b, (((0,), (0,)), ((), ())), preferred_element_type=F32)


def _mod_norm(x, g, shift, scale):
    y = x * lax.rsqrt(jnp.mean(x * x, axis=-1, keepdims=True) + EPS)
    return (y * g) * (1.0 + scale) + shift


def _group_index(tm, t_ctx, n_lat):
    n_ctx_tiles = t_ctx // tm
    per_seq = n_lat // tm
    return lambda i: jnp.where(i < n_ctx_tiles, 0, 1 + (i - n_ctx_tiles) // per_seq)


def _mods_kernel(c_ref, w_ref, b_ref, o_ref):
    s = _silu(c_ref[...]).astype(BF16)
    o_ref[...] = _dot(s, w_ref[...].astype(BF16)) + b_ref[...]


def _mods_call(cond, mod_w, mod_b):
    depth, d, n = mod_w.shape
    tn = 1024
    return pl.pallas_call(
        _mods_kernel,
        grid=(depth, n // tn),
        in_specs=[
            pl.BlockSpec((N_COND_ROWS, d), lambda l, j: (0, 0)),
            pl.BlockSpec((None, d, tn), lambda l, j: (l, 0, j)),
            pl.BlockSpec((None, 1, tn), lambda l, j: (l, 0, j)),
        ],
        out_specs=pl.BlockSpec((None, N_COND_ROWS, tn), lambda l, j: (l, 0, j)),
        out_shape=jax.ShapeDtypeStruct((depth, N_COND_ROWS, n), F32),
        compiler_params=_params(("parallel", "parallel")),
    )(cond, mod_w, mod_b.reshape(depth, 1, n))


ROW_CHUNK = 256
EW_CHUNK = 64


def _rows(r, n):
    return pl.ds(pl.multiple_of(r * n, n), n)


def _ffn_kernel(x_ref, mods_ref, g_ref, wa_ref, wu_ref, wd_ref, *rest, slot, final):
    if final:
        fg_ref, o_ref, h_ref, wa_s, wu_s, wd_s = rest
    else:
        o_ref, h_ref, wa_s, wu_s, wd_s = rest
    j = pl.program_id(1)
    tm = x_ref.shape[0]
    shift = mods_ref[3 * slot:3 * slot + 1, :]
    scale = mods_ref[3 * slot + 1:3 * slot + 2, :]
    gate = mods_ref[3 * slot + 2:3 * slot + 3, :]

    @pl.when(j == 0)
    def _():
        def body(r, carry):
            sl = _rows(r, EW_CHUNK)
            h_ref[sl, :] = _mod_norm(x_ref[sl, :], g_ref[...], shift, scale).astype(BF16)
            o_ref[sl, :] = jnp.zeros((EW_CHUNK, o_ref.shape[1]), F32)
            return carry
        lax.fori_loop(0, tm // EW_CHUNK, body, 0)

    wa_s[...] = wa_ref[...].astype(BF16)
    wu_s[...] = wu_ref[...].astype(BF16)
    wd_s[...] = wd_ref[...].astype(BF16)

    def body(r, carry):
        sl = _rows(r, ROW_CHUNK)
        h = h_ref[sl, :]
        a = _dot(h, wa_s[...])
        u = _dot(h, wu_s[...])
        act = (_silu(a) * u).astype(BF16)
        o_ref[sl, :] += _dot(act, wd_s[...])
        return carry
    lax.fori_loop(0, tm // ROW_CHUNK, body, 0)

    @pl.when(j == pl.num_programs(1) - 1)
    def _():
        def body(r, carry):
            sl = _rows(r, EW_CHUNK)
            y = x_ref[sl, :] + (0.5 * gate) * o_ref[sl, :]
            if final:
                y = y * lax.rsqrt(jnp.mean(y * y, axis=-1, keepdims=True) + EPS) * fg_ref[...]
            o_ref[sl, :] = y
            return carry
        lax.fori_loop(0, tm // EW_CHUNK, body, 0)


def _ffn_call(x, mods, norm_g, w_up, w_down, *, layer, slot, t_ctx, n_lat, final_g=None):
    t, d = x.shape
    f = w_down.shape[1]
    tm, tf = 1024, 256
    nf = f // tf
    group = _group_index(tm, t_ctx, n_lat)
    norm_slot = slot
    in_specs = [
        pl.BlockSpec((tm, d), lambda i, j: (i, 0)),
        pl.BlockSpec((None, None, N_MOD, d), lambda i, j: (layer, group(i), 0, 0)),
        pl.BlockSpec((None, None, 1, d), lambda i, j: (layer, norm_slot, 0, 0)),
        pl.BlockSpec((None, d, tf), lambda i, j: (layer, 0, j)),
        pl.BlockSpec((None, d, tf), lambda i, j: (layer, 0, nf + j)),
        pl.BlockSpec((None, tf, d), lambda i, j: (layer, j, 0)),
    ]
    args = [x, mods, norm_g.reshape(norm_g.shape[0], norm_g.shape[1], 1, d), w_up, w_up, w_down]
    if final_g is not None:
        in_specs.append(pl.BlockSpec((1, d), lambda i, j: (0, 0)))
        args.append(final_g.reshape(1, d))
    return pl.pallas_call(
        functools.partial(_ffn_kernel, slot=slot, final=final_g is not None),
        grid=(t // tm, nf),
        in_specs=in_specs,
        out_specs=pl.BlockSpec((tm, d), lambda i, j: (i, 0)),
        out_shape=jax.ShapeDtypeStruct((t, d), F32),
        scratch_shapes=[
            pltpu.VMEM((tm, d), BF16),
            pltpu.VMEM((d, tf), BF16),
            pltpu.VMEM((d, tf), BF16),
            pltpu.VMEM((tf, d), BF16),
        ],
        compiler_params=_params(("parallel", "arbitrary")),
    )(*args)


def _inproj_kernel(x_ref, mods_ref, g_ref, w_ref, z_ref, h_ref, w_s):
    j = pl.program_id(1)
    tm = x_ref.shape[0]
    shift = mods_ref[3:4, :]
    scale = mods_ref[4:5, :]

    @pl.when(j == 0)
    def _():
        def body(r, carry):
            sl = _rows(r, EW_CHUNK)
            h_ref[sl, :] = _mod_norm(x_ref[sl, :], g_ref[...], shift, scale).astype(BF16)
            return carry
        lax.fori_loop(0, tm // EW_CHUNK, body, 0)

    w_s[...] = w_ref[...].astype(BF16)

    def body(r, carry):
        sl = _rows(r, ROW_CHUNK)
        z_ref[sl, :] = _dot(h_ref[sl, :], w_s[...])
        return carry
    lax.fori_loop(0, tm // ROW_CHUNK, body, 0)


def _inproj_call(x, mods, norm_g, w_in, *, layer, t_ctx, n_lat):
    t, d = x.shape
    n = w_in.shape[2]
    tm, tn = 1024, 512
    group = _group_index(tm, t_ctx, n_lat)
    return pl.pallas_call(
        _inproj_kernel,
        grid=(t // tm, n // tn),
        in_specs=[
            pl.BlockSpec((tm, d), lambda i, j: (i, 0)),
            pl.BlockSpec((None, None, N_MOD, d), lambda i, j: (layer, group(i), 0, 0)),
            pl.BlockSpec((None, None, 1, d), lambda i, j: (layer, 1, 0, 0)),
            pl.BlockSpec((None, d, tn), lambda i, j: (layer, 0, j)),
        ],
        out_specs=pl.BlockSpec((tm, tn), lambda i, j: (i, j)),
        out_shape=jax.ShapeDtypeStruct((t, n), F32),
        scratch_shapes=[pltpu.VMEM((tm, d), BF16), pltpu.VMEM((d, tn), BF16)],
        compiler_params=_params(("parallel", "arbitrary")),
    )(x, mods, norm_g.reshape(norm_g.shape[0], norm_g.shape[1], 1, d), w_in)


def _log_forget(zf, log_lb, log1m_lb):
    ls = jnp.minimum(zf, 0.0) - jnp.log1p(jnp.exp(-jnp.abs(zf)))
    b = log1m_lb + ls
    hi = jnp.maximum(log_lb, b)
    return hi + jnp.log1p(jnp.exp(-jnp.abs(log_lb - b)))


def _hgrn_kernel(zq_ref, zff_ref, zfb_ref, zi_ref, zo_ref, lg_ref, ng_ref, *rest, layer, has_s0):
    if has_s0:
        s0_ref, o_ref, sfin_ref, of_s, ob_s, st_s = rest
    else:
        o_ref, sfin_ref, of_s, ob_s, st_s = rest
    n = zq_ref.shape[0]
    n_chunks = n // HG_CHUNK
    c = HG_CHUNK

    logits = lg_ref[...]
    depth = logits.shape[0]
    mx = logits[0]
    for i in range(1, depth):
        mx = jnp.maximum(mx, logits[i])
    ex = [jnp.exp(logits[i] - mx) for i in range(depth)]
    tot = ex[0]
    for i in range(1, depth):
        tot = tot + ex[i]
    lb = jnp.zeros_like(mx)
    for i in range(1, layer + 1):
        lb = lb + ex[i] / tot
    log_lb = jnp.log(lb)
    log1m_lb = jnp.log1p(-lb)

    if has_s0:
        st_s[...] = s0_ref[...]
    else:
        st_s[...] = jnp.zeros(st_s.shape, F32)

    row = lax.broadcasted_iota(jnp.int32, (c, c), 0)
    col = lax.broadcasted_iota(jnp.int32, (c, c), 1)
    tri = [(col <= row).astype(BF16), (col >= row).astype(BF16)]
    s_idx = lax.broadcasted_iota(jnp.int32, (c, LANE), 0)
    ones = jnp.ones((HG_DK, LANE), BF16)
    sel_r = lax.broadcasted_iota(jnp.int32, (c, c * c), 0)
    sel_c = lax.broadcasted_iota(jnp.int32, (c, c * c), 1)
    sel = ((sel_c >= sel_r * c) & (sel_c < sel_r * c + c)).astype(BF16)

    def chunk(d, h, ci):
        rows = _rows(ci, c)
        cols = slice(h * HG_DK, (h + 1) * HG_DK)
        q = _silu(zq_ref[rows, cols])
        v = zi_ref[rows, cols]
        zf = (zff_ref if d == 0 else zfb_ref)[rows, cols]
        lf = _log_forget(zf, log_lb[d:d + 1, cols], log1m_lb[d:d + 1, cols])
        k = -_expm1(lf)
        lf_hi = lf.astype(BF16)
        lf_lo = (lf - lf_hi.astype(F32)).astype(BF16)
        b = _dot(tri[d], lf_hi) + _dot(tri[d], lf_lo)
        edge = c - 1 if d == 0 else 0
        b_last = b[edge:edge + 1, :]
        parts = []
        for t in range(c):
            vis = (s_idx <= t) if d == 0 else (s_idx >= t)
            dec = jnp.exp(jnp.where(vis, b[t:t + 1, :] - b, NEG))
            parts.append((q[t:t + 1, :] * dec) * k)
        p = jnp.concatenate(parts, axis=0).astype(BF16)
        scores = _dot(p, ones)
        xv = (scores * jnp.concatenate([v] * c, axis=0)).astype(BF16)
        o_intra = _dot(sel, xv)
        st = st_s[d, h]
        o_inter = _dot_nt((q * jnp.exp(b)).astype(BF16), st.astype(BF16))
        kd = (k * jnp.exp(b_last - b)).astype(BF16)
        st_s[d, h] = st * jnp.exp(b_last) + _dot_tn(v.astype(BF16), kd)
        (of_s if d == 0 else ob_s)[rows, cols] = o_intra + o_inter

    def body(i, carry):
        for h in range(HG_HEADS):
            chunk(0, h, i)
            chunk(1, h, n_chunks - 1 - i)
        return carry
    lax.fori_loop(0, n_chunks, body, 0)

    sfin_ref[...] = st_s[...]

    def fin(r, carry):
        rows = _rows(r, EW_CHUNK)
        for h in range(HG_HEADS):
            cols = slice(h * HG_DK, (h + 1) * HG_DK)
            o = of_s[rows, cols] + ob_s[rows, cols]
            o = o * lax.rsqrt(jnp.mean(o * o, axis=-1, keepdims=True) + EPS) * ng_ref[h:h + 1, :]
            o_ref[rows, cols] = (o * _silu(zo_ref[rows, cols])).astype(o_ref.dtype)
        return carry
    lax.fori_loop(0, n // EW_CHUNK, fin, 0)


def _hgrn_call(z, lb_logits, hgrn_norm_g, s0_t, *, layer, n_seq, n_tok, row_block0):
    w = HG_HEADS * HG_DK
    depth = lb_logits.shape[0]
    zspec = lambda k: pl.BlockSpec((n_tok, w), lambda s: (row_block0 + s, k))
    in_specs = [zspec(0), zspec(1), zspec(2), zspec(3), zspec(4),
                pl.BlockSpec((depth, 2, w), lambda s: (0, 0, 0)),
                pl.BlockSpec((None, HG_HEADS, HG_DK), lambda s: (layer, 0, 0))]
    args = [z, z, z, z, z, lb_logits, hgrn_norm_g]
    st_spec = pl.BlockSpec((None, 2, HG_HEADS, HG_DK, HG_DK), lambda s: (s, 0, 0, 0, 0))
    if s0_t is not None:
        in_specs.append(st_spec)
        args.append(s0_t)
    return pl.pallas_call(
        functools.partial(_hgrn_kernel, layer=layer, has_s0=s0_t is not None),
        grid=(n_seq,),
        in_specs=in_specs,
        out_specs=[pl.BlockSpec((n_tok, w), lambda s: (s, 0)), st_spec],
        out_shape=[jax.ShapeDtypeStruct((n_seq * n_tok, w), BF16),
                   jax.ShapeDtypeStruct((n_seq, 2, HG_HEADS, HG_DK, HG_DK), F32)],
        scratch_shapes=[pltpu.VMEM((n_tok, w), F32), pltpu.VMEM((n_tok, w), F32),
                        pltpu.VMEM((2, HG_HEADS, HG_DK, HG_DK), F32)],
        compiler_params=_params(("parallel",)),
    )(*args)


def _ctx_attn_kernel(q0_ref, q1_ref, k0_ref, k1_ref, v0_ref, v1_ref, o_ref, *, scale):
    q_refs, k_refs, v_refs = (q0_ref, q1_ref), (k0_ref, k1_ref), (v0_ref, v1_ref)
    heads_per_block = q0_ref.shape[1] // NA_HEAD_DIM
    for h in range(NA_HEADS):
        blk, off = divmod(h, heads_per_block)
        cols = slice(off * NA_HEAD_DIM, (off + 1) * NA_HEAD_DIM)
        s = _dot_nt(q_refs[blk][:, cols].astype(BF16), k_refs[blk][:, cols].astype(BF16)) * scale
        p = jnp.exp(s - jnp.max(s, axis=-1, keepdims=True))
        p = p / jnp.sum(p, axis=-1, keepdims=True)
        o = _dot(p.astype(BF16), v_refs[blk][:, cols].astype(BF16))
        o_ref[:, h * NA_HEAD_DIM:(h + 1) * NA_HEAD_DIM] = o.astype(o_ref.dtype)


def _ctx_attn_call(z, *, n_seq, n_tok, col0):
    w = NA_HEADS * NA_HEAD_DIM
    half = w // 2
    zspec = lambda k: pl.BlockSpec((n_tok, half), lambda s: (s, col0 // half + k))
    return pl.pallas_call(
        functools.partial(_ctx_attn_kernel, scale=NA_HEAD_DIM ** -0.5),
        grid=(n_seq,),
        in_specs=[zspec(k) for k in range(6)],
        out_specs=pl.BlockSpec((n_tok, w), lambda s: (s, 0)),
        out_shape=jax.ShapeDtypeStruct((n_seq * n_tok, w), BF16),
        compiler_params=_params(("parallel",)),
    )(z, z, z, z, z, z)


def _na_kernel(q_ref, k_ref, v_ref, kc_ref, vc_ref, bias_ref, o_ref, *, scale, rows, kh):
    kc = kc_ref[...].astype(BF16)
    vc = vc_ref[...].astype(BF16)
    for r in range(rows):
        start = min(max(r - kh // 2, 0), rows - kh) * GRID_W
        q = q_ref[r * GRID_W:(r + 1) * GRID_W, :].astype(BF16)
        k_loc = k_ref[start:start + kh * GRID_W, :].astype(BF16)
        v_loc = v_ref[start:start + kh * GRID_W, :].astype(BF16)
        s_loc = _dot_nt(q, k_loc) * scale + bias_ref[r]
        s_ctx = _dot_nt(q, kc) * scale
        m = jnp.maximum(jnp.max(s_loc, axis=-1, keepdims=True), jnp.max(s_ctx, axis=-1, keepdims=True))
        p_loc = jnp.exp(s_loc - m)
        p_ctx = jnp.exp(s_ctx - m)
        den = jnp.sum(p_loc, axis=-1, keepdims=True) + jnp.sum(p_ctx, axis=-1, keepdims=True)
        o = _dot((p_loc / den).astype(BF16), v_loc) + _dot((p_ctx / den).astype(BF16), vc)
        o_ref[r * GRID_W:(r + 1) * GRID_W, :] = o.astype(o_ref.dtype)


def _na_bias(rpb, rows, kh):
    r = jnp.arange(rows)
    key_rows = jnp.clip(r - kh // 2, 0, rows - kh)[:, None] + jnp.arange(kh)[None]
    dy = key_rows - r[:, None] + NA_KH - 1
    qcol = jnp.arange(GRID_W)
    kcol = jnp.arange(GRID_W)
    win_start = jnp.clip(qcol - NA_KW // 2, 0, GRID_W - NA_KW)
    in_win = (kcol[None, :] >= win_start[:, None]) & (kcol[None, :] < win_start[:, None] + NA_KW)
    dx = jnp.clip(kcol[None, :] - qcol[:, None] + NA_KW - 1, 0, 2 * NA_KW - 2)
    bias = rpb.astype(F32)[:, dy[:, None, :, None], dx[None, :, None, :]]
    bias = jnp.where(in_win[None, None, :, None, :], bias, NEG)
    return bias.reshape(rpb.shape[0], rows, GRID_W, kh * GRID_W)


def _na_call(z, cache_k, cache_v, bias, *, layer, n_seq, n_tok, row_block0, col_block0):
    rows = n_tok // GRID_W
    kh = min(NA_KH, rows)
    dh = NA_HEAD_DIM
    past = cache_k.shape[2]
    zspec = lambda k: pl.BlockSpec((n_tok, dh), lambda b, h: (row_block0 + b, col_block0 + k * NA_HEADS + h))
    cspec = pl.BlockSpec((None, None, past, dh), lambda b, h: (b, layer, 0, h))
    return pl.pallas_call(
        functools.partial(_na_kernel, scale=dh ** -0.5, rows=rows, kh=kh),
        grid=(n_seq, NA_HEADS),
        in_specs=[zspec(0), zspec(1), zspec(2), cspec, cspec,
                  pl.BlockSpec((None, rows, GRID_W, kh * GRID_W), lambda b, h: (h, 0, 0, 0))],
        out_specs=pl.BlockSpec((n_tok, dh), lambda b, h: (b, h)),
        out_shape=jax.ShapeDtypeStruct((n_seq * n_tok, NA_HEADS * dh), BF16),
        compiler_params=_params(("parallel", "parallel")),
    )(z, z, z, cache_k, cache_v, bias)


def _lru_kernel(zx_ref, zg_ref, cw_ref, cb_ref, wa_ref, ba_ref, wx_ref, bx_ref, lam_ref, *rest,
                n_seq, n_tok, has_h0):
    if has_h0:
        h0_ref, y_ref, hfin_ref, a_s, u_s = rest
    else:
        y_ref, hfin_ref, a_s, u_s = rest
    tpos = lax.broadcasted_iota(jnp.int32, (n_tok, LANE), 0)
    neg_lam = -lam_ref[...]
    softplus = jnp.maximum(neg_lam, 0.0) + jnp.log1p(jnp.exp(-jnp.abs(neg_lam)))
    left = LRU_CONV // 2

    def gates(g, carry):
        rows = _rows(g, n_tok)
        zx = zx_ref[rows, :]
        x = cb_ref[...] + cw_ref[left:left + 1, :] * zx
        for j in range(LRU_CONV):
            off = j - left
            if off == 0:
                continue
            shifted = pltpu.roll(zx, (-off) % n_tok, 0)
            valid = (tpos + off >= 0) & (tpos + off < n_tok)
            x = x + cw_ref[j:j + 1, :] * jnp.where(valid, shifted, 0.0)
        xb = x.astype(BF16)
        for d in range(2):
            r_gate = jax.nn.sigmoid(_dot(xb, wa_ref[d].astype(BF16)) + ba_ref[d:d + 1, :])
            i_gate = jax.nn.sigmoid(_dot(xb, wx_ref[d].astype(BF16)) + bx_ref[d:d + 1, :])
            log_a = (-LRU_C * r_gate) * softplus[d:d + 1, :]
            a_s[d, rows, :] = jnp.exp(log_a)
            u_s[d, rows, :] = jnp.sqrt(-_expm1(2.0 * log_a)) * (i_gate * x)
        return carry
    lax.fori_loop(0, n_seq, gates, 0)

    if has_h0:
        h_init = (h0_ref[0], h0_ref[1])
    else:
        h_init = (jnp.zeros((n_seq, LANE), F32), jnp.zeros((n_seq, LANE), F32))

    def step(t, carry):
        hf, hb = carry
        fw = pl.ds(t, n_seq, stride=n_tok)
        bw = pl.ds(n_tok - 1 - t, n_seq, stride=n_tok)
        hf = a_s[0, fw, :] * hf + u_s[0, fw, :]
        hb = a_s[1, bw, :] * hb + u_s[1, bw, :]
        u_s[0, fw, :] = hf
        u_s[1, bw, :] = hb
        return hf, hb
    hf, hb = lax.fori_loop(0, n_tok, step, h_init)
    hfin_ref[0] = hf
    hfin_ref[1] = hb

    def fin(g, carry):
        rows = _rows(g, n_tok)
        y = (u_s[0, rows, :] + u_s[1, rows, :]) * jax.nn.gelu(zg_ref[rows, :], approximate=True)
        y_ref[rows, :] = y.astype(y_ref.dtype)
        return carry
    lax.fori_loop(0, n_seq, fin, 0)


def _block_diag_pairs(w):
    depth, nd, nb, bw, _ = w.shape
    w = w.reshape(depth, nd, nb // 2, 2, bw, bw)
    eye = jnp.eye(2, dtype=w.dtype)
    out = w[:, :, :, :, :, None, :] * eye[None, None, None, :, None, :, None]
    return out.reshape(depth, nd, nb // 2, 2 * bw, 2 * bw)


def _lru_call(z, conv_w, conv_b, wa_bd, b_a, wx_bd, b_x, lam, h0_t, *, layer, n_seq, n_tok, row_block0,
              col_block0):
    w = conv_b.shape[1]
    n_cb = w // LANE
    rows = n_seq * n_tok
    zspec = lambda k: pl.BlockSpec((rows, LANE), lambda cb: (row_block0, col_block0 + k * n_cb + cb))
    vec2 = pl.BlockSpec((None, 2, LANE), lambda cb: (layer, 0, cb))
    wspec = pl.BlockSpec((None, 2, None, LANE, LANE), lambda cb: (layer, 0, cb, 0, 0))
    hspec = pl.BlockSpec((2, n_seq, LANE), lambda cb: (0, 0, cb))
    in_specs = [zspec(0), zspec(1),
                pl.BlockSpec((None, LRU_CONV, LANE), lambda cb: (layer, 0, cb)),
                pl.BlockSpec((None, 1, LANE), lambda cb: (layer, 0, cb)),
                wspec, vec2, wspec, vec2, vec2]
    args = [z, z, conv_w, conv_b.reshape(conv_b.shape[0], 1, w), wa_bd, b_a, wx_bd, b_x, lam]
    if h0_t is not None:
        in_specs.append(hspec)
        args.append(h0_t)
    return pl.pallas_call(
        functools.partial(_lru_kernel, n_seq=n_seq, n_tok=n_tok, has_h0=h0_t is not None),
        grid=(n_cb,),
        in_specs=in_specs,
        out_specs=[pl.BlockSpec((rows, LANE), lambda cb: (0, cb)), hspec],
        out_shape=[jax.ShapeDtypeStruct((rows, w), BF16), jax.ShapeDtypeStruct((2, n_seq, w), F32)],
        scratch_shapes=[pltpu.VMEM((2, rows, LANE), F32), pltpu.VMEM((2, rows, LANE), F32)],
        compiler_params=_params(("parallel",)),
    )(*args)


def _merge_kernel(x_ref, mods_ref, oa_ref, ob_ref, oc_ref, ga_ref, gb_ref, gc_ref,
                  wa_ref, wb_ref, wc_ref, wo_ref, o_ref):
    j = pl.program_id(1)
    tm = x_ref.shape[0]
    gate = mods_ref[5:6, :]

    @pl.when(j == 0)
    def _():
        o_ref[...] = jnp.zeros(o_ref.shape, F32)

    def branch(o_r, w_r, g_r):
        return jax.nn.sigmoid(g_r[...]) * _dot(o_r[...], w_r[...].astype(BF16))

    m = branch(oa_ref, wa_ref, ga_ref) + branch(ob_ref, wb_ref, gb_ref) + branch(oc_ref, wc_ref, gc_ref)
    o_ref[...] += _dot(m.astype(BF16), wo_ref[...].astype(BF16))

    @pl.when(j == pl.num_programs(1) - 1)
    def _():
        def body(r, carry):
            sl = _rows(r, EW_CHUNK)
            o_ref[sl, :] = x_ref[sl, :] + gate * o_ref[sl, :]
            return carry
        lax.fori_loop(0, tm // EW_CHUNK, body, 0)


def _merge_call(x, mods, o_a, o_b, o_c, z, w_proj_a, w_proj_b, w_proj_c, w_out, *, layer, t_ctx, n_lat,
                gate_col0):
    t, d = x.shape
    tm, tn = 512, 256
    nj = d // tn
    g0 = gate_col0 // tn
    group = _group_index(tm, t_ctx, n_lat)
    gspec = lambda k: pl.BlockSpec((tm, tn), lambda i, j: (i, g0 + k * nj + j))
    ospec = lambda w: pl.BlockSpec((tm, w), lambda i, j: (i, 0))
    wspec = lambda w: pl.BlockSpec((None, w, tn), lambda i, j: (layer, 0, j))
    return pl.pallas_call(
        _merge_kernel,
        grid=(t // tm, nj),
        in_specs=[
            pl.BlockSpec((tm, d), lambda i, j: (i, 0)),
            pl.BlockSpec((None, None, N_MOD, d), lambda i, j: (layer, group(i), 0, 0)),
            ospec(o_a.shape[1]), ospec(o_b.shape[1]), ospec(o_c.shape[1]),
            gspec(0), gspec(1), gspec(2),
            wspec(o_a.shape[1]), wspec(o_b.shape[1]), wspec(o_c.shape[1]),
            pl.BlockSpec((None, tn, d), lambda i, j: (layer, j, 0)),
        ],
        out_specs=pl.BlockSpec((tm, d), lambda i, j: (i, 0)),
        out_shape=jax.ShapeDtypeStruct((t, d), F32),
        compiler_params=_params(("parallel", "arbitrary")),
    )(x, mods, o_a, o_b, o_c, z, z, z, w_proj_a, w_proj_b, w_proj_c, w_out)


def kernel(x_prompt, x_sample, cache_na_k, cache_na_v, state_hgrn, state_lru, c, c_ctx, mod_w, mod_b, norm_g, ffn1_w_up, ffn1_w_down, ffn2_w_up, ffn2_w_down, w_in, hgrn_lb_logits, hgrn_norm_g, na_rpb, lru_conv_w, lru_conv_b, lru_w_a, lru_b_a, lru_w_x, lru_b_x, lru_lambda, w_proj_a, w_proj_b, w_proj_c, w_out, final_norm_g):
    b_ctx, n_ctx, d = x_prompt.shape
    b_lat, n_lat, _ = x_sample.shape
    depth = mod_w.shape[0]
    t_ctx, t_lat = b_ctx * n_ctx, b_lat * n_lat
    hg_w = HG_HEADS * HG_DK
    na_w = NA_HEADS * NA_HEAD_DIM
    lru_w = lru_conv_b.shape[1]
    na_col0 = 5 * hg_w
    lru_col0 = na_col0 + 3 * na_w
    gate_col0 = lru_col0 + 2 * lru_w
    lat_block0 = t_ctx // n_lat

    x = jnp.concatenate([x_prompt.reshape(t_ctx, d), x_sample.reshape(t_lat, d)], axis=0)
    cond = jnp.concatenate([c_ctx[None], c, jnp.zeros((N_COND_ROWS - 1 - b_lat, d), F32)], axis=0)
    mods = _mods_call(cond, mod_w, mod_b).reshape(depth, N_COND_ROWS, N_MOD, d)

    rows = n_lat // GRID_W
    kh = min(NA_KH, rows)
    wa_bd = _block_diag_pairs(lru_w_a)
    wx_bd = _block_diag_pairs(lru_w_x)
    cache_k = cache_na_k.reshape(b_lat, depth, cache_na_k.shape[2], na_w)
    cache_v = cache_na_v.reshape(b_lat, depth, cache_na_v.shape[2], na_w)
    s0_t = jnp.swapaxes(state_hgrn, -1, -2)
    h0_t = jnp.transpose(state_lru, (1, 2, 0, 3))

    ks, vs, hgs, lrus = [], [], [], []
    for l in range(depth):
        last = l == depth - 1
        x = _ffn_call(x, mods, norm_g, ffn1_w_up, ffn1_w_down, layer=l, slot=0, t_ctx=t_ctx, n_lat=n_lat)
        z = _inproj_call(x, mods, norm_g, w_in, layer=l, t_ctx=t_ctx, n_lat=n_lat)

        oa_c, s_c = _hgrn_call(z, hgrn_lb_logits, hgrn_norm_g, None, layer=l, n_seq=b_ctx, n_tok=n_ctx,
                               row_block0=0)
        oa_l, _ = _hgrn_call(z, hgrn_lb_logits, hgrn_norm_g, s0_t[:, l], layer=l, n_seq=b_lat, n_tok=n_lat,
                             row_block0=lat_block0)
        ob_c = _ctx_attn_call(z, n_seq=b_ctx, n_tok=n_ctx, col0=na_col0)
        bias = _na_bias(na_rpb[l], rows, kh)
        ob_l = _na_call(z, cache_k, cache_v, bias, layer=l, n_seq=b_lat, n_tok=n_lat,
                        row_block0=lat_block0, col_block0=na_col0 // NA_HEAD_DIM)
        lru_args = (lru_conv_w, lru_conv_b, wa_bd, lru_b_a, wx_bd, lru_b_x, lru_lambda)
        oc_c, h_c = _lru_call(z, *lru_args, None, layer=l, n_seq=b_ctx, n_tok=n_ctx, row_block0=0,
                              col_block0=lru_col0 // LANE)
        oc_l, _ = _lru_call(z, *lru_args, h0_t[l], layer=l, n_seq=b_lat, n_tok=n_lat,
                            row_block0=t_ctx // t_lat, col_block0=lru_col0 // LANE)

        o_a = jnp.concatenate([oa_c, oa_l], axis=0)
        o_b = jnp.concatenate([ob_c, ob_l], axis=0)
        o_c = jnp.concatenate([oc_c, oc_l], axis=0)
        x = _merge_call(x, mods, o_a, o_b, o_c, z, w_proj_a, w_proj_b, w_proj_c, w_out, layer=l,
                        t_ctx=t_ctx, n_lat=n_lat, gate_col0=gate_col0)
        x = _ffn_call(x, mods, norm_g, ffn2_w_up, ffn2_w_down, layer=l, slot=2, t_ctx=t_ctx, n_lat=n_lat,
                      final_g=final_norm_g if last else None)

        ks.append(z[:t_ctx, na_col0 + na_w:na_col0 + 2 * na_w].reshape(b_ctx, n_ctx, NA_HEADS, NA_HEAD_DIM))
        vs.append(z[:t_ctx, na_col0 + 2 * na_w:na_col0 + 3 * na_w].reshape(b_ctx, n_ctx, NA_HEADS, NA_HEAD_DIM))
        hgs.append(jnp.swapaxes(s_c, -1, -2))
        lrus.append(jnp.transpose(h_c, (1, 0, 2)))

    y_prompt = x[:t_ctx].reshape(b_ctx, n_ctx, d)
    y_sample = x[t_ctx:].reshape(b_lat, n_lat, d)
    return (y_prompt, y_sample, jnp.stack(ks, axis=1), jnp.stack(vs, axis=1),
            jnp.stack(hgs, axis=1), jnp.stack(lrus, axis=1))
```

```python
import functools

import jax
import jax.numpy as jnp
from jax import lax
from jax.experimental import pallas as pl
from jax.experimental.pallas import tpu as pltpu

F32 = jnp.float32
BF16 = jnp.bfloat16

EPS = 1e-6
NEG = -1e30
N_MOD = 9
N_COND_ROWS = 8
HG_HEADS = 4
HG_DK = 128
HG_CHUNK = 16
NA_HEADS = 8
NA_HEAD_DIM = 128
NA_KH = 8
NA_KW = 16
NA_QBW = 8
NA_BAND = NA_QBW + NA_KW
GRID_W = 64
LRU_BLOCKS = 8
LRU_CONV = 4
LRU_C = 8.0
LANE = 128
VMEM_LIMIT = 56 * 1024 * 1024


def _params(sem, vmem=VMEM_LIMIT):
    return pltpu.CompilerParams(dimension_semantics=sem, vmem_limit_bytes=vmem)


def _silu(x):
    return x * jax.nn.sigmoid(x)


def _expm1(x):
    u = jnp.exp(x)
    um1 = u - 1.0
    y = um1 * x / jnp.where(u == 1.0, 1.0, jnp.log(u))
    return jnp.where(u == 1.0, x, jnp.where(um1 == -1.0, -1.0, y))


def _dot(a, b):
    return jnp.dot(a, b, preferred_element_type=F32)


def _dot_nt(a, b):
    return lax.dot_general(a, b, (((1,), (1,)), ((), ())), preferred_element_type=F32)


def _dot_tn(a, b):
    return lax.dot_general(a, b, (((0,), (0,)), ((), ())), preferred_element_type=F32)


def _mod_norm(x, g, shift, scale):
    y = x * lax.rsqrt(jnp.mean(x * x, axis=-1, keepdims=True) + EPS)
    return (y * g) * (1.0 + scale) + shift


def _group_index(tm, t_ctx, n_lat):
    n_ctx_tiles = t_ctx // tm
    per_seq = n_lat // tm
    return lambda i: jnp.where(i < n_ctx_tiles, 0, 1 + (i - n_ctx_tiles) // per_seq)


def _mods_kernel(c_ref, w_ref, b_ref, o_ref):
    s = _silu(c_ref[...]).astype(BF16)
    o_ref[...] = _dot(s, w_ref[...].astype(BF16)) + b_ref[...]


def _mods_call(cond, mod_w, mod_b):
    depth, d, n = mod_w.shape
    tn = 1024
    return pl.pallas_call(
        _mods_kernel,
        grid=(depth, n // tn),
        in_specs=[
            pl.BlockSpec((N_COND_ROWS, d), lambda l, j: (0, 0)),
            pl.BlockSpec((None, d, tn), lambda l, j: (l, 0, j)),
            pl.BlockSpec((None, 1, tn), lambda l, j: (l, 0, j)),
        ],
        out_specs=pl.BlockSpec((None, N_COND_ROWS, tn), lambda l, j: (l, 0, j)),
        out_shape=jax.ShapeDtypeStruct((depth, N_COND_ROWS, n), F32),
        compiler_params=_params(("parallel", "parallel")),
    )(cond, mod_w, mod_b.reshape(depth, 1, n))


ROW_CHUNK = 256
EW_CHUNK = 64


def _rows(r, n):
    return pl.ds(pl.multiple_of(r * n, n), n)


def _ffn_kernel(x_ref, mods_ref, g_ref, wa_ref, wu_ref, wd_ref, *rest, slot, final):
    if final:
        fg_ref, o_ref, h_ref, wa_s, wu_s, wd_s = rest
    else:
        o_ref, h_ref, wa_s, wu_s, wd_s = rest
    j = pl.program_id(1)
    tm = x_ref.shape[0]
    shift = mods_ref[3 * slot:3 * slot + 1, :]
    scale = mods_ref[3 * slot + 1:3 * slot + 2, :]
    gate = mods_ref[3 * slot + 2:3 * slot + 3, :]

    @pl.when(j == 0)
    def _():
        def body(r, carry):
            sl = _rows(r, EW_CHUNK)
            h_ref[sl, :] = _mod_norm(x_ref[sl, :], g_ref[...], shift, scale).astype(BF16)
            o_ref[sl, :] = jnp.zeros((EW_CHUNK, o_ref.shape[1]), F32)
            return carry
        lax.fori_loop(0, tm // EW_CHUNK, body, 0)

    wa_s[...] = wa_ref[...].astype(BF16)
    wu_s[...] = wu_ref[...].astype(BF16)
    wd_s[...] = wd_ref[...].astype(BF16)

    def body(r, carry):
        sl = _rows(r, ROW_CHUNK)
        h = h_ref[sl, :]
        a = _dot(h, wa_s[...])
        u = _dot(h, wu_s[...])
        act = (_silu(a) * u).astype(BF16)
        o_ref[sl, :] += _dot(act, wd_s[...])
        return carry
    lax.fori_loop(0, tm // ROW_CHUNK, body, 0)

    @pl.when(j == pl.num_programs(1) - 1)
    def _():
        def body(r, carry):
            sl = _rows(r, EW_CHUNK)
            y = x_ref[sl, :] + (0.5 * gate) * o_ref[sl, :]
            if final:
                y = y * lax.rsqrt(jnp.mean(y * y, axis=-1, keepdims=True) + EPS) * fg_ref[...]
            o_ref[sl, :] = y
            return carry
        lax.fori_loop(0, tm // EW_CHUNK, body, 0)


def _ffn_call(x, mods, norm_g, w_up, w_down, *, layer, slot, t_ctx, n_lat, final_g=None):
    t, d = x.shape
    f = w_down.shape[1]
    tm, tf = 1024, 256
    nf = f // tf
    group = _group_index(tm, t_ctx, n_lat)
    norm_slot = slot
    in_specs = [
        pl.BlockSpec((tm, d), lambda i, j: (i, 0)),
        pl.BlockSpec((None, None, N_MOD, d), lambda i, j: (layer, group(i), 0, 0)),
        pl.BlockSpec((None, None, 1, d), lambda i, j: (layer, norm_slot, 0, 0)),
        pl.BlockSpec((None, d, tf), lambda i, j: (layer, 0, j)),
        pl.BlockSpec((None, d, tf), lambda i, j: (layer, 0, nf + j)),
        pl.BlockSpec((None, tf, d), lambda i, j: (layer, j, 0)),
    ]
    args = [x, mods, norm_g.reshape(norm_g.shape[0], norm_g.shape[1], 1, d), w_up, w_up, w_down]
    if final_g is not None:
        in_specs.append(pl.BlockSpec((1, d), lambda i, j: (0, 0)))
        args.append(final_g.reshape(1, d))
    return pl.pallas_call(
        functools.partial(_ffn_kernel, slot=slot, final=final_g is not None),
        grid=(t // tm, nf),
        in_specs=in_specs,
        out_specs=pl.BlockSpec((tm, d), lambda i, j: (i, 0)),
        out_shape=jax.ShapeDtypeStruct((t, d), F32),
        scratch_shapes=[
            pltpu.VMEM((tm, d), BF16),
            pltpu.VMEM((d, tf), BF16),
            pltpu.VMEM((d, tf), BF16),
            pltpu.VMEM((tf, d), BF16),
        ],
        compiler_params=_params(("parallel", "arbitrary")),
    )(*args)


def _inproj_kernel(x_ref, mods_ref, g_ref, w_ref, z_ref, h_ref, w_s):
    j = pl.program_id(1)
    tm = x_ref.shape[0]
    shift = mods_ref[3:4, :]
    scale = mods_ref[4:5, :]

    @pl.when(j == 0)
    def _():
        def body(r, carry):
            sl = _rows(r, EW_CHUNK)
            h_ref[sl, :] = _mod_norm(x_ref[sl, :], g_ref[...], shift, scale).astype(BF16)
            return carry
        lax.fori_loop(0, tm // EW_CHUNK, body, 0)

    w_s[...] = w_ref[...].astype(BF16)

    def body(r, carry):
        sl = _rows(r, ROW_CHUNK)
        z_ref[sl, :] = _dot(h_ref[sl, :], w_s[...])
        return carry
    lax.fori_loop(0, tm // ROW_CHUNK, body, 0)


def _inproj_call(x, mods, norm_g, w_in, *, layer, t_ctx, n_lat):
    t, d = x.shape
    n = w_in.shape[2]
    tm, tn = 1024, 512
    group = _group_index(tm, t_ctx, n_lat)
    return pl.pallas_call(
        _inproj_kernel,
        grid=(t // tm, n // tn),
        in_specs=[
            pl.BlockSpec((tm, d), lambda i, j: (i, 0)),
            pl.BlockSpec((None, None, N_MOD, d), lambda i, j: (layer, group(i), 0, 0)),
            pl.BlockSpec((None, None, 1, d), lambda i, j: (layer, 1, 0, 0)),
            pl.BlockSpec((None, d, tn), lambda i, j: (layer, 0, j)),
        ],
        out_specs=pl.BlockSpec((tm, tn), lambda i, j: (i, j)),
        out_shape=jax.ShapeDtypeStruct((t, n), F32),
        scratch_shapes=[pltpu.VMEM((tm, d), BF16), pltpu.VMEM((d, tn), BF16)],
        compiler_params=_params(("parallel", "arbitrary")),
    )(x, mods, norm_g.reshape(norm_g.shape[0], norm_g.shape[1], 1, d), w_in)


def _log_forget(zf, log_lb, log1m_lb):
    ls = jnp.minimum(zf, 0.0) - jnp.log1p(jnp.exp(-jnp.abs(zf)))
    b = log1m_lb + ls
    hi = jnp.maximum(log_lb, b)
    return hi + jnp.log1p(jnp.exp(-jnp.abs(log_lb - b)))


def _hgrn_kernel(zq_ref, zff_ref, zfb_ref, zi_ref, zo_ref, lg_ref, ng_ref, *rest, layer, has_s0):
    if has_s0:
        s0_ref, o_ref, sfin_ref, of_s, ob_s, st_s = rest
    else:
        o_ref, sfin_ref, of_s, ob_s, st_s = rest
    n = zq_ref.shape[0]
    n_chunks = n // HG_CHUNK
    c = HG_CHUNK

    logits = lg_ref[...]
    depth = logits.shape[0]
    mx = logits[0]
    for i in range(1, depth):
        mx = jnp.maximum(mx, logits[i])
    ex = [jnp.exp(logits[i] - mx) for i in range(depth)]
    tot = ex[0]
    for i in range(1, depth):
        tot = tot + ex[i]
    lb = jnp.zeros_like(mx)
    for i in range(1, layer + 1):
        lb = lb + ex[i] / tot
    log_lb = jnp.log(lb)
    log1m_lb = jnp.log1p(-lb)

    if has_s0:
        st_s[...] = s0_ref[...]
    else:
        st_s[...] = jnp.zeros(st_s.shape, F32)

    row = lax.broadcasted_iota(jnp.int32, (c, c), 0)
    col = lax.broadcasted_iota(jnp.int32, (c, c), 1)
    tri = [(col <= row).astype(BF16), (col >= row).astype(BF16)]
    s_idx = lax.broadcasted_iota(jnp.int32, (c, LANE), 0)
    ones = jnp.ones((HG_DK, LANE), BF16)
    sel_r = lax.broadcasted_iota(jnp.int32, (c, c * c), 0)
    sel_c = lax.broadcasted_iota(jnp.int32, (c, c * c), 1)
    sel = ((sel_c >= sel_r * c) & (sel_c < sel_r * c + c)).astype(BF16)

    def chunk(d, h, ci):
        rows = _rows(ci, c)
        cols = slice(h * HG_DK, (h + 1) * HG_DK)
        q = _silu(zq_ref[rows, cols])
        v = zi_ref[rows, cols]
        zf = (zff_ref if d == 0 else zfb_ref)[rows, cols]
        lf = _log_forget(zf, log_lb[d:d + 1, cols], log1m_lb[d:d + 1, cols])
        k = -_expm1(lf)
        lf_hi = lf.astype(BF16)
        lf_lo = (lf - lf_hi.astype(F32)).astype(BF16)
        b = _dot(tri[d], lf_hi) + _dot(tri[d], lf_lo)
        edge = c - 1 if d == 0 else 0
        b_last = b[edge:edge + 1, :]
        parts = []
        for t in range(c):
            vis = (s_idx <= t) if d == 0 else (s_idx >= t)
            dec = jnp.exp(jnp.where(vis, b[t:t + 1, :] - b, NEG))
            parts.append((q[t:t + 1, :] * dec) * k)
        p = jnp.concatenate(parts, axis=0).astype(BF16)
        scores = _dot(p, ones)
        xv = (scores * jnp.concatenate([v] * c, axis=0)).astype(BF16)
        o_intra = _dot(sel, xv)
        st = st_s[d, h]
        o_inter = _dot_nt((q * jnp.exp(b)).astype(BF16), st.astype(BF16))
        kd = (k * jnp.exp(b_last - b)).astype(BF16)
        st_s[d, h] = st * jnp.exp(b_last) + _dot_tn(v.astype(BF16), kd)
        (of_s if d == 0 else ob_s)[rows, cols] = o_intra + o_inter

    def body(i, carry):
        for h in range(HG_HEADS):
            chunk(0, h, i)
            chunk(1, h, n_chunks - 1 - i)
        return carry
    lax.fori_loop(0, n_chunks, body, 0)

    sfin_ref[...] = st_s[...]

    def fin(r, carry):
        rows = _rows(r, EW_CHUNK)
        for h in range(HG_HEADS):
            cols = slice(h * HG_DK, (h + 1) * HG_DK)
            o = of_s[rows, cols] + ob_s[rows, cols]
            o = o * lax.rsqrt(jnp.mean(o * o, axis=-1, keepdims=True) + EPS) * ng_ref[h:h + 1, :]
            o_ref[rows, cols] = (o * _silu(zo_ref[rows, cols])).astype(o_ref.dtype)
        return carry
    lax.fori_loop(0, n // EW_CHUNK, fin, 0)


def _hgrn_call(z, lb_logits, hgrn_norm_g, s0_t, *, layer, n_seq, n_tok, row_block0):
    w = HG_HEADS * HG_DK
    depth = lb_logits.shape[0]
    zspec = lambda k: pl.BlockSpec((n_tok, w), lambda s: (row_block0 + s, k))
    in_specs = [zspec(0), zspec(1), zspec(2), zspec(3), zspec(4),
                pl.BlockSpec((depth, 2, w), lambda s: (0, 0, 0)),
                pl.BlockSpec((None, HG_HEADS, HG_DK), lambda s: (layer, 0, 0))]
    args = [z, z, z, z, z, lb_logits, hgrn_norm_g]
    st_spec = pl.BlockSpec((None, 2, HG_HEADS, HG_DK, HG_DK), lambda s: (s, 0, 0, 0, 0))
    if s0_t is not None:
        in_specs.append(st_spec)
        args.append(s0_t)
    return pl.pallas_call(
        functools.partial(_hgrn_kernel, layer=layer, has_s0=s0_t is not None),
        grid=(n_seq,),
        in_specs=in_specs,
        out_specs=[pl.BlockSpec((n_tok, w), lambda s: (s, 0)), st_spec],
        out_shape=[jax.ShapeDtypeStruct((n_seq * n_tok, w), BF16),
                   jax.ShapeDtypeStruct((n_seq, 2, HG_HEADS, HG_DK, HG_DK), F32)],
        scratch_shapes=[pltpu.VMEM((n_tok, w), F32), pltpu.VMEM((n_tok, w), F32),
                        pltpu.VMEM((2, HG_HEADS, HG_DK, HG_DK), F32)],
        compiler_params=_params(("parallel",)),
    )(*args)


def _ctx_attn_kernel(q0_ref, q1_ref, k0_ref, k1_ref, v0_ref, v1_ref, o_ref, *, scale):
    q_refs, k_refs, v_refs = (q0_ref, q1_ref), (k0_ref, k1_ref), (v0_ref, v1_ref)
    heads_per_block = q0_ref.shape[1] // NA_HEAD_DIM
    for h in range(NA_HEADS):
        blk, off = divmod(h, heads_per_block)
        cols = slice(off * NA_HEAD_DIM, (off + 1) * NA_HEAD_DIM)
        s = _dot_nt(q_refs[blk][:, cols].astype(BF16), k_refs[blk][:, cols].astype(BF16)) * scale
        p = jnp.exp(s - jnp.max(s, axis=-1, keepdims=True))
        p = p / jnp.sum(p, axis=-1, keepdims=True)
        o = _dot(p.astype(BF16), v_refs[blk][:, cols].astype(BF16))
        o_ref[:, h * NA_HEAD_DIM:(h + 1) * NA_HEAD_DIM] = o.astype(o_ref.dtype)


def _ctx_attn_call(z, *, n_seq, n_tok, col0):
    w = NA_HEADS * NA_HEAD_DIM
    half = w // 2
    zspec = lambda k: pl.BlockSpec((n_tok, half), lambda s: (s, col0 // half + k))
    return pl.pallas_call(
        functools.partial(_ctx_attn_kernel, scale=NA_HEAD_DIM ** -0.5),
        grid=(n_seq,),
        in_specs=[zspec(k) for k in range(6)],
        out_specs=pl.BlockSpec((n_tok, w), lambda s: (s, 0)),
        out_shape=jax.ShapeDtypeStruct((n_seq * n_tok, w), BF16),
        compiler_params=_params(("parallel",)),
    )(z, z, z, z, z, z)


def _na_kernel(q_ref, k_ref, v_ref, kc_ref, vc_ref, bias_ref, o_ref, *, scale, rows, kh):
    kc = kc_ref[...].astype(BF16)
    vc = vc_ref[...].astype(BF16)
    for r in range(rows):
        start = min(max(r - kh // 2, 0), rows - kh) * GRID_W
        q = q_ref[r * GRID_W:(r + 1) * GRID_W, :].astype(BF16)
        k_loc = k_ref[start:start + kh * GRID_W, :].astype(BF16)
        v_loc = v_ref[start:start + kh * GRID_W, :].astype(BF16)
        s_loc = _dot_nt(q, k_loc) * scale + bias_ref[r]
        s_ctx = _dot_nt(q, kc) * scale
        m = jnp.maximum(jnp.max(s_loc, axis=-1, keepdims=True), jnp.max(s_ctx, axis=-1, keepdims=True))
        p_loc = jnp.exp(s_loc - m)
        p_ctx = jnp.exp(s_ctx - m)
        den = jnp.sum(p_loc, axis=-1, keepdims=True) + jnp.sum(p_ctx, axis=-1, keepdims=True)
        o = _dot((p_loc / den).astype(BF16), v_loc) + _dot((p_ctx / den).astype(BF16), vc)
        o_ref[r * GRID_W:(r + 1) * GRID_W, :] = o.astype(o_ref.dtype)


def _na_bias(rpb, rows, kh):
    heads = rpb.shape[0]
    qcol = jnp.arange(GRID_W)
    kcol = jnp.arange(GRID_W)
    win_start = jnp.clip(qcol - NA_KW // 2, 0, GRID_W - NA_KW)
    in_win = (kcol[None, :] >= win_start[:, None]) & (kcol[None, :] < win_start[:, None] + NA_KW)
    lo = GRID_W - NA_KW
    pad = jnp.pad(rpb.astype(F32), ((0, 0), (0, 0), (lo, lo)))
    toep = jnp.stack([pad[:, :, GRID_W - 1 - q:2 * GRID_W - 1 - q] for q in range(GRID_W)], axis=2)
    toep = jnp.where(in_win[None, None], toep, NEG)
    per_row = []
    for r in range(rows):
        dy0 = min(max(r - kh // 2, 0), rows - kh) - r + NA_KH - 1
        per_row.append(toep[:, dy0:dy0 + kh])
    bias = jnp.stack(per_row, axis=1)
    return jnp.transpose(bias, (0, 1, 3, 2, 4)).reshape(heads, rows, GRID_W, kh * GRID_W)


def _na_call(z, cache_k, cache_v, bias, *, layer, n_seq, n_tok, row_block0, col_block0):
    rows = n_tok // GRID_W
    kh = min(NA_KH, rows)
    dh = NA_HEAD_DIM
    past = cache_k.shape[2]
    zspec = lambda k: pl.BlockSpec((n_tok, dh), lambda b, h: (row_block0 + b, col_block0 + k * NA_HEADS + h))
    cspec = pl.BlockSpec((None, None, past, dh), lambda b, h: (b, layer, 0, h))
    return pl.pallas_call(
        functools.partial(_na_kernel, scale=dh ** -0.5, rows=rows, kh=kh),
        grid=(n_seq, NA_HEADS),
        in_specs=[zspec(0), zspec(1), zspec(2), cspec, cspec,
                  pl.BlockSpec((None, rows, GRID_W, kh * GRID_W), lambda b, h: (h, 0, 0, 0))],
        out_specs=pl.BlockSpec((n_tok, dh), lambda b, h: (b, h)),
        out_shape=jax.ShapeDtypeStruct((n_seq * n_tok, NA_HEADS * dh), BF16),
        compiler_params=_params(("parallel", "parallel")),
    )(z, z, z, cache_k, cache_v, bias)


def _lru_kernel(zx_ref, zg_ref, cw_ref, cb_ref, wa_ref, ba_ref, wx_ref, bx_ref, lam_ref, *rest,
                n_seq, n_tok, has_h0):
    if has_h0:
        h0_ref, y_ref, hfin_ref, a_s, u_s = rest
    else:
        y_ref, hfin_ref, a_s, u_s = rest
    tpos = lax.broadcasted_iota(jnp.int32, (n_tok, LANE), 0)
    neg_lam = -lam_ref[...]
    softplus = jnp.maximum(neg_lam, 0.0) + jnp.log1p(jnp.exp(-jnp.abs(neg_lam)))
    left = LRU_CONV // 2

    def gates(g, carry):
        rows = _rows(g, n_tok)
        zx = zx_ref[rows, :]
        x = cb_ref[...] + cw_ref[left:left + 1, :] * zx
        for j in range(LRU_CONV):
            off = j - left
            if off == 0:
                continue
            shifted = pltpu.roll(zx, (-off) % n_tok, 0)
            valid = (tpos + off >= 0) & (tpos + off < n_tok)
            x = x + cw_ref[j:j + 1, :] * jnp.where(valid, shifted, 0.0)
        xb = x.astype(BF16)
        for d in range(2):
            r_gate = jax.nn.sigmoid(_dot(xb, wa_ref[d].astype(BF16)) + ba_ref[d:d + 1, :])
            i_gate = jax.nn.sigmoid(_dot(xb, wx_ref[d].astype(BF16)) + bx_ref[d:d + 1, :])
            log_a = (-LRU_C * r_gate) * softplus[d:d + 1, :]
            a_s[d, rows, :] = jnp.exp(log_a)
            u_s[d, rows, :] = jnp.sqrt(-_expm1(2.0 * log_a)) * (i_gate * x)
        return carry
    lax.fori_loop(0, n_seq, gates, 0)

    if has_h0:
        h_init = (h0_ref[0], h0_ref[1])
    else:
        h_init = (jnp.zeros((n_seq, LANE), F32), jnp.zeros((n_seq, LANE), F32))

    def step(t, carry):
        hf, hb = carry
        fw = pl.ds(t, n_seq, stride=n_tok)
        bw = pl.ds(n_tok - 1 - t, n_seq, stride=n_tok)
        hf = a_s[0, fw, :] * hf + u_s[0, fw, :]
        hb = a_s[1, bw, :] * hb + u_s[1, bw, :]
        u_s[0, fw, :] = hf
        u_s[1, bw, :] = hb
        return hf, hb
    hf, hb = lax.fori_loop(0, n_tok, step, h_init)
    hfin_ref[0] = hf
    hfin_ref[1] = hb

    def fin(g, carry):
        rows = _rows(g, n_tok)
        y = (u_s[0, rows, :] + u_s[1, rows, :]) * jax.nn.gelu(zg_ref[rows, :], approximate=True)
        y_ref[rows, :] = y.astype(y_ref.dtype)
        return carry
    lax.fori_loop(0, n_seq, fin, 0)


def _block_diag_pairs(w):
    depth, nd, nb, bw, _ = w.shape
    w = w.reshape(depth, nd, nb // 2, 2, bw, bw)
    eye = jnp.eye(2, dtype=w.dtype)
    out = w[:, :, :, :, :, None, :] * eye[None, None, None, :, None, :, None]
    return out.reshape(depth, nd, nb // 2, 2 * bw, 2 * bw)


def _lru_call(z, conv_w, conv_b, wa_bd, b_a, wx_bd, b_x, lam, h0_t, *, layer, n_seq, n_tok, row_block0,
              col_block0):
    w = conv_b.shape[1]
    n_cb = w // LANE
    rows = n_seq * n_tok
    zspec = lambda k: pl.BlockSpec((rows, LANE), lambda cb: (row_block0, col_block0 + k * n_cb + cb))
    vec2 = pl.BlockSpec((None, 2, LANE), lambda cb: (layer, 0, cb))
    wspec = pl.BlockSpec((None, 2, None, LANE, LANE), lambda cb: (layer, 0, cb, 0, 0))
    hspec = pl.BlockSpec((2, n_seq, LANE), lambda cb: (0, 0, cb))
    in_specs = [zspec(0), zspec(1),
                pl.BlockSpec((None, LRU_CONV, LANE), lambda cb: (layer, 0, cb)),
                pl.BlockSpec((None, 1, LANE), lambda cb: (layer, 0, cb)),
                wspec, vec2, wspec, vec2, vec2]
    args = [z, z, conv_w, conv_b.reshape(conv_b.shape[0], 1, w), wa_bd, b_a, wx_bd, b_x, lam]
    if h0_t is not None:
        in_specs.append(hspec)
        args.append(h0_t)
    return pl.pallas_call(
        functools.partial(_lru_kernel, n_seq=n_seq, n_tok=n_tok, has_h0=h0_t is not None),
        grid=(n_cb,),
        in_specs=in_specs,
        out_specs=[pl.BlockSpec((rows, LANE), lambda cb: (0, cb)), hspec],
        out_shape=[jax.ShapeDtypeStruct((rows, w), BF16), jax.ShapeDtypeStruct((2, n_seq, w), F32)],
        scratch_shapes=[pltpu.VMEM((2, rows, LANE), F32), pltpu.VMEM((2, rows, LANE), F32)],
        compiler_params=_params(("parallel",)),
    )(*args)


def _merge_kernel(x_ref, mods_ref, oa_ref, ob_ref, oc_ref, ga_ref, gb_ref, gc_ref,
                  wa_ref, wb_ref, wc_ref, wo_ref, o_ref):
    j = pl.program_id(1)
    tm = x_ref.shape[0]
    gate = mods_ref[5:6, :]

    @pl.when(j == 0)
    def _():
        o_ref[...] = jnp.zeros(o_ref.shape, F32)

    def branch(o_r, w_r, g_r):
        return jax.nn.sigmoid(g_r[...]) * _dot(o_r[...], w_r[...].astype(BF16))

    m = branch(oa_ref, wa_ref, ga_ref) + branch(ob_ref, wb_ref, gb_ref) + branch(oc_ref, wc_ref, gc_ref)
    o_ref[...] += _dot(m.astype(BF16), wo_ref[...].astype(BF16))

    @pl.when(j == pl.num_programs(1) - 1)
    def _():
        def body(r, carry):
            sl = _rows(r, EW_CHUNK)
            o_ref[sl, :] = x_ref[sl, :] + gate * o_ref[sl, :]
            return carry
        lax.fori_loop(0, tm // EW_CHUNK, body, 0)


def _merge_call(x, mods, o_a, o_b, o_c, z, w_proj_a, w_proj_b, w_proj_c, w_out, *, layer, t_ctx, n_lat,
                gate_col0):
    t, d = x.shape
    tm, tn = 512, 256
    nj = d // tn
    g0 = gate_col0 // tn
    group = _group_index(tm, t_ctx, n_lat)
    gspec = lambda k: pl.BlockSpec((tm, tn), lambda i, j: (i, g0 + k * nj + j))
    ospec = lambda w: pl.BlockSpec((tm, w), lambda i, j: (i, 0))
    wspec = lambda w: pl.BlockSpec((None, w, tn), lambda i, j: (layer, 0, j))
    return pl.pallas_call(
        _merge_kernel,
        grid=(t // tm, nj),
        in_specs=[
            pl.BlockSpec((tm, d), lambda i, j: (i, 0)),
            pl.BlockSpec((None, None, N_MOD, d), lambda i, j: (layer, group(i), 0, 0)),
            ospec(o_a.shape[1]), ospec(o_b.shape[1]), ospec(o_c.shape[1]),
            gspec(0), gspec(1), gspec(2),
            wspec(o_a.shape[1]), wspec(o_b.shape[1]), wspec(o_c.shape[1]),
            pl.BlockSpec((None, tn, d), lambda i, j: (layer, j, 0)),
        ],
        out_specs=pl.BlockSpec((tm, d), lambda i, j: (i, 0)),
        out_shape=jax.ShapeDtypeStruct((t, d), F32),
        compiler_params=_params(("parallel", "arbitrary")),
    )(x, mods, o_a, o_b, o_c, z, z, z, w_proj_a, w_proj_b, w_proj_c, w_out)


def kernel(x_prompt, x_sample, cache_na_k, cache_na_v, state_hgrn, state_lru, c, c_ctx, mod_w, mod_b, norm_g, ffn1_w_up, ffn1_w_down, ffn2_w_up, ffn2_w_down, w_in, hgrn_lb_logits, hgrn_norm_g, na_rpb, lru_conv_w, lru_conv_b, lru_w_a, lru_b_a, lru_w_x, lru_b_x, lru_lambda, w_proj_a, w_proj_b, w_proj_c, w_out, final_norm_g):
    b_ctx, n_ctx, d = x_prompt.shape
    b_lat, n_lat, _ = x_sample.shape
    depth = mod_w.shape[0]
    t_ctx, t_lat = b_ctx * n_ctx, b_lat * n_lat
    hg_w = HG_HEADS * HG_DK
    na_w = NA_HEADS * NA_HEAD_DIM
    lru_w = lru_conv_b.shape[1]
    na_col0 = 5 * hg_w
    lru_col0 = na_col0 + 3 * na_w
    gate_col0 = lru_col0 + 2 * lru_w
    lat_block0 = t_ctx // n_lat

    x = jnp.concatenate([x_prompt.reshape(t_ctx, d), x_sample.reshape(t_lat, d)], axis=0)
    cond = jnp.concatenate([c_ctx[None], c, jnp.zeros((N_COND_ROWS - 1 - b_lat, d), F32)], axis=0)
    mods = _mods_call(cond, mod_w, mod_b).reshape(depth, N_COND_ROWS, N_MOD, d)

    rows = n_lat // GRID_W
    kh = min(NA_KH, rows)
    wa_bd = _block_diag_pairs(lru_w_a)
    wx_bd = _block_diag_pairs(lru_w_x)
    cache_k = cache_na_k.reshape(b_lat, depth, cache_na_k.shape[2], na_w)
    cache_v = cache_na_v.reshape(b_lat, depth, cache_na_v.shape[2], na_w)
    s0_t = jnp.swapaxes(state_hgrn, -1, -2)
    h0_t = jnp.transpose(state_lru, (1, 2, 0, 3))

    ks, vs, hgs, lrus = [], [], [], []
    for l in range(depth):
        last = l == depth - 1
        x = _ffn_call(x, mods, norm_g, ffn1_w_up, ffn1_w_down, layer=l, slot=0, t_ctx=t_ctx, n_lat=n_lat)
        z = _inproj_call(x, mods, norm_g, w_in, layer=l, t_ctx=t_ctx, n_lat=n_lat)

        oa_c, s_c = _hgrn_call(z, hgrn_lb_logits, hgrn_norm_g, None, layer=l, n_seq=b_ctx, n_tok=n_ctx,
                               row_block0=0)
        oa_l, _ = _hgrn_call(z, hgrn_lb_logits, hgrn_norm_g, s0_t[:, l], layer=l, n_seq=b_lat, n_tok=n_lat,
                             row_block0=lat_block0)
        ob_c = _ctx_attn_call(z, n_seq=b_ctx, n_tok=n_ctx, col0=na_col0)
        bias = _na_bias(na_rpb[l], rows, kh)
        ob_l = _na_call(z, cache_k, cache_v, bias, layer=l, n_seq=b_lat, n_tok=n_lat,
                        row_block0=lat_block0, col_block0=na_col0 // NA_HEAD_DIM)
        lru_args = (lru_conv_w, lru_conv_b, wa_bd, lru_b_a, wx_bd, lru_b_x, lru_lambda)
        oc_c, h_c = _lru_call(z, *lru_args, None, layer=l, n_seq=b_ctx, n_tok=n_ctx, row_block0=0,
                              col_block0=lru_col0 // LANE)
        oc_l, _ = _lru_call(z, *lru_args, h0_t[l], layer=l, n_seq=b_lat, n_tok=n_lat,
                            row_block0=t_ctx // t_lat, col_block0=lru_col0 // LANE)

        o_a = jnp.concatenate([oa_c, oa_l], axis=0)
        o_b = jnp.concatenate([ob_c, ob_l], axis=0)
        o_c = jnp.concatenate([oc_c, oc_l], axis=0)
        x = _merge_call(x, mods, o_a, o_b, o_c, z, w_proj_a, w_proj_b, w_proj_c, w_out, layer=l,
                        t_ctx=t_ctx, n_lat=n_lat, gate_col0=gate_col0)
        x = _ffn_call(x, mods, norm_g, ffn2_w_up, ffn2_w_down, layer=l, slot=2, t_ctx=t_ctx, n_lat=n_lat,
                      final_g=final_norm_g if last else None)

        ks.append(z[:t_ctx, na_col0 + na_w:na_col0 + 2 * na_w].reshape(b_ctx, n_ctx, NA_HEADS, NA_HEAD_DIM))
        vs.append(z[:t_ctx, na_col0 + 2 * na_w:na_col0 + 3 * na_w].reshape(b_ctx, n_ctx, NA_HEADS, NA_HEAD_DIM))
        hgs.append(jnp.swapaxes(s_c, -1, -2))
        lrus.append(jnp.transpose(h_c, (1, 0, 2)))

    y_prompt = x[:t_ctx].reshape(b_ctx, n_ctx, d)
    y_sample = x[t_ctx:].reshape(b_lat, n_lat, d)
    return (y_prompt, y_sample, jnp.stack(ks, axis=1), jnp.stack(vs, axis=1),
            jnp.stack(hgs, axis=1), jnp.stack(lrus, axis=1))
```

```python
import functools

import jax
import jax.numpy as jnp
from jax import lax
from jax.experimental import pallas as pl
from jax.experimental.pallas import tpu as pltpu

F32 = jnp.float32
BF16 = jnp.bfloat16

EPS = 1e-6
NEG = -1e30
N_MOD = 9
N_COND_ROWS = 8
HG_HEADS = 4
HG_DK = 128
HG_CHUNK = 16
NA_HEADS = 8
NA_HEAD_DIM = 128
NA_KH = 8
NA_KW = 16
NA_QBW = 8
NA_BAND = NA_QBW + NA_KW
GRID_W = 64
LRU_BLOCKS = 8
LRU_CONV = 4
LRU_C = 8.0
LANE = 128
VMEM_LIMIT = 56 * 1024 * 1024


def _params(sem, vmem=VMEM_LIMIT):
    return pltpu.CompilerParams(dimension_semantics=sem, vmem_limit_bytes=vmem)


def _silu(x):
    return x * jax.nn.sigmoid(x)


def _expm1(x):
    u = jnp.exp(x)
    um1 = u - 1.0
    y = um1 * x / jnp.where(u == 1.0, 1.0, jnp.log(u))
    return jnp.where(u == 1.0, x, jnp.where(um1 == -1.0, -1.0, y))


def _dot(a, b):
    return jnp.dot(a, b, preferred_element_type=F32)


def _dot_nt(a, b):
    return lax.dot_general(a, b, (((1,), (1,)), ((), ())), preferred_element_type=F32)


def _dot_tn(a, b):
    return lax.dot_general(a, b, (((0,), (0,)), ((), ())), preferred_element_type=F32)


def _mod_norm(x, g, shift, scale):
    y = x * lax.rsqrt(jnp.mean(x * x, axis=-1, keepdims=True) + EPS)
    return (y * g) * (1.0 + scale) + shift


def _group_index(tm, t_ctx, n_lat):
    n_ctx_tiles = t_ctx // tm
    per_seq = n_lat // tm
    return lambda i: jnp.where(i < n_ctx_tiles, 0, 1 + (i - n_ctx_tiles) // per_seq)


def _mods_kernel(c_ref, w_ref, b_ref, o_ref):
    s = _silu(c_ref[...]).astype(BF16)
    o_ref[...] = _dot(s, w_ref[...].astype(BF16)) + b_ref[...]


def _mods_call(cond, mod_w, mod_b):
    depth, d, n = mod_w.shape
    tn = 1024
    return pl.pallas_call(
        _mods_kernel,
        grid=(depth, n // tn),
        in_specs=[
            pl.BlockSpec((N_COND_ROWS, d), lambda l, j: (0, 0)),
            pl.BlockSpec((None, d, tn), lambda l, j: (l, 0, j)),
            pl.BlockSpec((None, 1, tn), lambda l, j: (l, 0, j)),
        ],
        out_specs=pl.BlockSpec((None, N_COND_ROWS, tn), lambda l, j: (l, 0, j)),
        out_shape=jax.ShapeDtypeStruct((depth, N_COND_ROWS, n), F32),
        compiler_params=_params(("parallel", "parallel")),
    )(cond, mod_w, mod_b.reshape(depth, 1, n))


ROW_CHUNK = 256
EW_CHUNK = 64


def _rows(r, n):
    return pl.ds(pl.multiple_of(r * n, n), n)


def _ffn_kernel(x_ref, mods_ref, g_ref, wa_ref, wu_ref, wd_ref, *rest, slot, final):
    if final:
        fg_ref, o_ref, h_ref, wa_s, wu_s, wd_s = rest
    else:
        o_ref, h_ref, wa_s, wu_s, wd_s = rest
    j = pl.program_id(1)
    tm = x_ref.shape[0]
    shift = mods_ref[3 * slot:3 * slot + 1, :]
    scale = mods_ref[3 * slot + 1:3 * slot + 2, :]
    gate = mods_ref[3 * slot + 2:3 * slot + 3, :]

    @pl.when(j == 0)
    def _():
        def body(r, carry):
            sl = _rows(r, EW_CHUNK)
            h_ref[sl, :] = _mod_norm(x_ref[sl, :], g_ref[...], shift, scale).astype(BF16)
            o_ref[sl, :] = jnp.zeros((EW_CHUNK, o_ref.shape[1]), F32)
            return carry
        lax.fori_loop(0, tm // EW_CHUNK, body, 0)

    wa_s[...] = wa_ref[...].astype(BF16)
    wu_s[...] = wu_ref[...].astype(BF16)
    wd_s[...] = wd_ref[...].astype(BF16)

    for r in range(tm // ROW_CHUNK):
        sl = slice(r * ROW_CHUNK, (r + 1) * ROW_CHUNK)
        h = h_ref[sl, :]
        a = _dot(h, wa_s[...])
        u = _dot(h, wu_s[...])
        act = (_silu(a) * u).astype(BF16)
        o_ref[sl, :] += _dot(act, wd_s[...])

    @pl.when(j == pl.num_programs(1) - 1)
    def _():
        def body(r, carry):
            sl = _rows(r, EW_CHUNK)
            y = x_ref[sl, :] + (0.5 * gate) * o_ref[sl, :]
            if final:
                y = y * lax.rsqrt(jnp.mean(y * y, axis=-1, keepdims=True) + EPS) * fg_ref[...]
            o_ref[sl, :] = y
            return carry
        lax.fori_loop(0, tm // EW_CHUNK, body, 0)


def _ffn_call(x, mods, norm_g, w_up, w_down, *, layer, slot, t_ctx, n_lat, final_g=None):
    t, d = x.shape
    f = w_down.shape[1]
    tm, tf = 1024, 256
    nf = f // tf
    group = _group_index(tm, t_ctx, n_lat)
    norm_slot = slot
    in_specs = [
        pl.BlockSpec((tm, d), lambda i, j: (i, 0)),
        pl.BlockSpec((None, None, N_MOD, d), lambda i, j: (layer, group(i), 0, 0)),
        pl.BlockSpec((None, None, 1, d), lambda i, j: (layer, norm_slot, 0, 0)),
        pl.BlockSpec((None, d, tf), lambda i, j: (layer, 0, j)),
        pl.BlockSpec((None, d, tf), lambda i, j: (layer, 0, nf + j)),
        pl.BlockSpec((None, tf, d), lambda i, j: (layer, j, 0)),
    ]
    args = [x, mods, norm_g.reshape(norm_g.shape[0], norm_g.shape[1], 1, d), w_up, w_up, w_down]
    if final_g is not None:
        in_specs.append(pl.BlockSpec((1, d), lambda i, j: (0, 0)))
        args.append(final_g.reshape(1, d))
    return pl.pallas_call(
        functools.partial(_ffn_kernel, slot=slot, final=final_g is not None),
        grid=(t // tm, nf),
        in_specs=in_specs,
        out_specs=pl.BlockSpec((tm, d), lambda i, j: (i, 0)),
        out_shape=jax.ShapeDtypeStruct((t, d), F32),
        scratch_shapes=[
            pltpu.VMEM((tm, d), BF16),
            pltpu.VMEM((d, tf), BF16),
            pltpu.VMEM((d, tf), BF16),
            pltpu.VMEM((tf, d), BF16),
        ],
        compiler_params=_params(("parallel", "arbitrary")),
    )(*args)


def _inproj_kernel(x_ref, mods_ref, g_ref, w_ref, z_ref, h_ref, w_s):
    j = pl.program_id(1)
    tm = x_ref.shape[0]
    shift = mods_ref[3:4, :]
    scale = mods_ref[4:5, :]

    @pl.when(j == 0)
    def _():
        def body(r, carry):
            sl = _rows(r, EW_CHUNK)
            h_ref[sl, :] = _mod_norm(x_ref[sl, :], g_ref[...], shift, scale).astype(BF16)
            return carry
        lax.fori_loop(0, tm // EW_CHUNK, body, 0)

    w_s[...] = w_ref[...].astype(BF16)

    for r in range(tm // ROW_CHUNK):
        sl = slice(r * ROW_CHUNK, (r + 1) * ROW_CHUNK)
        z_ref[sl, :] = _dot(h_ref[sl, :], w_s[...])


def _inproj_call(x, mods, norm_g, w_in, *, layer, t_ctx, n_lat):
    t, d = x.shape
    n = w_in.shape[2]
    tm, tn = 1024, 512
    group = _group_index(tm, t_ctx, n_lat)
    return pl.pallas_call(
        _inproj_kernel,
        grid=(t // tm, n // tn),
        in_specs=[
            pl.BlockSpec((tm, d), lambda i, j: (i, 0)),
            pl.BlockSpec((None, None, N_MOD, d), lambda i, j: (layer, group(i), 0, 0)),
            pl.BlockSpec((None, None, 1, d), lambda i, j: (layer, 1, 0, 0)),
            pl.BlockSpec((None, d, tn), lambda i, j: (layer, 0, j)),
        ],
        out_specs=pl.BlockSpec((tm, tn), lambda i, j: (i, j)),
        out_shape=jax.ShapeDtypeStruct((t, n), F32),
        scratch_shapes=[pltpu.VMEM((tm, d), BF16), pltpu.VMEM((d, tn), BF16)],
        compiler_params=_params(("parallel", "arbitrary")),
    )(x, mods, norm_g.reshape(norm_g.shape[0], norm_g.shape[1], 1, d), w_in)


def _log_forget(zf, log_lb, log1m_lb):
    ls = jnp.minimum(zf, 0.0) - jnp.log1p(jnp.exp(-jnp.abs(zf)))
    b = log1m_lb + ls
    hi = jnp.maximum(log_lb, b)
    return hi + jnp.log1p(jnp.exp(-jnp.abs(log_lb - b)))


def _hgrn_kernel(zq_ref, zff_ref, zfb_ref, zi_ref, zo_ref, lg_ref, ng_ref, *rest, layer, has_s0):
    if has_s0:
        s0_ref, o_ref, sfin_ref, of_s, ob_s, st_s = rest
    else:
        o_ref, sfin_ref, of_s, ob_s, st_s = rest
    n = zq_ref.shape[0]
    n_chunks = n // HG_CHUNK
    c = HG_CHUNK

    logits = lg_ref[...]
    depth = logits.shape[0]
    mx = logits[0]
    for i in range(1, depth):
        mx = jnp.maximum(mx, logits[i])
    ex = [jnp.exp(logits[i] - mx) for i in range(depth)]
    tot = ex[0]
    for i in range(1, depth):
        tot = tot + ex[i]
    lb = jnp.zeros_like(mx)
    for i in range(1, layer + 1):
        lb = lb + ex[i] / tot
    log_lb = jnp.log(lb)
    log1m_lb = jnp.log1p(-lb)

    if has_s0:
        st_s[...] = s0_ref[...]
    else:
        st_s[...] = jnp.zeros(st_s.shape, F32)

    row = lax.broadcasted_iota(jnp.int32, (c, c), 0)
    col = lax.broadcasted_iota(jnp.int32, (c, c), 1)
    tri = [(col <= row).astype(BF16), (col >= row).astype(BF16)]
    s_idx = lax.broadcasted_iota(jnp.int32, (c, LANE), 0)
    ones = jnp.ones((HG_DK, LANE), BF16)
    sel_r = lax.broadcasted_iota(jnp.int32, (c, c * c), 0)
    sel_c = lax.broadcasted_iota(jnp.int32, (c, c * c), 1)
    sel = ((sel_c >= sel_r * c) & (sel_c < sel_r * c + c)).astype(BF16)

    def chunk(d, h, ci):
        rows = _rows(ci, c)
        cols = slice(h * HG_DK, (h + 1) * HG_DK)
        q = _silu(zq_ref[rows, cols])
        v = zi_ref[rows, cols]
        zf = (zff_ref if d == 0 else zfb_ref)[rows, cols]
        lf = _log_forget(zf, log_lb[d:d + 1, cols], log1m_lb[d:d + 1, cols])
        k = -_expm1(lf)
        lf_hi = lf.astype(BF16)
        lf_lo = (lf - lf_hi.astype(F32)).astype(BF16)
        b = _dot(tri[d], lf_hi) + _dot(tri[d], lf_lo)
        edge = c - 1 if d == 0 else 0
        b_last = b[edge:edge + 1, :]
        parts = []
        for t in range(c):
            vis = (s_idx <= t) if d == 0 else (s_idx >= t)
            dec = jnp.exp(jnp.where(vis, b[t:t + 1, :] - b, NEG))
            parts.append((q[t:t + 1, :] * dec) * k)
        p = jnp.concatenate(parts, axis=0).astype(BF16)
        scores = _dot(p, ones)
        xv = (scores * jnp.concatenate([v] * c, axis=0)).astype(BF16)
        o_intra = _dot(sel, xv)
        st = st_s[d, h]
        o_inter = _dot_nt((q * jnp.exp(b)).astype(BF16), st.astype(BF16))
        kd = (k * jnp.exp(b_last - b)).astype(BF16)
        st_s[d, h] = st * jnp.exp(b_last) + _dot_tn(v.astype(BF16), kd)
        (of_s if d == 0 else ob_s)[rows, cols] = o_intra + o_inter

    def body(i, carry):
        for h in range(HG_HEADS):
            chunk(0, h, i)
            chunk(1, h, n_chunks - 1 - i)
        return carry
    lax.fori_loop(0, n_chunks, body, 0)

    sfin_ref[...] = st_s[...]

    def fin(r, carry):
        rows = _rows(r, EW_CHUNK)
        for h in range(HG_HEADS):
            cols = slice(h * HG_DK, (h + 1) * HG_DK)
            o = of_s[rows, cols] + ob_s[rows, cols]
            o = o * lax.rsqrt(jnp.mean(o * o, axis=-1, keepdims=True) + EPS) * ng_ref[h:h + 1, :]
            o_ref[rows, cols] = (o * _silu(zo_ref[rows, cols])).astype(o_ref.dtype)
        return carry
    lax.fori_loop(0, n // EW_CHUNK, fin, 0)


def _hgrn_call(z, lb_logits, hgrn_norm_g, s0_t, *, layer, n_seq, n_tok, row_block0):
    w = HG_HEADS * HG_DK
    depth = lb_logits.shape[0]
    zspec = lambda k: pl.BlockSpec((n_tok, w), lambda s: (row_block0 + s, k))
    in_specs = [zspec(0), zspec(1), zspec(2), zspec(3), zspec(4),
                pl.BlockSpec((depth, 2, w), lambda s: (0, 0, 0)),
                pl.BlockSpec((None, HG_HEADS, HG_DK), lambda s: (layer, 0, 0))]
    args = [z, z, z, z, z, lb_logits, hgrn_norm_g]
    st_spec = pl.BlockSpec((None, 2, HG_HEADS, HG_DK, HG_DK), lambda s: (s, 0, 0, 0, 0))
    if s0_t is not None:
        in_specs.append(st_spec)
        args.append(s0_t)
    return pl.pallas_call(
        functools.partial(_hgrn_kernel, layer=layer, has_s0=s0_t is not None),
        grid=(n_seq,),
        in_specs=in_specs,
        out_specs=[pl.BlockSpec((n_tok, w), lambda s: (s, 0)), st_spec],
        out_shape=[jax.ShapeDtypeStruct((n_seq * n_tok, w), BF16),
                   jax.ShapeDtypeStruct((n_seq, 2, HG_HEADS, HG_DK, HG_DK), F32)],
        scratch_shapes=[pltpu.VMEM((n_tok, w), F32), pltpu.VMEM((n_tok, w), F32),
                        pltpu.VMEM((2, HG_HEADS, HG_DK, HG_DK), F32)],
        compiler_params=_params(("parallel",)),
    )(*args)


def _ctx_attn_kernel(q0_ref, q1_ref, k0_ref, k1_ref, v0_ref, v1_ref, o_ref, *, scale):
    q_refs, k_refs, v_refs = (q0_ref, q1_ref), (k0_ref, k1_ref), (v0_ref, v1_ref)
    heads_per_block = q0_ref.shape[1] // NA_HEAD_DIM
    for h in range(NA_HEADS):
        blk, off = divmod(h, heads_per_block)
        cols = slice(off * NA_HEAD_DIM, (off + 1) * NA_HEAD_DIM)
        s = _dot_nt(q_refs[blk][:, cols].astype(BF16), k_refs[blk][:, cols].astype(BF16)) * scale
        p = jnp.exp(s - jnp.max(s, axis=-1, keepdims=True))
        p = p / jnp.sum(p, axis=-1, keepdims=True)
        o = _dot(p.astype(BF16), v_refs[blk][:, cols].astype(BF16))
        o_ref[:, h * NA_HEAD_DIM:(h + 1) * NA_HEAD_DIM] = o.astype(o_ref.dtype)


def _ctx_attn_call(z, *, n_seq, n_tok, col0):
    w = NA_HEADS * NA_HEAD_DIM
    half = w // 2
    zspec = lambda k: pl.BlockSpec((n_tok, half), lambda s: (s, col0 // half + k))
    return pl.pallas_call(
        functools.partial(_ctx_attn_kernel, scale=NA_HEAD_DIM ** -0.5),
        grid=(n_seq,),
        in_specs=[zspec(k) for k in range(6)],
        out_specs=pl.BlockSpec((n_tok, w), lambda s: (s, 0)),
        out_shape=jax.ShapeDtypeStruct((n_seq * n_tok, w), BF16),
        compiler_params=_params(("parallel",)),
    )(z, z, z, z, z, z)


def _na_kernel(q_ref, k_ref, v_ref, kc_ref, vc_ref, bias_ref, o_ref, *, scale, rows, kh):
    kc = kc_ref[...].astype(BF16)
    vc = vc_ref[...].astype(BF16)
    for r in range(rows):
        start = min(max(r - kh // 2, 0), rows - kh) * GRID_W
        q = q_ref[r * GRID_W:(r + 1) * GRID_W, :].astype(BF16)
        k_loc = k_ref[start:start + kh * GRID_W, :].astype(BF16)
        v_loc = v_ref[start:start + kh * GRID_W, :].astype(BF16)
        s_loc = _dot_nt(q, k_loc) * scale + bias_ref[r]
        s_ctx = _dot_nt(q, kc) * scale
        m = jnp.maximum(jnp.max(s_loc, axis=-1, keepdims=True), jnp.max(s_ctx, axis=-1, keepdims=True))
        p_loc = jnp.exp(s_loc - m)
        p_ctx = jnp.exp(s_ctx - m)
        den = jnp.sum(p_loc, axis=-1, keepdims=True) + jnp.sum(p_ctx, axis=-1, keepdims=True)
        o = _dot((p_loc / den).astype(BF16), v_loc) + _dot((p_ctx / den).astype(BF16), vc)
        o_ref[r * GRID_W:(r + 1) * GRID_W, :] = o.astype(o_ref.dtype)


def _na_bias(rpb, rows, kh):
    heads = rpb.shape[0]
    qcol = jnp.arange(GRID_W)
    kcol = jnp.arange(GRID_W)
    win_start = jnp.clip(qcol - NA_KW // 2, 0, GRID_W - NA_KW)
    in_win = (kcol[None, :] >= win_start[:, None]) & (kcol[None, :] < win_start[:, None] + NA_KW)
    lo = GRID_W - NA_KW
    pad = jnp.pad(rpb.astype(F32), ((0, 0), (0, 0), (lo, lo)))
    toep = jnp.stack([pad[:, :, GRID_W - 1 - q:2 * GRID_W - 1 - q] for q in range(GRID_W)], axis=2)
    toep = jnp.where(in_win[None, None], toep, NEG)
    per_row = []
    for r in range(rows):
        dy0 = min(max(r - kh // 2, 0), rows - kh) - r + NA_KH - 1
        per_row.append(toep[:, dy0:dy0 + kh])
    bias = jnp.stack(per_row, axis=1)
    return jnp.transpose(bias, (0, 1, 3, 2, 4)).reshape(heads, rows, GRID_W, kh * GRID_W)


def _na_call(z, cache_k, cache_v, bias, *, layer, n_seq, n_tok, row_block0, col_block0):
    rows = n_tok // GRID_W
    kh = min(NA_KH, rows)
    dh = NA_HEAD_DIM
    past = cache_k.shape[2]
    zspec = lambda k: pl.BlockSpec((n_tok, dh), lambda b, h: (row_block0 + b, col_block0 + k * NA_HEADS + h))
    cspec = pl.BlockSpec((None, None, past, dh), lambda b, h: (b, layer, 0, h))
    return pl.pallas_call(
        functools.partial(_na_kernel, scale=dh ** -0.5, rows=rows, kh=kh),
        grid=(n_seq, NA_HEADS),
        in_specs=[zspec(0), zspec(1), zspec(2), cspec, cspec,
                  pl.BlockSpec((None, rows, GRID_W, kh * GRID_W), lambda b, h: (h, 0, 0, 0))],
        out_specs=pl.BlockSpec((n_tok, dh), lambda b, h: (b, h)),
        out_shape=jax.ShapeDtypeStruct((n_seq * n_tok, NA_HEADS * dh), BF16),
        compiler_params=_params(("parallel", "parallel")),
    )(z, z, z, cache_k, cache_v, bias)


def _lru_kernel(zx_ref, zg_ref, cw_ref, cb_ref, wa_ref, ba_ref, wx_ref, bx_ref, lam_ref, *rest,
                n_seq, n_tok, has_h0):
    if has_h0:
        h0_ref, y_ref, hfin_ref, a_s, u_s = rest
    else:
        y_ref, hfin_ref, a_s, u_s = rest
    tpos = lax.broadcasted_iota(jnp.int32, (n_tok, LANE), 0)
    neg_lam = -lam_ref[...]
    softplus = jnp.maximum(neg_lam, 0.0) + jnp.log1p(jnp.exp(-jnp.abs(neg_lam)))
    left = LRU_CONV // 2

    def gates(g, carry):
        rows = _rows(g, n_tok)
        zx = zx_ref[rows, :]
        x = cb_ref[...] + cw_ref[left:left + 1, :] * zx
        for j in range(LRU_CONV):
            off = j - left
            if off == 0:
                continue
            shifted = pltpu.roll(zx, (-off) % n_tok, 0)
            valid = (tpos + off >= 0) & (tpos + off < n_tok)
            x = x + cw_ref[j:j + 1, :] * jnp.where(valid, shifted, 0.0)
        xb = x.astype(BF16)
        for d in range(2):
            r_gate = jax.nn.sigmoid(_dot(xb, wa_ref[d].astype(BF16)) + ba_ref[d:d + 1, :])
            i_gate = jax.nn.sigmoid(_dot(xb, wx_ref[d].astype(BF16)) + bx_ref[d:d + 1, :])
            log_a = (-LRU_C * r_gate) * softplus[d:d + 1, :]
            a_s[d, rows, :] = jnp.exp(log_a)
            u_s[d, rows, :] = jnp.sqrt(-_expm1(2.0 * log_a)) * (i_gate * x)
        return carry
    lax.fori_loop(0, n_seq, gates, 0)

    if has_h0:
        h_init = (h0_ref[0], h0_ref[1])
    else:
        h_init = (jnp.zeros((n_seq, LANE), F32), jnp.zeros((n_seq, LANE), F32))

    def step(t, carry):
        hf, hb = carry
        fw = pl.ds(t, n_seq, stride=n_tok)
        bw = pl.ds(n_tok - 1 - t, n_seq, stride=n_tok)
        hf = a_s[0, fw, :] * hf + u_s[0, fw, :]
        hb = a_s[1, bw, :] * hb + u_s[1, bw, :]
        u_s[0, fw, :] = hf
        u_s[1, bw, :] = hb
        return hf, hb
    hf, hb = lax.fori_loop(0, n_tok, step, h_init)
    hfin_ref[0] = hf
    hfin_ref[1] = hb

    def fin(g, carry):
        rows = _rows(g, n_tok)
        y = (u_s[0, rows, :] + u_s[1, rows, :]) * jax.nn.gelu(zg_ref[rows, :], approximate=True)
        y_ref[rows, :] = y.astype(y_ref.dtype)
        return carry
    lax.fori_loop(0, n_seq, fin, 0)


def _block_diag_pairs(w):
    depth, nd, nb, bw, _ = w.shape
    w = w.reshape(depth, nd, nb // 2, 2, bw, bw)
    eye = jnp.eye(2, dtype=w.dtype)
    out = w[:, :, :, :, :, None, :] * eye[None, None, None, :, None, :, None]
    return out.reshape(depth, nd, nb // 2, 2 * bw, 2 * bw)


def _lru_call(z, conv_w, conv_b, wa_bd, b_a, wx_bd, b_x, lam, h0_t, *, layer, n_seq, n_tok, row_block0,
              col_block0):
    w = conv_b.shape[1]
    n_cb = w // LANE
    rows = n_seq * n_tok
    zspec = lambda k: pl.BlockSpec((rows, LANE), lambda cb: (row_block0, col_block0 + k * n_cb + cb))
    vec2 = pl.BlockSpec((None, 2, LANE), lambda cb: (layer, 0, cb))
    wspec = pl.BlockSpec((None, 2, None, LANE, LANE), lambda cb: (layer, 0, cb, 0, 0))
    hspec = pl.BlockSpec((2, n_seq, LANE), lambda cb: (0, 0, cb))
    in_specs = [zspec(0), zspec(1),
                pl.BlockSpec((None, LRU_CONV, LANE), lambda cb: (layer, 0, cb)),
                pl.BlockSpec((None, 1, LANE), lambda cb: (layer, 0, cb)),
                wspec, vec2, wspec, vec2, vec2]
    args = [z, z, conv_w, conv_b.reshape(conv_b.shape[0], 1, w), wa_bd, b_a, wx_bd, b_x, lam]
    if h0_t is not None:
        in_specs.append(hspec)
        args.append(h0_t)
    return pl.pallas_call(
        functools.partial(_lru_kernel, n_seq=n_seq, n_tok=n_tok, has_h0=h0_t is not None),
        grid=(n_cb,),
        in_specs=in_specs,
        out_specs=[pl.BlockSpec((rows, LANE), lambda cb: (0, cb)), hspec],
        out_shape=[jax.ShapeDtypeStruct((rows, w), BF16), jax.ShapeDtypeStruct((2, n_seq, w), F32)],
        scratch_shapes=[pltpu.VMEM((2, rows, LANE), F32), pltpu.VMEM((2, rows, LANE), F32)],
        compiler_params=_params(("parallel",)),
    )(*args)


def _merge_kernel(x_ref, mods_ref, oa_ref, ob_ref, oc_ref, ga_ref, gb_ref, gc_ref,
                  wa_ref, wb_ref, wc_ref, wo_ref, o_ref, wa_s, wb_s, wc_s, wo_s):
    j = pl.program_id(1)
    tm = x_ref.shape[0]
    gate = mods_ref[5:6, :]

    @pl.when(j == 0)
    def _():
        def body(r, carry):
            o_ref[_rows(r, EW_CHUNK), :] = jnp.zeros((EW_CHUNK, o_ref.shape[1]), F32)
            return carry
        lax.fori_loop(0, tm // EW_CHUNK, body, 0)

    wa_s[...] = wa_ref[...].astype(BF16)
    wb_s[...] = wb_ref[...].astype(BF16)
    wc_s[...] = wc_ref[...].astype(BF16)
    wo_s[...] = wo_ref[...].astype(BF16)

    for r in range(tm // ROW_CHUNK):
        sl = slice(r * ROW_CHUNK, (r + 1) * ROW_CHUNK)

        def branch(o_r, w_s, g_r):
            return jax.nn.sigmoid(g_r[sl, :]) * _dot(o_r[sl, :], w_s[...])

        m = branch(oa_ref, wa_s, ga_ref) + branch(ob_ref, wb_s, gb_ref) + branch(oc_ref, wc_s, gc_ref)
        o_ref[sl, :] += _dot(m.astype(BF16), wo_s[...])

    @pl.when(j == pl.num_programs(1) - 1)
    def _():
        def body(r, carry):
            sl = _rows(r, EW_CHUNK)
            o_ref[sl, :] = x_ref[sl, :] + gate * o_ref[sl, :]
            return carry
        lax.fori_loop(0, tm // EW_CHUNK, body, 0)


def _merge_call(x, mods, o_a, o_b, o_c, z, w_proj_a, w_proj_b, w_proj_c, w_out, *, layer, t_ctx, n_lat,
                gate_col0):
    t, d = x.shape
    tm, tn = 1024, 256
    nj = d // tn
    g0 = gate_col0 // tn
    group = _group_index(tm, t_ctx, n_lat)
    gspec = lambda k: pl.BlockSpec((tm, tn), lambda i, j: (i, g0 + k * nj + j))
    ospec = lambda w: pl.BlockSpec((tm, w), lambda i, j: (i, 0))
    wspec = lambda w: pl.BlockSpec((None, w, tn), lambda i, j: (layer, 0, j))
    return pl.pallas_call(
        _merge_kernel,
        grid=(t // tm, nj),
        in_specs=[
            pl.BlockSpec((tm, d), lambda i, j: (i, 0), pipeline_mode=pl.Buffered(1)),
            pl.BlockSpec((None, None, N_MOD, d), lambda i, j: (layer, group(i), 0, 0)),
            ospec(o_a.shape[1]), ospec(o_b.shape[1]), ospec(o_c.shape[1]),
            gspec(0), gspec(1), gspec(2),
            wspec(o_a.shape[1]), wspec(o_b.shape[1]), wspec(o_c.shape[1]),
            pl.BlockSpec((None, tn, d), lambda i, j: (layer, j, 0)),
        ],
        out_specs=pl.BlockSpec((tm, d), lambda i, j: (i, 0)),
        out_shape=jax.ShapeDtypeStruct((t, d), F32),
        scratch_shapes=[pltpu.VMEM((o_a.shape[1], tn), BF16), pltpu.VMEM((o_b.shape[1], tn), BF16),
                        pltpu.VMEM((o_c.shape[1], tn), BF16), pltpu.VMEM((tn, d), BF16)],
        compiler_params=_params(("parallel", "arbitrary")),
    )(x, mods, o_a, o_b, o_c, z, z, z, w_proj_a, w_proj_b, w_proj_c, w_out)


def kernel(x_prompt, x_sample, cache_na_k, cache_na_v, state_hgrn, state_lru, c, c_ctx, mod_w, mod_b, norm_g, ffn1_w_up, ffn1_w_down, ffn2_w_up, ffn2_w_down, w_in, hgrn_lb_logits, hgrn_norm_g, na_rpb, lru_conv_w, lru_conv_b, lru_w_a, lru_b_a, lru_w_x, lru_b_x, lru_lambda, w_proj_a, w_proj_b, w_proj_c, w_out, final_norm_g):
    b_ctx, n_ctx, d = x_prompt.shape
    b_lat, n_lat, _ = x_sample.shape
    depth = mod_w.shape[0]
    t_ctx, t_lat = b_ctx * n_ctx, b_lat * n_lat
    hg_w = HG_HEADS * HG_DK
    na_w = NA_HEADS * NA_HEAD_DIM
    lru_w = lru_conv_b.shape[1]
    na_col0 = 5 * hg_w
    lru_col0 = na_col0 + 3 * na_w
    gate_col0 = lru_col0 + 2 * lru_w
    lat_block0 = t_ctx // n_lat

    x = jnp.concatenate([x_prompt.reshape(t_ctx, d), x_sample.reshape(t_lat, d)], axis=0)
    cond = jnp.concatenate([c_ctx[None], c, jnp.zeros((N_COND_ROWS - 1 - b_lat, d), F32)], axis=0)
    mods = _mods_call(cond, mod_w, mod_b).reshape(depth, N_COND_ROWS, N_MOD, d)

    rows = n_lat // GRID_W
    kh = min(NA_KH, rows)
    wa_bd = _block_diag_pairs(lru_w_a)
    wx_bd = _block_diag_pairs(lru_w_x)
    cache_k = cache_na_k.reshape(b_lat, depth, cache_na_k.shape[2], na_w)
    cache_v = cache_na_v.reshape(b_lat, depth, cache_na_v.shape[2], na_w)
    s0_t = jnp.swapaxes(state_hgrn, -1, -2)
    h0_t = jnp.transpose(state_lru, (1, 2, 0, 3))

    ks, vs, hgs, lrus = [], [], [], []
    for l in range(depth):
        last = l == depth - 1
        x = _ffn_call(x, mods, norm_g, ffn1_w_up, ffn1_w_down, layer=l, slot=0, t_ctx=t_ctx, n_lat=n_lat)
        z = _inproj_call(x, mods, norm_g, w_in, layer=l, t_ctx=t_ctx, n_lat=n_lat)

        oa_c, s_c = _hgrn_call(z, hgrn_lb_logits, hgrn_norm_g, None, layer=l, n_seq=b_ctx, n_tok=n_ctx,
                               row_block0=0)
        oa_l, _ = _hgrn_call(z, hgrn_lb_logits, hgrn_norm_g, s0_t[:, l], layer=l, n_seq=b_lat, n_tok=n_lat,
                             row_block0=lat_block0)
        ob_c = _ctx_attn_call(z, n_seq=b_ctx, n_tok=n_ctx, col0=na_col0)
        bias = _na_bias(na_rpb[l], rows, kh)
        ob_l = _na_call(z, cache_k, cache_v, bias, layer=l, n_seq=b_lat, n_tok=n_lat,
                        row_block0=lat_block0, col_block0=na_col0 // NA_HEAD_DIM)
        lru_args = (lru_conv_w, lru_conv_b, wa_bd, lru_b_a, wx_bd, lru_b_x, lru_lambda)
        oc_c, h_c = _lru_call(z, *lru_args, None, layer=l, n_seq=b_ctx, n_tok=n_ctx, row_block0=0,
                              col_block0=lru_col0 // LANE)
        oc_l, _ = _lru_call(z, *lru_args, h0_t[l], layer=l, n_seq=b_lat, n_tok=n_lat,
                            row_block0=t_ctx // t_lat, col_block0=lru_col0 // LANE)

        o_a = jnp.concatenate([oa_c, oa_l], axis=0)
        o_b = jnp.concatenate([ob_c, ob_l], axis=0)
        o_c = jnp.concatenate([oc_c, oc_l], axis=0)
        x = _merge_call(x, mods, o_a, o_b, o_c, z, w_proj_a, w_proj_b, w_proj_c, w_out, layer=l,
                        t_ctx=t_ctx, n_lat=n_lat, gate_col0=gate_col0)
        x = _ffn_call(x, mods, norm_g, ffn2_w_up, ffn2_w_down, layer=l, slot=2, t_ctx=t_ctx, n_lat=n_lat,
                      final_g=final_norm_g if last else None)

        ks.append(z[:t_ctx, na_col0 + na_w:na_col0 + 2 * na_w].reshape(b_ctx, n_ctx, NA_HEADS, NA_HEAD_DIM))
        vs.append(z[:t_ctx, na_col0 + 2 * na_w:na_col0 + 3 * na_w].reshape(b_ctx, n_ctx, NA_HEADS, NA_HEAD_DIM))
        hgs.append(jnp.swapaxes(s_c, -1, -2))
        lrus.append(jnp.transpose(h_c, (1, 0, 2)))

    y_prompt = x[:t_ctx].reshape(b_ctx, n_ctx, d)
    y_sample = x[t_ctx:].reshape(b_lat, n_lat, d)
    return (y_prompt, y_sample, jnp.stack(ks, axis=1), jnp.stack(vs, axis=1),
            jnp.stack(hgs, axis=1), jnp.stack(lrus, axis=1))
```

```python
import functools

import jax
import jax.numpy as jnp
from jax import lax
from jax.experimental import pallas as pl
from jax.experimental.pallas import tpu as pltpu

F32 = jnp.float32
BF16 = jnp.bfloat16

EPS = 1e-6
NEG = -1e30
N_MOD = 9
N_COND_ROWS = 8
HG_HEADS = 4
HG_DK = 128
HG_CHUNK = 16
HG_BLOCK = 128
NA_HEADS = 8
NA_HEAD_DIM = 128
NA_KH = 8
NA_KW = 16
NA_QBW = 8
NA_BAND = NA_QBW + NA_KW
GRID_W = 64
LRU_BLOCKS = 8
LRU_CONV = 4
LRU_C = 8.0
LANE = 128
SUBLANE = 8
VMEM_LIMIT = 56 * 1024 * 1024


def _params(sem, vmem=VMEM_LIMIT):
    return pltpu.CompilerParams(dimension_semantics=sem, vmem_limit_bytes=vmem)


def _silu(x):
    return x * jax.nn.sigmoid(x)


def _expm1(x):
    u = jnp.exp(x)
    um1 = u - 1.0
    y = um1 * x / jnp.where(u == 1.0, 1.0, jnp.log(u))
    return jnp.where(u == 1.0, x, jnp.where(um1 == -1.0, -1.0, y))


def _dot(a, b):
    return jnp.dot(a, b, preferred_element_type=F32)


def _dot_nt(a, b):
    return lax.dot_general(a, b, (((1,), (1,)), ((), ())), preferred_element_type=F32)


def _dot_tn(a, b):
    return lax.dot_general(a, b, (((0,), (0,)), ((), ())), preferred_element_type=F32)


def _mod_norm(x, g, shift, scale):
    y = x * lax.rsqrt(jnp.mean(x * x, axis=-1, keepdims=True) + EPS)
    return (y * g) * (1.0 + scale) + shift


def _group_index(tm, t_ctx, n_lat):
    n_ctx_tiles = t_ctx // tm
    per_seq = n_lat // tm
    return lambda i: jnp.where(i < n_ctx_tiles, 0, 1 + (i - n_ctx_tiles) // per_seq)


def _mods_kernel(c_ref, w_ref, b_ref, o_ref):
    s = _silu(c_ref[...]).astype(BF16)
    o_ref[...] = _dot(s, w_ref[...].astype(BF16)) + b_ref[...]


def _mods_call(cond, mod_w, mod_b):
    depth, d, n = mod_w.shape
    tn = 1024
    return pl.pallas_call(
        _mods_kernel,
        grid=(depth, n // tn),
        in_specs=[
            pl.BlockSpec((N_COND_ROWS, d), lambda l, j: (0, 0)),
            pl.BlockSpec((None, d, tn), lambda l, j: (l, 0, j)),
            pl.BlockSpec((None, 1, tn), lambda l, j: (l, 0, j)),
        ],
        out_specs=pl.BlockSpec((None, N_COND_ROWS, tn), lambda l, j: (l, 0, j)),
        out_shape=jax.ShapeDtypeStruct((depth, N_COND_ROWS, n), F32),
        compiler_params=_params(("parallel", "parallel")),
    )(cond, mod_w, mod_b.reshape(depth, 1, n))


ROW_CHUNK = 256
EW_CHUNK = 64


def _rows(r, n):
    return pl.ds(pl.multiple_of(r * n, n), n)


def _ffn_kernel(x_ref, mods_ref, g_ref, wa_ref, wu_ref, wd_ref, *rest, slot, final):
    if final:
        fg_ref, o_ref, h_ref, wa_s, wu_s, wd_s = rest
    else:
        o_ref, h_ref, wa_s, wu_s, wd_s = rest
    j = pl.program_id(1)
    tm = x_ref.shape[0]
    shift = mods_ref[3 * slot:3 * slot + 1, :]
    scale = mods_ref[3 * slot + 1:3 * slot + 2, :]
    gate = mods_ref[3 * slot + 2:3 * slot + 3, :]

    @pl.when(j == 0)
    def _():
        def body(r, carry):
            sl = _rows(r, EW_CHUNK)
            h_ref[sl, :] = _mod_norm(x_ref[sl, :], g_ref[...], shift, scale).astype(BF16)
            o_ref[sl, :] = jnp.zeros((EW_CHUNK, o_ref.shape[1]), F32)
            return carry
        lax.fori_loop(0, tm // EW_CHUNK, body, 0)

    wa_s[...] = wa_ref[...].astype(BF16)
    wu_s[...] = wu_ref[...].astype(BF16)
    wd_s[...] = wd_ref[...].astype(BF16)

    for r in range(tm // ROW_CHUNK):
        sl = slice(r * ROW_CHUNK, (r + 1) * ROW_CHUNK)
        h = h_ref[sl, :]
        a = _dot(h, wa_s[...])
        u = _dot(h, wu_s[...])
        act = (_silu(a) * u).astype(BF16)
        o_ref[sl, :] += _dot(act, wd_s[...])

    @pl.when(j == pl.num_programs(1) - 1)
    def _():
        def body(r, carry):
            sl = _rows(r, EW_CHUNK)
            y = x_ref[sl, :] + (0.5 * gate) * o_ref[sl, :]
            if final:
                y = y * lax.rsqrt(jnp.mean(y * y, axis=-1, keepdims=True) + EPS) * fg_ref[...]
            o_ref[sl, :] = y
            return carry
        lax.fori_loop(0, tm // EW_CHUNK, body, 0)


def _ffn_call(x, mods, norm_g, w_up, w_down, *, layer, slot, t_ctx, n_lat, final_g=None):
    t, d = x.shape
    f = w_down.shape[1]
    tm, tf = 1024, 256
    nf = f // tf
    group = _group_index(tm, t_ctx, n_lat)
    norm_slot = slot
    in_specs = [
        pl.BlockSpec((tm, d), lambda i, j: (i, 0)),
        pl.BlockSpec((None, None, N_MOD, d), lambda i, j: (layer, group(i), 0, 0)),
        pl.BlockSpec((None, None, 1, d), lambda i, j: (layer, norm_slot, 0, 0)),
        pl.BlockSpec((None, d, tf), lambda i, j: (layer, 0, j)),
        pl.BlockSpec((None, d, tf), lambda i, j: (layer, 0, nf + j)),
        pl.BlockSpec((None, tf, d), lambda i, j: (layer, j, 0)),
    ]
    args = [x, mods, norm_g.reshape(norm_g.shape[0], norm_g.shape[1], 1, d), w_up, w_up, w_down]
    if final_g is not None:
        in_specs.append(pl.BlockSpec((1, d), lambda i, j: (0, 0)))
        args.append(final_g.reshape(1, d))
    return pl.pallas_call(
        functools.partial(_ffn_kernel, slot=slot, final=final_g is not None),
        grid=(t // tm, nf),
        in_specs=in_specs,
        out_specs=pl.BlockSpec((tm, d), lambda i, j: (i, 0)),
        out_shape=jax.ShapeDtypeStruct((t, d), F32),
        scratch_shapes=[
            pltpu.VMEM((tm, d), BF16),
            pltpu.VMEM((d, tf), BF16),
            pltpu.VMEM((d, tf), BF16),
            pltpu.VMEM((tf, d), BF16),
        ],
        compiler_params=_params(("parallel", "arbitrary")),
    )(*args)


def _inproj_kernel(x_ref, mods_ref, g_ref, w_ref, z_ref, h_ref, w_s):
    j = pl.program_id(1)
    tm = x_ref.shape[0]
    shift = mods_ref[3:4, :]
    scale = mods_ref[4:5, :]

    @pl.when(j == 0)
    def _():
        def body(r, carry):
            sl = _rows(r, EW_CHUNK)
            h_ref[sl, :] = _mod_norm(x_ref[sl, :], g_ref[...], shift, scale).astype(BF16)
            return carry
        lax.fori_loop(0, tm // EW_CHUNK, body, 0)

    w_s[...] = w_ref[...].astype(BF16)

    for r in range(tm // ROW_CHUNK):
        sl = slice(r * ROW_CHUNK, (r + 1) * ROW_CHUNK)
        z_ref[sl, :] = _dot(h_ref[sl, :], w_s[...])


def _inproj_call(x, mods, norm_g, w_in, *, layer, t_ctx, n_lat):
    t, d = x.shape
    n = w_in.shape[2]
    tm, tn = 1024, 512
    group = _group_index(tm, t_ctx, n_lat)
    return pl.pallas_call(
        _inproj_kernel,
        grid=(t // tm, n // tn),
        in_specs=[
            pl.BlockSpec((tm, d), lambda i, j: (i, 0)),
            pl.BlockSpec((None, None, N_MOD, d), lambda i, j: (layer, group(i), 0, 0)),
            pl.BlockSpec((None, None, 1, d), lambda i, j: (layer, 1, 0, 0)),
            pl.BlockSpec((None, d, tn), lambda i, j: (layer, 0, j)),
        ],
        out_specs=pl.BlockSpec((tm, tn), lambda i, j: (i, j)),
        out_shape=jax.ShapeDtypeStruct((t, n), F32),
        scratch_shapes=[pltpu.VMEM((tm, d), BF16), pltpu.VMEM((d, tn), BF16)],
        compiler_params=_params(("parallel", "arbitrary")),
    )(x, mods, norm_g.reshape(norm_g.shape[0], norm_g.shape[1], 1, d), w_in)


def _log_forget(zf, log_lb, log1m_lb):
    ls = jnp.minimum(zf, 0.0) - jnp.log1p(jnp.exp(-jnp.abs(zf)))
    b = log1m_lb + ls
    hi = jnp.maximum(log_lb, b)
    return hi + jnp.log1p(jnp.exp(-jnp.abs(log_lb - b)))


def _hgrn_kernel(zq_ref, zff_ref, zfb_ref, zi_ref, zo_ref, lg_ref, ng_ref, *rest, layer, has_s0):
    if has_s0:
        s0_ref, o_ref, sfin_ref, q_s, k_s, b_s, qb_s, kd_s, dec_s, of_s, ob_s, st_s = rest
    else:
        o_ref, sfin_ref, q_s, k_s, b_s, qb_s, kd_s, dec_s, of_s, ob_s, st_s = rest
    n = zq_ref.shape[0]
    c = HG_CHUNK
    n_chunks = n // c
    blk = HG_BLOCK
    cpb = blk // c
    sh = c.bit_length() - 1

    logits = lg_ref[...]
    depth = logits.shape[0]
    mx = logits[0]
    for i in range(1, depth):
        mx = jnp.maximum(mx, logits[i])
    ex = [jnp.exp(logits[i] - mx) for i in range(depth)]
    tot = ex[0]
    for i in range(1, depth):
        tot = tot + ex[i]
    lb = jnp.zeros_like(mx)
    for i in range(1, layer + 1):
        lb = lb + ex[i] / tot
    log_lb = jnp.log(lb)
    log1m_lb = jnp.log1p(-lb)

    if has_s0:
        st_s[...] = s0_ref[...]
    else:
        st_s[...] = jnp.zeros(st_s.shape, F32)

    br = lax.broadcasted_iota(jnp.int32, (blk, blk), 0)
    bc = lax.broadcasted_iota(jnp.int32, (blk, blk), 1)
    same = (br >> sh) == (bc >> sh)
    cum_mat = [(same & (bc <= br)).astype(BF16), (same & (bc >= br)).astype(BF16)]
    tot_mat = same.astype(BF16)

    def gates(i, carry):
        rows = _rows(i, blk)
        q = _silu(zq_ref[rows, :])
        q_s[rows, :] = q
        for d in range(2):
            zf = (zff_ref if d == 0 else zfb_ref)[rows, :]
            lf = _log_forget(zf, log_lb[d:d + 1, :], log1m_lb[d:d + 1, :])
            k = -_expm1(lf)
            hi = lf.astype(BF16)
            lo = (lf - hi.astype(F32)).astype(BF16)
            b = _dot(cum_mat[d], hi) + _dot(cum_mat[d], lo)
            b_tot = _dot(tot_mat, hi) + _dot(tot_mat, lo)
            k_s[d, rows, :] = k
            b_s[d, rows, :] = b
            qb_s[d, rows, :] = (q * jnp.exp(b)).astype(BF16)
            kd_s[d, rows, :] = (k * jnp.exp(b_tot - b)).astype(BF16)
            dec_s[d, rows, :] = jnp.exp(b_tot)
        return carry
    lax.fori_loop(0, n // blk, gates, 0)

    nh = HG_HEADS
    hs = [slice(h * HG_DK, (h + 1) * HG_DK) for h in range(nh)]
    s8 = lax.broadcasted_iota(jnp.int32, (SUBLANE, nh * HG_DK), 0)
    ones = jnp.ones((HG_DK, LANE), BF16)
    sel_r = lax.broadcasted_iota(jnp.int32, (nh * c, nh * c * c), 0)
    sel_c = lax.broadcasted_iota(jnp.int32, (nh * c, nh * c * c), 1)
    sel = ((sel_c >= sel_r * c) & (sel_c < sel_r * c + c)).astype(BF16)
    zero8 = jnp.zeros((SUBLANE, nh * HG_DK), F32)

    def scores_of(d, rows):
        q = q_s[rows, :]
        k = k_s[d, rows, :]
        b = b_s[d, rows, :]
        halves = [(b[:SUBLANE], k[:SUBLANE]), (b[SUBLANE:], k[SUBLANE:])]
        parts = []
        for t in range(c):
            bt, qt = b[t:t + 1, :], q[t:t + 1, :]
            own = t // SUBLANE
            row = []
            for half, (bh, kh) in enumerate(halves):
                if half == own:
                    tt = t - own * SUBLANE
                    vis = (s8 <= tt) if d == 0 else (s8 >= tt)
                    row.append((qt * jnp.exp(jnp.where(vis, bt - bh, NEG))) * kh)
                elif (half < own) == (d == 0):
                    row.append((qt * jnp.exp(bt - bh)) * kh)
                else:
                    row.append(zero8)
            parts.append(row)
        p = jnp.concatenate([parts[t][half][:, hs[h]] for h in range(nh) for t in range(c) for half in range(2)],
                            axis=0).astype(BF16)
        return _dot(p, ones)

    def state_step(d, h, rows, ci):
        st = st_s[d, h]
        o_inter = _dot_nt(qb_s[d, rows, hs[h]], st.astype(BF16))
        dec8 = dec_s[d, pl.ds(pl.multiple_of(ci * c, c), SUBLANE), hs[h]]
        st_dec = (st.reshape(HG_DK // SUBLANE, SUBLANE, HG_DK) * dec8[None]).reshape(HG_DK, HG_DK)
        st_s[d, h] = st_dec + _dot_tn(zi_ref[rows, hs[h]].astype(BF16), kd_s[d, rows, hs[h]])
        return o_inter

    def body(i, carry):
        cis = (i, n_chunks - 1 - i)
        rows = [_rows(ci, c) for ci in cis]
        scores = [scores_of(d, rows[d]) for d in range(2)]
        o_inter = [[state_step(d, h, rows[d], cis[d]) for h in range(nh)] for d in range(2)]
        for d in range(2):
            v = zi_ref[rows[d], :]
            v_rep = jnp.concatenate([v[:, hs[h]] for h in range(nh) for _ in range(c)], axis=0)
            o_intra = _dot(sel, (scores[d] * v_rep).astype(BF16))
            for h in range(nh):
                (of_s if d == 0 else ob_s)[rows[d], hs[h]] = o_intra[h * c:(h + 1) * c, :] + o_inter[d][h]
        return carry
    lax.fori_loop(0, n_chunks, body, 0)

    sfin_ref[...] = st_s[...]

    def fin(r, carry):
        rows = _rows(r, EW_CHUNK)
        for h in range(HG_HEADS):
            cols = slice(h * HG_DK, (h + 1) * HG_DK)
            o = of_s[rows, cols] + ob_s[rows, cols]
            o = o * lax.rsqrt(jnp.mean(o * o, axis=-1, keepdims=True) + EPS) * ng_ref[h:h + 1, :]
            o_ref[rows, cols] = (o * _silu(zo_ref[rows, cols])).astype(o_ref.dtype)
        return carry
    lax.fori_loop(0, n // EW_CHUNK, fin, 0)


def _hgrn_call(z, lb_logits, hgrn_norm_g, s0_t, *, layer, n_seq, n_tok, row_block0):
    w = HG_HEADS * HG_DK
    depth = lb_logits.shape[0]
    zspec = lambda k: pl.BlockSpec((n_tok, w), lambda s: (row_block0 + s, k))
    in_specs = [zspec(0), zspec(1), zspec(2), zspec(3), zspec(4),
                pl.BlockSpec((depth, 2, w), lambda s: (0, 0, 0)),
                pl.BlockSpec((None, HG_HEADS, HG_DK), lambda s: (layer, 0, 0))]
    args = [z, z, z, z, z, lb_logits, hgrn_norm_g]
    st_spec = pl.BlockSpec((None, 2, HG_HEADS, HG_DK, HG_DK), lambda s: (s, 0, 0, 0, 0))
    if s0_t is not None:
        in_specs.append(st_spec)
        args.append(s0_t)
    return pl.pallas_call(
        functools.partial(_hgrn_kernel, layer=layer, has_s0=s0_t is not None),
        grid=(n_seq,),
        in_specs=in_specs,
        out_specs=[pl.BlockSpec((n_tok, w), lambda s: (s, 0)), st_spec],
        out_shape=[jax.ShapeDtypeStruct((n_seq * n_tok, w), BF16),
                   jax.ShapeDtypeStruct((n_seq, 2, HG_HEADS, HG_DK, HG_DK), F32)],
        scratch_shapes=[pltpu.VMEM((n_tok, w), F32),
                        pltpu.VMEM((2, n_tok, w), F32),
                        pltpu.VMEM((2, n_tok, w), F32),
                        pltpu.VMEM((2, n_tok, w), BF16),
                        pltpu.VMEM((2, n_tok, w), BF16),
                        pltpu.VMEM((2, n_tok, w), F32),
                        pltpu.VMEM((n_tok, w), F32), pltpu.VMEM((n_tok, w), F32),
                        pltpu.VMEM((2, HG_HEADS, HG_DK, HG_DK), F32)],
        compiler_params=_params(("parallel",)),
    )(*args)


def _ctx_attn_kernel(q0_ref, q1_ref, k0_ref, k1_ref, v0_ref, v1_ref, o_ref, *, scale):
    q_refs, k_refs, v_refs = (q0_ref, q1_ref), (k0_ref, k1_ref), (v0_ref, v1_ref)
    heads_per_block = q0_ref.shape[1] // NA_HEAD_DIM
    for h in range(NA_HEADS):
        blk, off = divmod(h, heads_per_block)
        cols = slice(off * NA_HEAD_DIM, (off + 1) * NA_HEAD_DIM)
        s = _dot_nt(q_refs[blk][:, cols].astype(BF16), k_refs[blk][:, cols].astype(BF16)) * scale
        p = jnp.exp(s - jnp.max(s, axis=-1, keepdims=True))
        p = p / jnp.sum(p, axis=-1, keepdims=True)
        o = _dot(p.astype(BF16), v_refs[blk][:, cols].astype(BF16))
        o_ref[:, h * NA_HEAD_DIM:(h + 1) * NA_HEAD_DIM] = o.astype(o_ref.dtype)


def _ctx_attn_call(z, *, n_seq, n_tok, col0):
    w = NA_HEADS * NA_HEAD_DIM
    half = w // 2
    zspec = lambda k: pl.BlockSpec((n_tok, half), lambda s: (s, col0 // half + k))
    return pl.pallas_call(
        functools.partial(_ctx_attn_kernel, scale=NA_HEAD_DIM ** -0.5),
        grid=(n_seq,),
        in_specs=[zspec(k) for k in range(6)],
        out_specs=pl.BlockSpec((n_tok, w), lambda s: (s, 0)),
        out_shape=jax.ShapeDtypeStruct((n_seq * n_tok, w), BF16),
        compiler_params=_params(("parallel",)),
    )(z, z, z, z, z, z)


def _na_kernel(q_ref, k_ref, v_ref, kc_ref, vc_ref, bias_ref, o_ref, *, scale, rows, kh):
    kc = kc_ref[...].astype(BF16)
    vc = vc_ref[...].astype(BF16)
    for r in range(rows):
        start = min(max(r - kh // 2, 0), rows - kh) * GRID_W
        q = q_ref[r * GRID_W:(r + 1) * GRID_W, :].astype(BF16)
        k_loc = k_ref[start:start + kh * GRID_W, :].astype(BF16)
        v_loc = v_ref[start:start + kh * GRID_W, :].astype(BF16)
        s_loc = _dot_nt(q, k_loc) * scale + bias_ref[r]
        s_ctx = _dot_nt(q, kc) * scale
        m = jnp.maximum(jnp.max(s_loc, axis=-1, keepdims=True), jnp.max(s_ctx, axis=-1, keepdims=True))
        p_loc = jnp.exp(s_loc - m)
        p_ctx = jnp.exp(s_ctx - m)
        den = jnp.sum(p_loc, axis=-1, keepdims=True) + jnp.sum(p_ctx, axis=-1, keepdims=True)
        o = _dot((p_loc / den).astype(BF16), v_loc) + _dot((p_ctx / den).astype(BF16), vc)
        o_ref[r * GRID_W:(r + 1) * GRID_W, :] = o.astype(o_ref.dtype)


def _na_bias(rpb, rows, kh):
    heads = rpb.shape[0]
    qcol = jnp.arange(GRID_W)
    kcol = jnp.arange(GRID_W)
    win_start = jnp.clip(qcol - NA_KW // 2, 0, GRID_W - NA_KW)
    in_win = (kcol[None, :] >= win_start[:, None]) & (kcol[None, :] < win_start[:, None] + NA_KW)
    lo = GRID_W - NA_KW
    pad = jnp.pad(rpb.astype(F32), ((0, 0), (0, 0), (lo, lo)))
    toep = jnp.stack([pad[:, :, GRID_W - 1 - q:2 * GRID_W - 1 - q] for q in range(GRID_W)], axis=2)
    toep = jnp.where(in_win[None, None], toep, NEG)
    per_row = []
    for r in range(rows):
        dy0 = min(max(r - kh // 2, 0), rows - kh) - r + NA_KH - 1
        per_row.append(toep[:, dy0:dy0 + kh])
    bias = jnp.stack(per_row, axis=1)
    return jnp.transpose(bias, (0, 1, 3, 2, 4)).reshape(heads, rows, GRID_W, kh * GRID_W)


def _na_call(z, cache_k, cache_v, bias, *, layer, n_seq, n_tok, row_block0, col_block0):
    rows = n_tok // GRID_W
    kh = min(NA_KH, rows)
    dh = NA_HEAD_DIM
    past = cache_k.shape[2]
    zspec = lambda k: pl.BlockSpec((n_tok, dh), lambda b, h: (row_block0 + b, col_block0 + k * NA_HEADS + h))
    cspec = pl.BlockSpec((None, None, past, dh), lambda b, h: (b, layer, 0, h))
    return pl.pallas_call(
        functools.partial(_na_kernel, scale=dh ** -0.5, rows=rows, kh=kh),
        grid=(n_seq, NA_HEADS),
        in_specs=[zspec(0), zspec(1), zspec(2), cspec, cspec,
                  pl.BlockSpec((None, rows, GRID_W, kh * GRID_W), lambda b, h: (h, 0, 0, 0))],
        out_specs=pl.BlockSpec((n_tok, dh), lambda b, h: (b, h)),
        out_shape=jax.ShapeDtypeStruct((n_seq * n_tok, NA_HEADS * dh), BF16),
        compiler_params=_params(("parallel", "parallel")),
    )(z, z, z, cache_k, cache_v, bias)


def _lru_kernel(zx_ref, zg_ref, cw_ref, cb_ref, wa_ref, ba_ref, wx_ref, bx_ref, lam_ref, *rest,
                n_seq, n_tok, has_h0):
    if has_h0:
        h0_ref, y_ref, hfin_ref, a_s, u_s = rest
    else:
        y_ref, hfin_ref, a_s, u_s = rest
    tpos = lax.broadcasted_iota(jnp.int32, (n_tok, LANE), 0)
    neg_lam = -lam_ref[...]
    softplus = jnp.maximum(neg_lam, 0.0) + jnp.log1p(jnp.exp(-jnp.abs(neg_lam)))
    left = LRU_CONV // 2

    def gates(g, carry):
        rows = _rows(g, n_tok)
        zx = zx_ref[rows, :]
        x = cb_ref[...] + cw_ref[left:left + 1, :] * zx
        for j in range(LRU_CONV):
            off = j - left
            if off == 0:
                continue
            shifted = pltpu.roll(zx, (-off) % n_tok, 0)
            valid = (tpos + off >= 0) & (tpos + off < n_tok)
            x = x + cw_ref[j:j + 1, :] * jnp.where(valid, shifted, 0.0)
        xb = x.astype(BF16)
        for d in range(2):
            r_gate = jax.nn.sigmoid(_dot(xb, wa_ref[d].astype(BF16)) + ba_ref[d:d + 1, :])
            i_gate = jax.nn.sigmoid(_dot(xb, wx_ref[d].astype(BF16)) + bx_ref[d:d + 1, :])
            log_a = (-LRU_C * r_gate) * softplus[d:d + 1, :]
            a_s[d, rows, :] = jnp.exp(log_a)
            u_s[d, rows, :] = jnp.sqrt(-_expm1(2.0 * log_a)) * (i_gate * x)
        return carry
    lax.fori_loop(0, n_seq, gates, 0)

    if has_h0:
        h_init = (h0_ref[0], h0_ref[1])
    else:
        h_init = (jnp.zeros((n_seq, LANE), F32), jnp.zeros((n_seq, LANE), F32))

    def step(t, carry):
        hf, hb = carry
        fw = pl.ds(t, n_seq, stride=n_tok)
        bw = pl.ds(n_tok - 1 - t, n_seq, stride=n_tok)
        hf = a_s[0, fw, :] * hf + u_s[0, fw, :]
        hb = a_s[1, bw, :] * hb + u_s[1, bw, :]
        u_s[0, fw, :] = hf
        u_s[1, bw, :] = hb
        return hf, hb
    hf, hb = lax.fori_loop(0, n_tok, step, h_init)
    hfin_ref[0] = hf
    hfin_ref[1] = hb

    def fin(g, carry):
        rows = _rows(g, n_tok)
        y = (u_s[0, rows, :] + u_s[1, rows, :]) * jax.nn.gelu(zg_ref[rows, :], approximate=True)
        y_ref[rows, :] = y.astype(y_ref.dtype)
        return carry
    lax.fori_loop(0, n_seq, fin, 0)


def _block_diag_pairs(w):
    depth, nd, nb, bw, _ = w.shape
    w = w.reshape(depth, nd, nb // 2, 2, bw, bw)
    eye = jnp.eye(2, dtype=w.dtype)
    out = w[:, :, :, :, :, None, :] * eye[None, None, None, :, None, :, None]
    return out.reshape(depth, nd, nb // 2, 2 * bw, 2 * bw)


def _lru_call(z, conv_w, conv_b, wa_bd, b_a, wx_bd, b_x, lam, h0_t, *, layer, n_seq, n_tok, row_block0,
              col_block0):
    w = conv_b.shape[1]
    n_cb = w // LANE
    rows = n_seq * n_tok
    zspec = lambda k: pl.BlockSpec((rows, LANE), lambda cb: (row_block0, col_block0 + k * n_cb + cb))
    vec2 = pl.BlockSpec((None, 2, LANE), lambda cb: (layer, 0, cb))
    wspec = pl.BlockSpec((None, 2, None, LANE, LANE), lambda cb: (layer, 0, cb, 0, 0))
    hspec = pl.BlockSpec((2, n_seq, LANE), lambda cb: (0, 0, cb))
    in_specs = [zspec(0), zspec(1),
                pl.BlockSpec((None, LRU_CONV, LANE), lambda cb: (layer, 0, cb)),
                pl.BlockSpec((None, 1, LANE), lambda cb: (layer, 0, cb)),
                wspec, vec2, wspec, vec2, vec2]
    args = [z, z, conv_w, conv_b.reshape(conv_b.shape[0], 1, w), wa_bd, b_a, wx_bd, b_x, lam]
    if h0_t is not None:
        in_specs.append(hspec)
        args.append(h0_t)
    return pl.pallas_call(
        functools.partial(_lru_kernel, n_seq=n_seq, n_tok=n_tok, has_h0=h0_t is not None),
        grid=(n_cb,),
        in_specs=in_specs,
        out_specs=[pl.BlockSpec((rows, LANE), lambda cb: (0, cb)), hspec],
        out_shape=[jax.ShapeDtypeStruct((rows, w), BF16), jax.ShapeDtypeStruct((2, n_seq, w), F32)],
        scratch_shapes=[pltpu.VMEM((2, rows, LANE), F32), pltpu.VMEM((2, rows, LANE), F32)],
        compiler_params=_params(("parallel",)),
    )(*args)


def _merge_kernel(x_ref, mods_ref, oa_ref, ob_ref, oc_ref, ga_ref, gb_ref, gc_ref,
                  wa_ref, wb_ref, wc_ref, wo_ref, o_ref, wa_s, wb_s, wc_s, wo_s):
    j = pl.program_id(1)
    tm = x_ref.shape[0]
    gate = mods_ref[5:6, :]

    @pl.when(j == 0)
    def _():
        def body(r, carry):
            o_ref[_rows(r, EW_CHUNK), :] = jnp.zeros((EW_CHUNK, o_ref.shape[1]), F32)
            return carry
        lax.fori_loop(0, tm // EW_CHUNK, body, 0)

    wa_s[...] = wa_ref[...].astype(BF16)
    wb_s[...] = wb_ref[...].astype(BF16)
    wc_s[...] = wc_ref[...].astype(BF16)
    wo_s[...] = wo_ref[...].astype(BF16)

    for r in range(tm // ROW_CHUNK):
        sl = slice(r * ROW_CHUNK, (r + 1) * ROW_CHUNK)

        def branch(o_r, w_s, g_r):
            return jax.nn.sigmoid(g_r[sl, :]) * _dot(o_r[sl, :], w_s[...])

        m = branch(oa_ref, wa_s, ga_ref) + branch(ob_ref, wb_s, gb_ref) + branch(oc_ref, wc_s, gc_ref)
        o_ref[sl, :] += _dot(m.astype(BF16), wo_s[...])

    @pl.when(j == pl.num_programs(1) - 1)
    def _():
        def body(r, carry):
            sl = _rows(r, EW_CHUNK)
            o_ref[sl, :] = x_ref[sl, :] + gate * o_ref[sl, :]
            return carry
        lax.fori_loop(0, tm // EW_CHUNK, body, 0)


def _merge_call(x, mods, o_a, o_b, o_c, z, w_proj_a, w_proj_b, w_proj_c, w_out, *, layer, t_ctx, n_lat,
                gate_col0):
    t, d = x.shape
    tm, tn = 1024, 256
    nj = d // tn
    g0 = gate_col0 // tn
    group = _group_index(tm, t_ctx, n_lat)
    gspec = lambda k: pl.BlockSpec((tm, tn), lambda i, j: (i, g0 + k * nj + j))
    ospec = lambda w: pl.BlockSpec((tm, w), lambda i, j: (i, 0))
    wspec = lambda w: pl.BlockSpec((None, w, tn), lambda i, j: (layer, 0, j))
    return pl.pallas_call(
        _merge_kernel,
        grid=(t // tm, nj),
        in_specs=[
            pl.BlockSpec((tm, d), lambda i, j: (i, 0), pipeline_mode=pl.Buffered(1)),
            pl.BlockSpec((None, None, N_MOD, d), lambda i, j: (layer, group(i), 0, 0)),
            ospec(o_a.shape[1]), ospec(o_b.shape[1]), ospec(o_c.shape[1]),
            gspec(0), gspec(1), gspec(2),
            wspec(o_a.shape[1]), wspec(o_b.shape[1]), wspec(o_c.shape[1]),
            pl.BlockSpec((None, tn, d), lambda i, j: (layer, j, 0)),
        ],
        out_specs=pl.BlockSpec((tm, d), lambda i, j: (i, 0)),
        out_shape=jax.ShapeDtypeStruct((t, d), F32),
        scratch_shapes=[pltpu.VMEM((o_a.shape[1], tn), BF16), pltpu.VMEM((o_b.shape[1], tn), BF16),
                        pltpu.VMEM((o_c.shape[1], tn), BF16), pltpu.VMEM((tn, d), BF16)],
        compiler_params=_params(("parallel", "arbitrary")),
    )(x, mods, o_a, o_b, o_c, z, z, z, w_proj_a, w_proj_b, w_proj_c, w_out)


def kernel(x_prompt, x_sample, cache_na_k, cache_na_v, state_hgrn, state_lru, c, c_ctx, mod_w, mod_b, norm_g, ffn1_w_up, ffn1_w_down, ffn2_w_up, ffn2_w_down, w_in, hgrn_lb_logits, hgrn_norm_g, na_rpb, lru_conv_w, lru_conv_b, lru_w_a, lru_b_a, lru_w_x, lru_b_x, lru_lambda, w_proj_a, w_proj_b, w_proj_c, w_out, final_norm_g):
    b_ctx, n_ctx, d = x_prompt.shape
    b_lat, n_lat, _ = x_sample.shape
    depth = mod_w.shape[0]
    t_ctx, t_lat = b_ctx * n_ctx, b_lat * n_lat
    hg_w = HG_HEADS * HG_DK
    na_w = NA_HEADS * NA_HEAD_DIM
    lru_w = lru_conv_b.shape[1]
    na_col0 = 5 * hg_w
    lru_col0 = na_col0 + 3 * na_w
    gate_col0 = lru_col0 + 2 * lru_w
    lat_block0 = t_ctx // n_lat

    x = jnp.concatenate([x_prompt.reshape(t_ctx, d), x_sample.reshape(t_lat, d)], axis=0)
    cond = jnp.concatenate([c_ctx[None], c, jnp.zeros((N_COND_ROWS - 1 - b_lat, d), F32)], axis=0)
    mods = _mods_call(cond, mod_w, mod_b).reshape(depth, N_COND_ROWS, N_MOD, d)

    rows = n_lat // GRID_W
    kh = min(NA_KH, rows)
    wa_bd = _block_diag_pairs(lru_w_a)
    wx_bd = _block_diag_pairs(lru_w_x)
    cache_k = cache_na_k.reshape(b_lat, depth, cache_na_k.shape[2], na_w)
    cache_v = cache_na_v.reshape(b_lat, depth, cache_na_v.shape[2], na_w)
    s0_t = jnp.swapaxes(state_hgrn, -1, -2)
    h0_t = jnp.transpose(state_lru, (1, 2, 0, 3))

    ks, vs, hgs, lrus = [], [], [], []
    for l in range(depth):
        last = l == depth - 1
        x = _ffn_call(x, mods, norm_g, ffn1_w_up, ffn1_w_down, layer=l, slot=0, t_ctx=t_ctx, n_lat=n_lat)
        z = _inproj_call(x, mods, norm_g, w_in, layer=l, t_ctx=t_ctx, n_lat=n_lat)

        oa_c, s_c = _hgrn_call(z, hgrn_lb_logits, hgrn_norm_g, None, layer=l, n_seq=b_ctx, n_tok=n_ctx,
                               row_block0=0)
        oa_l, _ = _hgrn_call(z, hgrn_lb_logits, hgrn_norm_g, s0_t[:, l], layer=l, n_seq=b_lat, n_tok=n_lat,
                             row_block0=lat_block0)
        ob_c = _ctx_attn_call(z, n_seq=b_ctx, n_tok=n_ctx, col0=na_col0)
        bias = _na_bias(na_rpb[l], rows, kh)
        ob_l = _na_call(z, cache_k, cache_v, bias, layer=l, n_seq=b_lat, n_tok=n_lat,
                        row_block0=lat_block0, col_block0=na_col0 // NA_HEAD_DIM)
        lru_args = (lru_conv_w, lru_conv_b, wa_bd, lru_b_a, wx_bd, lru_b_x, lru_lambda)
        oc_c, h_c = _lru_call(z, *lru_args, None, layer=l, n_seq=b_ctx, n_tok=n_ctx, row_block0=0,
                              col_block0=lru_col0 // LANE)
        oc_l, _ = _lru_call(z, *lru_args, h0_t[l], layer=l, n_seq=b_lat, n_tok=n_lat,
                            row_block0=t_ctx // t_lat, col_block0=lru_col0 // LANE)

        o_a = jnp.concatenate([oa_c, oa_l], axis=0)
        o_b = jnp.concatenate([ob_c, ob_l], axis=0)
        o_c = jnp.concatenate([oc_c, oc_l], axis=0)
        x = _merge_call(x, mods, o_a, o_b, o_c, z, w_proj_a, w_proj_b, w_proj_c, w_out, layer=l,
                        t_ctx=t_ctx, n_lat=n_lat, gate_col0=gate_col0)
        x = _ffn_call(x, mods, norm_g, ffn2_w_up, ffn2_w_down, layer=l, slot=2, t_ctx=t_ctx, n_lat=n_lat,
                      final_g=final_norm_g if last else None)

        ks.append(z[:t_ctx, na_col0 + na_w:na_col0 + 2 * na_w].reshape(b_ctx, n_ctx, NA_HEADS, NA_HEAD_DIM))
        vs.append(z[:t_ctx, na_col0 + 2 * na_w:na_col0 + 3 * na_w].reshape(b_ctx, n_ctx, NA_HEADS, NA_HEAD_DIM))
        hgs.append(jnp.swapaxes(s_c, -1, -2))
        lrus.append(jnp.transpose(h_c, (1, 0, 2)))

    y_prompt = x[:t_ctx].reshape(b_ctx, n_ctx, d)
    y_sample = x[t_ctx:].reshape(b_lat, n_lat, d)
    return (y_prompt, y_sample, jnp.stack(ks, axis=1), jnp.stack(vs, axis=1),
            jnp.stack(hgs, axis=1), jnp.stack(lrus, axis=1))
```

```python
import functools

import jax
import jax.numpy as jnp
from jax import lax
from jax.experimental import pallas as pl
from jax.experimental.pallas import tpu as pltpu

F32 = jnp.float32
BF16 = jnp.bfloat16

EPS = 1e-6
NEG = -1e30
N_MOD = 9
N_COND_ROWS = 8
HG_HEADS = 4
HG_DK = 128
HG_CHUNK = 16
HG_BLOCK = 128
NA_HEADS = 8
NA_HEAD_DIM = 128
NA_KH = 8
NA_KW = 16
NA_QBW = 8
NA_BAND = NA_QBW + NA_KW
GRID_W = 64
LRU_BLOCKS = 8
LRU_CONV = 4
LRU_C = 8.0
LANE = 128
SUBLANE = 8
VMEM_LIMIT = 56 * 1024 * 1024


def _params(sem, vmem=VMEM_LIMIT):
    return pltpu.CompilerParams(dimension_semantics=sem, vmem_limit_bytes=vmem)


def _silu(x):
    return x * jax.nn.sigmoid(x)


def _expm1(x):
    u = jnp.exp(x)
    um1 = u - 1.0
    y = um1 * x / jnp.where(u == 1.0, 1.0, jnp.log(u))
    return jnp.where(u == 1.0, x, jnp.where(um1 == -1.0, -1.0, y))


def _dot(a, b):
    return jnp.dot(a, b, preferred_element_type=F32)


def _dot_nt(a, b):
    return lax.dot_general(a, b, (((1,), (1,)), ((), ())), preferred_element_type=F32)


def _dot_tn(a, b):
    return lax.dot_general(a, b, (((0,), (0,)), ((), ())), preferred_element_type=F32)


def _mod_norm(x, g, shift, scale):
    y = x * lax.rsqrt(jnp.mean(x * x, axis=-1, keepdims=True) + EPS)
    return (y * g) * (1.0 + scale) + shift


def _group_index(tm, group0, rows_per_group):
    tiles_per_group = rows_per_group // tm
    return lambda i: group0 + i // tiles_per_group


def _mods_kernel(c_ref, w_ref, b_ref, o_ref):
    s = _silu(c_ref[...]).astype(BF16)
    o_ref[...] = _dot(s, w_ref[...].astype(BF16)) + b_ref[...]


def _mods_call(cond, mod_w, mod_b):
    depth, d, n = mod_w.shape
    tn = 1024
    return pl.pallas_call(
        _mods_kernel,
        grid=(depth, n // tn),
        in_specs=[
            pl.BlockSpec((N_COND_ROWS, d), lambda l, j: (0, 0)),
            pl.BlockSpec((None, d, tn), lambda l, j: (l, 0, j)),
            pl.BlockSpec((None, 1, tn), lambda l, j: (l, 0, j)),
        ],
        out_specs=pl.BlockSpec((None, N_COND_ROWS, tn), lambda l, j: (l, 0, j)),
        out_shape=jax.ShapeDtypeStruct((depth, N_COND_ROWS, n), F32),
        compiler_params=_params(("parallel", "parallel")),
    )(cond, mod_w, mod_b.reshape(depth, 1, n))


ROW_CHUNK = 256
EW_CHUNK = 64


def _rows(r, n):
    return pl.ds(pl.multiple_of(r * n, n), n)


def _ffn_kernel(x_ref, mods_ref, g_ref, wa_ref, wu_ref, wd_ref, *rest, slot, final):
    if final:
        fg_ref, o_ref, h_ref, wa_s, wu_s, wd_s = rest
    else:
        o_ref, h_ref, wa_s, wu_s, wd_s = rest
    j = pl.program_id(1)
    tm = x_ref.shape[0]
    shift = mods_ref[3 * slot:3 * slot + 1, :]
    scale = mods_ref[3 * slot + 1:3 * slot + 2, :]
    gate = mods_ref[3 * slot + 2:3 * slot + 3, :]

    @pl.when(j == 0)
    def _():
        def body(r, carry):
            sl = _rows(r, EW_CHUNK)
            h_ref[sl, :] = _mod_norm(x_ref[sl, :], g_ref[...], shift, scale).astype(BF16)
            o_ref[sl, :] = jnp.zeros((EW_CHUNK, o_ref.shape[1]), F32)
            return carry
        lax.fori_loop(0, tm // EW_CHUNK, body, 0)

    wa_s[...] = wa_ref[...].astype(BF16)
    wu_s[...] = wu_ref[...].astype(BF16)
    wd_s[...] = wd_ref[...].astype(BF16)

    for r in range(tm // ROW_CHUNK):
        sl = slice(r * ROW_CHUNK, (r + 1) * ROW_CHUNK)
        h = h_ref[sl, :]
        a = _dot(h, wa_s[...])
        u = _dot(h, wu_s[...])
        act = (_silu(a) * u).astype(BF16)
        o_ref[sl, :] += _dot(act, wd_s[...])

    @pl.when(j == pl.num_programs(1) - 1)
    def _():
        def body(r, carry):
            sl = _rows(r, EW_CHUNK)
            y = x_ref[sl, :] + (0.5 * gate) * o_ref[sl, :]
            if final:
                y = y * lax.rsqrt(jnp.mean(y * y, axis=-1, keepdims=True) + EPS) * fg_ref[...]
            o_ref[sl, :] = y
            return carry
        lax.fori_loop(0, tm // EW_CHUNK, body, 0)


def _ffn_call(x, mods, norm_g, w_up, w_down, *, layer, slot, group0, rows_per_group, final_g=None):
    t, d = x.shape
    f = w_down.shape[1]
    tm, tf = 1024, 256
    nf = f // tf
    group = _group_index(tm, group0, rows_per_group)
    norm_slot = slot
    in_specs = [
        pl.BlockSpec((tm, d), lambda i, j: (i, 0)),
        pl.BlockSpec((None, None, N_MOD, d), lambda i, j: (layer, group(i), 0, 0)),
        pl.BlockSpec((None, None, 1, d), lambda i, j: (layer, norm_slot, 0, 0)),
        pl.BlockSpec((None, d, tf), lambda i, j: (layer, 0, j)),
        pl.BlockSpec((None, d, tf), lambda i, j: (layer, 0, nf + j)),
        pl.BlockSpec((None, tf, d), lambda i, j: (layer, j, 0)),
    ]
    args = [x, mods, norm_g.reshape(norm_g.shape[0], norm_g.shape[1], 1, d), w_up, w_up, w_down]
    if final_g is not None:
        in_specs.append(pl.BlockSpec((1, d), lambda i, j: (0, 0)))
        args.append(final_g.reshape(1, d))
    return pl.pallas_call(
        functools.partial(_ffn_kernel, slot=slot, final=final_g is not None),
        grid=(t // tm, nf),
        in_specs=in_specs,
        out_specs=pl.BlockSpec((tm, d), lambda i, j: (i, 0)),
        out_shape=jax.ShapeDtypeStruct((t, d), F32),
        scratch_shapes=[
            pltpu.VMEM((tm, d), BF16),
            pltpu.VMEM((d, tf), BF16),
            pltpu.VMEM((d, tf), BF16),
            pltpu.VMEM((tf, d), BF16),
        ],
        compiler_params=_params(("parallel", "arbitrary")),
    )(*args)


def _inproj_kernel(x_ref, mods_ref, g_ref, w_ref, z_ref, h_ref, w_s):
    j = pl.program_id(1)
    tm = x_ref.shape[0]
    shift = mods_ref[3:4, :]
    scale = mods_ref[4:5, :]

    @pl.when(j == 0)
    def _():
        def body(r, carry):
            sl = _rows(r, EW_CHUNK)
            h_ref[sl, :] = _mod_norm(x_ref[sl, :], g_ref[...], shift, scale).astype(BF16)
            return carry
        lax.fori_loop(0, tm // EW_CHUNK, body, 0)

    w_s[...] = w_ref[...].astype(BF16)

    for r in range(tm // ROW_CHUNK):
        sl = slice(r * ROW_CHUNK, (r + 1) * ROW_CHUNK)
        z_ref[sl, :] = _dot(h_ref[sl, :], w_s[...])


def _inproj_call(x, mods, norm_g, w_in, *, layer, group0, rows_per_group):
    t, d = x.shape
    n = w_in.shape[2]
    tm, tn = 1024, 1280
    group = _group_index(tm, group0, rows_per_group)
    return pl.pallas_call(
        _inproj_kernel,
        grid=(t // tm, n // tn),
        in_specs=[
            pl.BlockSpec((tm, d), lambda i, j: (i, 0), pipeline_mode=pl.Buffered(1)),
            pl.BlockSpec((None, None, N_MOD, d), lambda i, j: (layer, group(i), 0, 0)),
            pl.BlockSpec((None, None, 1, d), lambda i, j: (layer, 1, 0, 0)),
            pl.BlockSpec((None, d, tn), lambda i, j: (layer, 0, j)),
        ],
        out_specs=pl.BlockSpec((tm, tn), lambda i, j: (i, j)),
        out_shape=jax.ShapeDtypeStruct((t, n), F32),
        scratch_shapes=[pltpu.VMEM((tm, d), BF16), pltpu.VMEM((d, tn), BF16)],
        compiler_params=_params(("parallel", "arbitrary")),
    )(x, mods, norm_g.reshape(norm_g.shape[0], norm_g.shape[1], 1, d), w_in)


def _log_forget(zf, log_lb, log1m_lb):
    ls = jnp.minimum(zf, 0.0) - jnp.log1p(jnp.exp(-jnp.abs(zf)))
    b = log1m_lb + ls
    hi = jnp.maximum(log_lb, b)
    return hi + jnp.log1p(jnp.exp(-jnp.abs(log_lb - b)))


def _hgrn_kernel(zq_ref, zff_ref, zfb_ref, zi_ref, zo_ref, lg_ref, ng_ref, *rest, layer, has_s0):
    if has_s0:
        s0_ref, o_ref, sfin_ref, q_s, k_s, b_s, qb_s, kd_s, dec_s, of_s, ob_s, st_s = rest
    else:
        o_ref, sfin_ref, q_s, k_s, b_s, qb_s, kd_s, dec_s, of_s, ob_s, st_s = rest
    n = zq_ref.shape[0]
    c = HG_CHUNK
    n_chunks = n // c
    blk = HG_BLOCK
    cpb = blk // c
    sh = c.bit_length() - 1

    logits = lg_ref[...]
    depth = logits.shape[0]
    mx = logits[0]
    for i in range(1, depth):
        mx = jnp.maximum(mx, logits[i])
    ex = [jnp.exp(logits[i] - mx) for i in range(depth)]
    tot = ex[0]
    for i in range(1, depth):
        tot = tot + ex[i]
    lb = jnp.zeros_like(mx)
    for i in range(1, layer + 1):
        lb = lb + ex[i] / tot
    log_lb = jnp.log(lb)
    log1m_lb = jnp.log1p(-lb)

    if has_s0:
        st_s[...] = s0_ref[...]
    else:
        st_s[...] = jnp.zeros(st_s.shape, F32)

    br = lax.broadcasted_iota(jnp.int32, (blk, blk), 0)
    bc = lax.broadcasted_iota(jnp.int32, (blk, blk), 1)
    same = (br >> sh) == (bc >> sh)
    cum_mat = [(same & (bc <= br)).astype(BF16), (same & (bc >= br)).astype(BF16)]
    tot_mat = same.astype(BF16)

    def gates(i, carry):
        rows = _rows(i, blk)
        q = _silu(zq_ref[rows, :])
        q_s[rows, :] = q
        for d in range(2):
            zf = (zff_ref if d == 0 else zfb_ref)[rows, :]
            lf = _log_forget(zf, log_lb[d:d + 1, :], log1m_lb[d:d + 1, :])
            k = -_expm1(lf)
            hi = lf.astype(BF16)
            lo = (lf - hi.astype(F32)).astype(BF16)
            b = _dot(cum_mat[d], hi) + _dot(cum_mat[d], lo)
            b_tot = _dot(tot_mat, hi) + _dot(tot_mat, lo)
            k_s[d, rows, :] = k
            b_s[d, rows, :] = b
            qb_s[d, rows, :] = (q * jnp.exp(b)).astype(BF16)
            kd_s[d, rows, :] = (k * jnp.exp(b_tot - b)).astype(BF16)
            dec_s[d, rows, :] = jnp.exp(b_tot)
        return carry
    lax.fori_loop(0, n // blk, gates, 0)

    nh = HG_HEADS
    hs = [slice(h * HG_DK, (h + 1) * HG_DK) for h in range(nh)]
    s8 = lax.broadcasted_iota(jnp.int32, (SUBLANE, nh * HG_DK), 0)
    ones = jnp.ones((HG_DK, LANE), BF16)
    sel_r = lax.broadcasted_iota(jnp.int32, (nh * c, nh * c * c), 0)
    sel_c = lax.broadcasted_iota(jnp.int32, (nh * c, nh * c * c), 1)
    sel = ((sel_c >= sel_r * c) & (sel_c < sel_r * c + c)).astype(BF16)
    zero8 = jnp.zeros((SUBLANE, nh * HG_DK), F32)

    def scores_of(d, rows):
        q = q_s[rows, :]
        k = k_s[d, rows, :]
        b = b_s[d, rows, :]
        halves = [(b[:SUBLANE], k[:SUBLANE]), (b[SUBLANE:], k[SUBLANE:])]
        parts = []
        for t in range(c):
            bt, qt = b[t:t + 1, :], q[t:t + 1, :]
            own = t // SUBLANE
            row = []
            for half, (bh, kh) in enumerate(halves):
                if half == own:
                    tt = t - own * SUBLANE
                    vis = (s8 <= tt) if d == 0 else (s8 >= tt)
                    row.append((qt * jnp.exp(jnp.where(vis, bt - bh, NEG))) * kh)
                elif (half < own) == (d == 0):
                    row.append((qt * jnp.exp(bt - bh)) * kh)
                else:
                    row.append(zero8)
            parts.append(row)
        p = jnp.concatenate([parts[t][half][:, hs[h]] for h in range(nh) for t in range(c) for half in range(2)],
                            axis=0).astype(BF16)
        return _dot(p, ones)

    def state_step(d, h, rows, ci):
        st = st_s[d, h]
        o_inter = _dot_nt(qb_s[d, rows, hs[h]], st.astype(BF16))
        dec8 = dec_s[d, pl.ds(pl.multiple_of(ci * c, c), SUBLANE), hs[h]]
        st_dec = (st.reshape(HG_DK // SUBLANE, SUBLANE, HG_DK) * dec8[None]).reshape(HG_DK, HG_DK)
        st_s[d, h] = st_dec + _dot_tn(zi_ref[rows, hs[h]].astype(BF16), kd_s[d, rows, hs[h]])
        return o_inter

    def body(i, carry):
        cis = (i, n_chunks - 1 - i)
        rows = [_rows(ci, c) for ci in cis]
        scores = [scores_of(d, rows[d]) for d in range(2)]
        o_inter = [[state_step(d, h, rows[d], cis[d]) for h in range(nh)] for d in range(2)]
        for d in range(2):
            v = zi_ref[rows[d], :]
            v_rep = jnp.concatenate([v[:, hs[h]] for h in range(nh) for _ in range(c)], axis=0)
            o_intra = _dot(sel, (scores[d] * v_rep).astype(BF16))
            for h in range(nh):
                (of_s if d == 0 else ob_s)[rows[d], hs[h]] = o_intra[h * c:(h + 1) * c, :] + o_inter[d][h]
        return carry
    lax.fori_loop(0, n_chunks, body, 0)

    sfin_ref[...] = st_s[...]

    def fin(r, carry):
        rows = _rows(r, EW_CHUNK)
        for h in range(HG_HEADS):
            cols = slice(h * HG_DK, (h + 1) * HG_DK)
            o = of_s[rows, cols] + ob_s[rows, cols]
            o = o * lax.rsqrt(jnp.mean(o * o, axis=-1, keepdims=True) + EPS) * ng_ref[h:h + 1, :]
            o_ref[rows, cols] = (o * _silu(zo_ref[rows, cols])).astype(o_ref.dtype)
        return carry
    lax.fori_loop(0, n // EW_CHUNK, fin, 0)


def _hgrn_call(z, lb_logits, hgrn_norm_g, s0_t, *, layer, n_seq, n_tok):
    w = HG_HEADS * HG_DK
    depth = lb_logits.shape[0]
    zspec = lambda k: pl.BlockSpec((n_tok, w), lambda s: (s, k))
    in_specs = [zspec(0), zspec(1), zspec(2), zspec(3), zspec(4),
                pl.BlockSpec((depth, 2, w), lambda s: (0, 0, 0)),
                pl.BlockSpec((None, HG_HEADS, HG_DK), lambda s: (layer, 0, 0))]
    args = [z, z, z, z, z, lb_logits, hgrn_norm_g]
    st_spec = pl.BlockSpec((None, 2, HG_HEADS, HG_DK, HG_DK), lambda s: (s, 0, 0, 0, 0))
    if s0_t is not None:
        in_specs.append(st_spec)
        args.append(s0_t)
    return pl.pallas_call(
        functools.partial(_hgrn_kernel, layer=layer, has_s0=s0_t is not None),
        grid=(n_seq,),
        in_specs=in_specs,
        out_specs=[pl.BlockSpec((n_tok, w), lambda s: (s, 0)), st_spec],
        out_shape=[jax.ShapeDtypeStruct((n_seq * n_tok, w), BF16),
                   jax.ShapeDtypeStruct((n_seq, 2, HG_HEADS, HG_DK, HG_DK), F32)],
        scratch_shapes=[pltpu.VMEM((n_tok, w), F32),
                        pltpu.VMEM((2, n_tok, w), F32),
                        pltpu.VMEM((2, n_tok, w), F32),
                        pltpu.VMEM((2, n_tok, w), BF16),
                        pltpu.VMEM((2, n_tok, w), BF16),
                        pltpu.VMEM((2, n_tok, w), F32),
                        pltpu.VMEM((n_tok, w), F32), pltpu.VMEM((n_tok, w), F32),
                        pltpu.VMEM((2, HG_HEADS, HG_DK, HG_DK), F32)],
        compiler_params=_params(("parallel",)),
    )(*args)


def _ctx_attn_kernel(q0_ref, q1_ref, k0_ref, k1_ref, v0_ref, v1_ref, o_ref, ko_ref, vo_ref, *, scale):
    q_refs, k_refs, v_refs = (q0_ref, q1_ref), (k0_ref, k1_ref), (v0_ref, v1_ref)
    half = q0_ref.shape[1]
    heads_per_block = half // NA_HEAD_DIM
    ko_ref[:, :half] = k0_ref[...]
    ko_ref[:, half:] = k1_ref[...]
    vo_ref[:, :half] = v0_ref[...]
    vo_ref[:, half:] = v1_ref[...]
    for h in range(NA_HEADS):
        blk, off = divmod(h, heads_per_block)
        cols = slice(off * NA_HEAD_DIM, (off + 1) * NA_HEAD_DIM)
        s = _dot_nt(q_refs[blk][:, cols].astype(BF16), k_refs[blk][:, cols].astype(BF16)) * scale
        p = jnp.exp(s - jnp.max(s, axis=-1, keepdims=True))
        p = p / jnp.sum(p, axis=-1, keepdims=True)
        o = _dot(p.astype(BF16), v_refs[blk][:, cols].astype(BF16))
        o_ref[:, h * NA_HEAD_DIM:(h + 1) * NA_HEAD_DIM] = o.astype(o_ref.dtype)


def _ctx_attn_call(z, *, n_seq, n_tok, col0):
    w = NA_HEADS * NA_HEAD_DIM
    half = w // 2
    zspec = lambda k: pl.BlockSpec((n_tok, half), lambda s: (s, col0 // half + k))
    return pl.pallas_call(
        functools.partial(_ctx_attn_kernel, scale=NA_HEAD_DIM ** -0.5),
        grid=(n_seq,),
        in_specs=[zspec(k) for k in range(6)],
        out_specs=[pl.BlockSpec((n_tok, w), lambda s: (s, 0))] * 3,
        out_shape=[jax.ShapeDtypeStruct((n_seq * n_tok, w), BF16),
                   jax.ShapeDtypeStruct((n_seq * n_tok, w), F32),
                   jax.ShapeDtypeStruct((n_seq * n_tok, w), F32)],
        compiler_params=_params(("parallel",)),
    )(z, z, z, z, z, z)


def _na_kernel(q_ref, k_ref, v_ref, kc_ref, vc_ref, bias_ref, o_ref, *, scale, rows, kh):
    kc = kc_ref[...].astype(BF16)
    vc = vc_ref[...].astype(BF16)
    for r in range(rows):
        start = min(max(r - kh // 2, 0), rows - kh) * GRID_W
        q = q_ref[r * GRID_W:(r + 1) * GRID_W, :].astype(BF16)
        k_loc = k_ref[start:start + kh * GRID_W, :].astype(BF16)
        v_loc = v_ref[start:start + kh * GRID_W, :].astype(BF16)
        s_loc = _dot_nt(q, k_loc) * scale + bias_ref[r]
        s_ctx = _dot_nt(q, kc) * scale
        m = jnp.maximum(jnp.max(s_loc, axis=-1, keepdims=True), jnp.max(s_ctx, axis=-1, keepdims=True))
        p_loc = jnp.exp(s_loc - m)
        p_ctx = jnp.exp(s_ctx - m)
        den = jnp.sum(p_loc, axis=-1, keepdims=True) + jnp.sum(p_ctx, axis=-1, keepdims=True)
        o = _dot((p_loc / den).astype(BF16), v_loc) + _dot((p_ctx / den).astype(BF16), vc)
        o_ref[r * GRID_W:(r + 1) * GRID_W, :] = o.astype(o_ref.dtype)


def _na_bias(rpb, rows, kh):
    heads = rpb.shape[0]
    qcol = jnp.arange(GRID_W)
    kcol = jnp.arange(GRID_W)
    win_start = jnp.clip(qcol - NA_KW // 2, 0, GRID_W - NA_KW)
    in_win = (kcol[None, :] >= win_start[:, None]) & (kcol[None, :] < win_start[:, None] + NA_KW)
    lo = GRID_W - NA_KW
    pad = jnp.pad(rpb.astype(F32), ((0, 0), (0, 0), (lo, lo)))
    toep = jnp.stack([pad[:, :, GRID_W - 1 - q:2 * GRID_W - 1 - q] for q in range(GRID_W)], axis=2)
    toep = jnp.where(in_win[None, None], toep, NEG)
    per_row = []
    for r in range(rows):
        dy0 = min(max(r - kh // 2, 0), rows - kh) - r + NA_KH - 1
        per_row.append(toep[:, dy0:dy0 + kh])
    bias = jnp.stack(per_row, axis=1)
    return jnp.transpose(bias, (0, 1, 3, 2, 4)).reshape(heads, rows, GRID_W, kh * GRID_W)


def _na_call(z, cache_k, cache_v, bias, *, layer, n_seq, n_tok, col_block0):
    rows = n_tok // GRID_W
    kh = min(NA_KH, rows)
    dh = NA_HEAD_DIM
    past = cache_k.shape[2]
    zspec = lambda k: pl.BlockSpec((n_tok, dh), lambda b, h: (b, col_block0 + k * NA_HEADS + h))
    cspec = pl.BlockSpec((None, None, past, dh), lambda b, h: (b, layer, 0, h))
    return pl.pallas_call(
        functools.partial(_na_kernel, scale=dh ** -0.5, rows=rows, kh=kh),
        grid=(n_seq, NA_HEADS),
        in_specs=[zspec(0), zspec(1), zspec(2), cspec, cspec,
                  pl.BlockSpec((None, rows, GRID_W, kh * GRID_W), lambda b, h: (h, 0, 0, 0))],
        out_specs=pl.BlockSpec((n_tok, dh), lambda b, h: (b, h)),
        out_shape=jax.ShapeDtypeStruct((n_seq * n_tok, NA_HEADS * dh), BF16),
        compiler_params=_params(("parallel", "parallel")),
    )(z, z, z, cache_k, cache_v, bias)


def _lru_kernel(zx_ref, zg_ref, cw_ref, cb_ref, wa_ref, ba_ref, wx_ref, bx_ref, lam_ref, *rest,
                n_seq, n_tok, has_h0):
    if has_h0:
        h0_ref, y_ref, hfin_ref, a_s, u_s = rest
    else:
        y_ref, hfin_ref, a_s, u_s = rest
    tpos = lax.broadcasted_iota(jnp.int32, (n_tok, LANE), 0)
    neg_lam = -lam_ref[...]
    softplus = jnp.maximum(neg_lam, 0.0) + jnp.log1p(jnp.exp(-jnp.abs(neg_lam)))
    left = LRU_CONV // 2

    def gates(g, carry):
        rows = _rows(g, n_tok)
        zx = zx_ref[rows, :]
        x = cb_ref[...] + cw_ref[left:left + 1, :] * zx
        for j in range(LRU_CONV):
            off = j - left
            if off == 0:
                continue
            shifted = pltpu.roll(zx, (-off) % n_tok, 0)
            valid = (tpos + off >= 0) & (tpos + off < n_tok)
            x = x + cw_ref[j:j + 1, :] * jnp.where(valid, shifted, 0.0)
        xb = x.astype(BF16)
        for d in range(2):
            r_gate = jax.nn.sigmoid(_dot(xb, wa_ref[d].astype(BF16)) + ba_ref[d:d + 1, :])
            i_gate = jax.nn.sigmoid(_dot(xb, wx_ref[d].astype(BF16)) + bx_ref[d:d + 1, :])
            log_a = (-LRU_C * r_gate) * softplus[d:d + 1, :]
            a_s[d, rows, :] = jnp.exp(log_a)
            u_s[d, rows, :] = jnp.sqrt(-_expm1(2.0 * log_a)) * (i_gate * x)
        return carry
    lax.fori_loop(0, n_seq, gates, 0)

    if has_h0:
        h_init = (h0_ref[0], h0_ref[1])
    else:
        h_init = (jnp.zeros((n_seq, LANE), F32), jnp.zeros((n_seq, LANE), F32))

    def step(t, carry):
        hf, hb = carry
        fw = pl.ds(t, n_seq, stride=n_tok)
        bw = pl.ds(n_tok - 1 - t, n_seq, stride=n_tok)
        hf = a_s[0, fw, :] * hf + u_s[0, fw, :]
        hb = a_s[1, bw, :] * hb + u_s[1, bw, :]
        u_s[0, fw, :] = hf
        u_s[1, bw, :] = hb
        return hf, hb
    hf, hb = lax.fori_loop(0, n_tok, step, h_init)
    hfin_ref[0] = hf
    hfin_ref[1] = hb

    def fin(g, carry):
        rows = _rows(g, n_tok)
        y = (u_s[0, rows, :] + u_s[1, rows, :]) * jax.nn.gelu(zg_ref[rows, :], approximate=True)
        y_ref[rows, :] = y.astype(y_ref.dtype)
        return carry
    lax.fori_loop(0, n_seq, fin, 0)


def _block_diag_pairs(w):
    depth, nd, nb, bw, _ = w.shape
    w = w.reshape(depth, nd, nb // 2, 2, bw, bw)
    eye = jnp.eye(2, dtype=w.dtype)
    out = w[:, :, :, :, :, None, :] * eye[None, None, None, :, None, :, None]
    return out.reshape(depth, nd, nb // 2, 2 * bw, 2 * bw)


def _lru_call(z, conv_w, conv_b, wa_bd, b_a, wx_bd, b_x, lam, h0_t, *, layer, n_seq, n_tok, col_block0):
    w = conv_b.shape[1]
    n_cb = w // LANE
    rows = n_seq * n_tok
    zspec = lambda k: pl.BlockSpec((rows, LANE), lambda cb: (0, col_block0 + k * n_cb + cb))
    vec2 = pl.BlockSpec((None, 2, LANE), lambda cb: (layer, 0, cb))
    wspec = pl.BlockSpec((None, 2, None, LANE, LANE), lambda cb: (layer, 0, cb, 0, 0))
    hspec = pl.BlockSpec((2, n_seq, LANE), lambda cb: (0, 0, cb))
    in_specs = [zspec(0), zspec(1),
                pl.BlockSpec((None, LRU_CONV, LANE), lambda cb: (layer, 0, cb)),
                pl.BlockSpec((None, 1, LANE), lambda cb: (layer, 0, cb)),
                wspec, vec2, wspec, vec2, vec2]
    args = [z, z, conv_w, conv_b.reshape(conv_b.shape[0], 1, w), wa_bd, b_a, wx_bd, b_x, lam]
    if h0_t is not None:
        in_specs.append(hspec)
        args.append(h0_t)
    return pl.pallas_call(
        functools.partial(_lru_kernel, n_seq=n_seq, n_tok=n_tok, has_h0=h0_t is not None),
        grid=(n_cb,),
        in_specs=in_specs,
        out_specs=[pl.BlockSpec((rows, LANE), lambda cb: (0, cb)), hspec],
        out_shape=[jax.ShapeDtypeStruct((rows, w), BF16), jax.ShapeDtypeStruct((2, n_seq, w), F32)],
        scratch_shapes=[pltpu.VMEM((2, rows, LANE), F32), pltpu.VMEM((2, rows, LANE), F32)],
        compiler_params=_params(("parallel",)),
    )(*args)


def _merge_kernel(x_ref, mods_ref, oa_ref, ob_ref, oc_ref, ga_ref, gb_ref, gc_ref,
                  wa_ref, wb_ref, wc_ref, wo_ref, o_ref, wa_s, wb_s, wc_s, wo_s):
    j = pl.program_id(1)
    tm = x_ref.shape[0]
    gate = mods_ref[5:6, :]

    @pl.when(j == 0)
    def _():
        def body(r, carry):
            o_ref[_rows(r, EW_CHUNK), :] = jnp.zeros((EW_CHUNK, o_ref.shape[1]), F32)
            return carry
        lax.fori_loop(0, tm // EW_CHUNK, body, 0)

    wa_s[...] = wa_ref[...].astype(BF16)
    wb_s[...] = wb_ref[...].astype(BF16)
    wc_s[...] = wc_ref[...].astype(BF16)
    wo_s[...] = wo_ref[...].astype(BF16)

    for r in range(tm // ROW_CHUNK):
        sl = slice(r * ROW_CHUNK, (r + 1) * ROW_CHUNK)

        def branch(o_r, w_s, g_r):
            return jax.nn.sigmoid(g_r[sl, :]) * _dot(o_r[sl, :], w_s[...])

        m = branch(oa_ref, wa_s, ga_ref) + branch(ob_ref, wb_s, gb_ref) + branch(oc_ref, wc_s, gc_ref)
        o_ref[sl, :] += _dot(m.astype(BF16), wo_s[...])

    @pl.when(j == pl.num_programs(1) - 1)
    def _():
        def body(r, carry):
            sl = _rows(r, EW_CHUNK)
            o_ref[sl, :] = x_ref[sl, :] + gate * o_ref[sl, :]
            return carry
        lax.fori_loop(0, tm // EW_CHUNK, body, 0)


def _merge_call(x, mods, o_a, o_b, o_c, z, w_proj_a, w_proj_b, w_proj_c, w_out, *, layer, group0,
                rows_per_group, gate_col0):
    t, d = x.shape
    tm, tn = 1024, 256
    nj = d // tn
    g0 = gate_col0 // tn
    group = _group_index(tm, group0, rows_per_group)
    gspec = lambda k: pl.BlockSpec((tm, tn), lambda i, j: (i, g0 + k * nj + j))
    ospec = lambda w: pl.BlockSpec((tm, w), lambda i, j: (i, 0))
    wspec = lambda w: pl.BlockSpec((None, w, tn), lambda i, j: (layer, 0, j))
    return pl.pallas_call(
        _merge_kernel,
        grid=(t // tm, nj),
        in_specs=[
            pl.BlockSpec((tm, d), lambda i, j: (i, 0), pipeline_mode=pl.Buffered(1)),
            pl.BlockSpec((None, None, N_MOD, d), lambda i, j: (layer, group(i), 0, 0)),
            ospec(o_a.shape[1]), ospec(o_b.shape[1]), ospec(o_c.shape[1]),
            gspec(0), gspec(1), gspec(2),
            wspec(o_a.shape[1]), wspec(o_b.shape[1]), wspec(o_c.shape[1]),
            pl.BlockSpec((None, tn, d), lambda i, j: (layer, j, 0)),
        ],
        out_specs=pl.BlockSpec((tm, d), lambda i, j: (i, 0)),
        out_shape=jax.ShapeDtypeStruct((t, d), F32),
        scratch_shapes=[pltpu.VMEM((o_a.shape[1], tn), BF16), pltpu.VMEM((o_b.shape[1], tn), BF16),
                        pltpu.VMEM((o_c.shape[1], tn), BF16), pltpu.VMEM((tn, d), BF16)],
        compiler_params=_params(("parallel", "arbitrary")),
    )(x, mods, o_a, o_b, o_c, z, z, z, w_proj_a, w_proj_b, w_proj_c, w_out)


def kernel(x_prompt, x_sample, cache_na_k, cache_na_v, state_hgrn, state_lru, c, c_ctx, mod_w, mod_b, norm_g, ffn1_w_up, ffn1_w_down, ffn2_w_up, ffn2_w_down, w_in, hgrn_lb_logits, hgrn_norm_g, na_rpb, lru_conv_w, lru_conv_b, lru_w_a, lru_b_a, lru_w_x, lru_b_x, lru_lambda, w_proj_a, w_proj_b, w_proj_c, w_out, final_norm_g):
    b_ctx, n_ctx, d = x_prompt.shape
    b_lat, n_lat, _ = x_sample.shape
    depth = mod_w.shape[0]
    t_ctx, t_lat = b_ctx * n_ctx, b_lat * n_lat
    hg_w = HG_HEADS * HG_DK
    na_w = NA_HEADS * NA_HEAD_DIM
    lru_w = lru_conv_b.shape[1]
    na_col0 = 5 * hg_w
    lru_col0 = na_col0 + 3 * na_w
    gate_col0 = lru_col0 + 2 * lru_w

    ctx = dict(group0=0, rows_per_group=t_ctx)
    lat = dict(group0=1, rows_per_group=n_lat)
    xc = x_prompt.reshape(t_ctx, d)
    xl = x_sample.reshape(t_lat, d)
    cond = jnp.concatenate([c_ctx[None], c, jnp.zeros((N_COND_ROWS - 1 - b_lat, d), F32)], axis=0)
    mods = _mods_call(cond, mod_w, mod_b).reshape(depth, N_COND_ROWS, N_MOD, d)

    rows = n_lat // GRID_W
    kh = min(NA_KH, rows)
    wa_bd = _block_diag_pairs(lru_w_a)
    wx_bd = _block_diag_pairs(lru_w_x)
    cache_k = cache_na_k.reshape(b_lat, depth, cache_na_k.shape[2], na_w)
    cache_v = cache_na_v.reshape(b_lat, depth, cache_na_v.shape[2], na_w)
    s0_t = jnp.swapaxes(state_hgrn, -1, -2)
    h0_t = jnp.transpose(state_lru, (1, 2, 0, 3))

    ks, vs, hgs, lrus = [], [], [], []
    for l in range(depth):
        last = l == depth - 1
        final_g = final_norm_g if last else None
        lru_args = (lru_conv_w, lru_conv_b, wa_bd, lru_b_a, wx_bd, lru_b_x, lru_lambda)
        merge_w = (w_proj_a, w_proj_b, w_proj_c, w_out)

        xc = _ffn_call(xc, mods, norm_g, ffn1_w_up, ffn1_w_down, layer=l, slot=0, **ctx)
        zc = _inproj_call(xc, mods, norm_g, w_in, layer=l, **ctx)
        oa, s_c = _hgrn_call(zc, hgrn_lb_logits, hgrn_norm_g, None, layer=l, n_seq=b_ctx, n_tok=n_ctx)
        ob, k_l, v_l = _ctx_attn_call(zc, n_seq=b_ctx, n_tok=n_ctx, col0=na_col0)
        oc, h_c = _lru_call(zc, *lru_args, None, layer=l, n_seq=b_ctx, n_tok=n_ctx, col_block0=lru_col0 // LANE)
        xc = _merge_call(xc, mods, oa, ob, oc, zc, *merge_w, layer=l, gate_col0=gate_col0, **ctx)
        xc = _ffn_call(xc, mods, norm_g, ffn2_w_up, ffn2_w_down, layer=l, slot=2, final_g=final_g, **ctx)

        xl = _ffn_call(xl, mods, norm_g, ffn1_w_up, ffn1_w_down, layer=l, slot=0, **lat)
        zl = _inproj_call(xl, mods, norm_g, w_in, layer=l, **lat)
        oa, _ = _hgrn_call(zl, hgrn_lb_logits, hgrn_norm_g, s0_t[:, l], layer=l, n_seq=b_lat, n_tok=n_lat)
        bias = _na_bias(na_rpb[l], rows, kh)
        ob = _na_call(zl, cache_k, cache_v, bias, layer=l, n_seq=b_lat, n_tok=n_lat,
                      col_block0=na_col0 // NA_HEAD_DIM)
        oc, _ = _lru_call(zl, *lru_args, h0_t[l], layer=l, n_seq=b_lat, n_tok=n_lat, col_block0=lru_col0 // LANE)
        xl = _merge_call(xl, mods, oa, ob, oc, zl, *merge_w, layer=l, gate_col0=gate_col0, **lat)
        xl = _ffn_call(xl, mods, norm_g, ffn2_w_up, ffn2_w_down, layer=l, slot=2, final_g=final_g, **lat)

        ks.append(k_l.reshape(b_ctx, n_ctx, NA_HEADS, NA_HEAD_DIM))
        vs.append(v_l.reshape(b_ctx, n_ctx, NA_HEADS, NA_HEAD_DIM))
        hgs.append(jnp.swapaxes(s_c, -1, -2))
        lrus.append(jnp.transpose(h_c, (1, 0, 2)))

    return (xc.reshape(b_ctx, n_ctx, d), xl.reshape(b_lat, n_lat, d), jnp.stack(ks, axis=1),
            jnp.stack(vs, axis=1), jnp.stack(hgs, axis=1), jnp.stack(lrus, axis=1))
```

```python
import functools

import jax
import jax.numpy as jnp
from jax import lax
from jax.experimental import pallas as pl
from jax.experimental.pallas import tpu as pltpu

F32 = jnp.float32
BF16 = jnp.bfloat16

EPS = 1e-6
NEG = -1e30
N_MOD = 9
N_COND_ROWS = 8
HG_HEADS = 4
HG_DK = 128
HG_CHUNK = 16
HG_BLOCK = 128
NA_HEADS = 8
NA_HEAD_DIM = 128
NA_KH = 8
NA_KW = 16
NA_QBW = 8
NA_BAND = NA_QBW + NA_KW
GRID_W = 64
LRU_BLOCKS = 8
LRU_CONV = 4
LRU_C = 8.0
LANE = 128
SUBLANE = 8
VMEM_LIMIT = 56 * 1024 * 1024


def _params(sem, vmem=VMEM_LIMIT):
    return pltpu.CompilerParams(dimension_semantics=sem, vmem_limit_bytes=vmem)


def _silu(x):
    return x * jax.nn.sigmoid(x)


def _expm1(x):
    u = jnp.exp(x)
    um1 = u - 1.0
    y = um1 * x / jnp.where(u == 1.0, 1.0, jnp.log(u))
    return jnp.where(u == 1.0, x, jnp.where(um1 == -1.0, -1.0, y))


def _dot(a, b):
    return jnp.dot(a, b, preferred_element_type=F32)


def _dot_nt(a, b):
    return lax.dot_general(a, b, (((1,), (1,)), ((), ())), preferred_element_type=F32)


def _dot_tn(a, b):
    return lax.dot_general(a, b, (((0,), (0,)), ((), ())), preferred_element_type=F32)


def _mod_norm(x, g, shift, scale):
    y = x * lax.rsqrt(jnp.mean(x * x, axis=-1, keepdims=True) + EPS)
    return (y * g) * (1.0 + scale) + shift


def _group_index(tm, group0, rows_per_group):
    tiles_per_group = rows_per_group // tm
    return lambda i: group0 + i // tiles_per_group


def _mods_kernel(c_ref, w_ref, b_ref, o_ref):
    s = _silu(c_ref[...]).astype(BF16)
    o_ref[...] = _dot(s, w_ref[...].astype(BF16)) + b_ref[...]


def _mods_call(cond, mod_w, mod_b):
    depth, d, n = mod_w.shape
    tn = 1024
    return pl.pallas_call(
        _mods_kernel,
        grid=(depth, n // tn),
        in_specs=[
            pl.BlockSpec((N_COND_ROWS, d), lambda l, j: (0, 0)),
            pl.BlockSpec((None, d, tn), lambda l, j: (l, 0, j)),
            pl.BlockSpec((None, 1, tn), lambda l, j: (l, 0, j)),
        ],
        out_specs=pl.BlockSpec((None, N_COND_ROWS, tn), lambda l, j: (l, 0, j)),
        out_shape=jax.ShapeDtypeStruct((depth, N_COND_ROWS, n), F32),
        compiler_params=_params(("parallel", "parallel")),
    )(cond, mod_w, mod_b.reshape(depth, 1, n))


ROW_CHUNK = 256
EW_CHUNK = 64


def _rows(r, n):
    return pl.ds(pl.multiple_of(r * n, n), n)


def _ffn_kernel(x_ref, mods_ref, g_ref, wa_ref, wu_ref, wd_ref, *rest, slot, final):
    if final:
        fg_ref, o_ref, h_ref = rest
    else:
        o_ref, h_ref = rest
    j = pl.program_id(1)
    tm = x_ref.shape[0]
    shift = mods_ref[3 * slot:3 * slot + 1, :]
    scale = mods_ref[3 * slot + 1:3 * slot + 2, :]
    gate = mods_ref[3 * slot + 2:3 * slot + 3, :]

    @pl.when(j == 0)
    def _():
        def body(r, carry):
            sl = _rows(r, EW_CHUNK)
            h_ref[sl, :] = _mod_norm(x_ref[sl, :], g_ref[...], shift, scale).astype(BF16)
            o_ref[sl, :] = jnp.zeros((EW_CHUNK, o_ref.shape[1]), F32)
            return carry
        lax.fori_loop(0, tm // EW_CHUNK, body, 0)

    for r in range(tm // ROW_CHUNK):
        sl = slice(r * ROW_CHUNK, (r + 1) * ROW_CHUNK)
        h = h_ref[sl, :]
        a = _dot(h, wa_ref[...].astype(BF16))
        u = _dot(h, wu_ref[...].astype(BF16))
        act = (_silu(a) * u).astype(BF16)
        o_ref[sl, :] += _dot(act, wd_ref[...].astype(BF16))

    @pl.when(j == pl.num_programs(1) - 1)
    def _():
        def body(r, carry):
            sl = _rows(r, EW_CHUNK)
            y = x_ref[sl, :] + (0.5 * gate) * o_ref[sl, :]
            if final:
                y = y * lax.rsqrt(jnp.mean(y * y, axis=-1, keepdims=True) + EPS) * fg_ref[...]
            o_ref[sl, :] = y
            return carry
        lax.fori_loop(0, tm // EW_CHUNK, body, 0)


def _ffn_call(x, mods, norm_g, w_up, w_down, *, layer, slot, group0, rows_per_group, final_g=None):
    t, d = x.shape
    f = w_down.shape[1]
    tm, tf = 1024, 256
    nf = f // tf
    group = _group_index(tm, group0, rows_per_group)
    norm_slot = slot
    in_specs = [
        pl.BlockSpec((tm, d), lambda i, j: (i, 0)),
        pl.BlockSpec((None, None, N_MOD, d), lambda i, j: (layer, group(i), 0, 0)),
        pl.BlockSpec((None, None, 1, d), lambda i, j: (layer, norm_slot, 0, 0)),
        pl.BlockSpec((None, d, tf), lambda i, j: (layer, 0, j)),
        pl.BlockSpec((None, d, tf), lambda i, j: (layer, 0, nf + j)),
        pl.BlockSpec((None, tf, d), lambda i, j: (layer, j, 0)),
    ]
    args = [x, mods, norm_g.reshape(norm_g.shape[0], norm_g.shape[1], 1, d), w_up, w_up, w_down]
    if final_g is not None:
        in_specs.append(pl.BlockSpec((1, d), lambda i, j: (0, 0)))
        args.append(final_g.reshape(1, d))
    return pl.pallas_call(
        functools.partial(_ffn_kernel, slot=slot, final=final_g is not None),
        grid=(t // tm, nf),
        in_specs=in_specs,
        out_specs=pl.BlockSpec((tm, d), lambda i, j: (i, 0)),
        out_shape=jax.ShapeDtypeStruct((t, d), F32),
        scratch_shapes=[pltpu.VMEM((tm, d), BF16)],
        compiler_params=_params(("parallel", "arbitrary")),
    )(*args)


def _inproj_kernel(x_ref, mods_ref, g_ref, w_ref, z_ref, h_ref, w_s):
    j = pl.program_id(1)
    tm = x_ref.shape[0]
    shift = mods_ref[3:4, :]
    scale = mods_ref[4:5, :]

    @pl.when(j == 0)
    def _():
        def body(r, carry):
            sl = _rows(r, EW_CHUNK)
            h_ref[sl, :] = _mod_norm(x_ref[sl, :], g_ref[...], shift, scale).astype(BF16)
            return carry
        lax.fori_loop(0, tm // EW_CHUNK, body, 0)

    w_s[...] = w_ref[...].astype(BF16)

    for r in range(tm // ROW_CHUNK):
        sl = slice(r * ROW_CHUNK, (r + 1) * ROW_CHUNK)
        z_ref[sl, :] = _dot(h_ref[sl, :], w_s[...])


def _inproj_call(x, mods, norm_g, w_in, *, layer, group0, rows_per_group):
    t, d = x.shape
    n = w_in.shape[2]
    tm, tn = 1024, 1280
    group = _group_index(tm, group0, rows_per_group)
    return pl.pallas_call(
        _inproj_kernel,
        grid=(t // tm, n // tn),
        in_specs=[
            pl.BlockSpec((tm, d), lambda i, j: (i, 0), pipeline_mode=pl.Buffered(1)),
            pl.BlockSpec((None, None, N_MOD, d), lambda i, j: (layer, group(i), 0, 0)),
            pl.BlockSpec((None, None, 1, d), lambda i, j: (layer, 1, 0, 0)),
            pl.BlockSpec((None, d, tn), lambda i, j: (layer, 0, j)),
        ],
        out_specs=pl.BlockSpec((tm, tn), lambda i, j: (i, j)),
        out_shape=jax.ShapeDtypeStruct((t, n), F32),
        scratch_shapes=[pltpu.VMEM((tm, d), BF16), pltpu.VMEM((d, tn), BF16)],
        compiler_params=_params(("parallel", "arbitrary")),
    )(x, mods, norm_g.reshape(norm_g.shape[0], norm_g.shape[1], 1, d), w_in)


def _log_forget(zf, log_lb, log1m_lb):
    ls = jnp.minimum(zf, 0.0) - jnp.log1p(jnp.exp(-jnp.abs(zf)))
    b = log1m_lb + ls
    hi = jnp.maximum(log_lb, b)
    return hi + jnp.log1p(jnp.exp(-jnp.abs(log_lb - b)))


def _hgrn_kernel(zq_ref, zff_ref, zfb_ref, zi_ref, zo_ref, lg_ref, ng_ref, *rest, layer, has_s0):
    if has_s0:
        s0_ref, o_ref, sfin_ref, q_s, k_s, b_s, qb_s, kd_s, dec_s, of_s, ob_s, st_s = rest
    else:
        o_ref, sfin_ref, q_s, k_s, b_s, qb_s, kd_s, dec_s, of_s, ob_s, st_s = rest
    n = zq_ref.shape[0]
    c = HG_CHUNK
    n_chunks = n // c
    blk = HG_BLOCK
    cpb = blk // c
    sh = c.bit_length() - 1

    logits = lg_ref[...]
    depth = logits.shape[0]
    mx = logits[0]
    for i in range(1, depth):
        mx = jnp.maximum(mx, logits[i])
    ex = [jnp.exp(logits[i] - mx) for i in range(depth)]
    tot = ex[0]
    for i in range(1, depth):
        tot = tot + ex[i]
    lb = jnp.zeros_like(mx)
    for i in range(1, layer + 1):
        lb = lb + ex[i] / tot
    log_lb = jnp.log(lb)
    log1m_lb = jnp.log1p(-lb)

    if has_s0:
        st_s[...] = s0_ref[...]
    else:
        st_s[...] = jnp.zeros(st_s.shape, F32)

    br = lax.broadcasted_iota(jnp.int32, (blk, blk), 0)
    bc = lax.broadcasted_iota(jnp.int32, (blk, blk), 1)
    same = (br >> sh) == (bc >> sh)
    cum_mat = [(same & (bc <= br)).astype(BF16), (same & (bc >= br)).astype(BF16)]
    tot_mat = same.astype(BF16)

    def gates(i, carry):
        rows = _rows(i, blk)
        q = _silu(zq_ref[rows, :])
        q_s[rows, :] = q
        for d in range(2):
            zf = (zff_ref if d == 0 else zfb_ref)[rows, :]
            lf = _log_forget(zf, log_lb[d:d + 1, :], log1m_lb[d:d + 1, :])
            k = -_expm1(lf)
            hi = lf.astype(BF16)
            lo = (lf - hi.astype(F32)).astype(BF16)
            b = _dot(cum_mat[d], hi) + _dot(cum_mat[d], lo)
            b_tot = _dot(tot_mat, hi) + _dot(tot_mat, lo)
            k_s[d, rows, :] = k
            b_s[d, rows, :] = b
            qb_s[d, rows, :] = (q * jnp.exp(b)).astype(BF16)
            kd_s[d, rows, :] = (k * jnp.exp(b_tot - b)).astype(BF16)
            dec_s[d, rows, :] = jnp.exp(b_tot)
        return carry
    lax.fori_loop(0, n // blk, gates, 0)

    nh = HG_HEADS
    hs = [slice(h * HG_DK, (h + 1) * HG_DK) for h in range(nh)]
    s8 = lax.broadcasted_iota(jnp.int32, (SUBLANE, nh * HG_DK), 0)
    ones = jnp.ones((HG_DK, LANE), BF16)
    sel_r = lax.broadcasted_iota(jnp.int32, (nh * c, nh * c * c), 0)
    sel_c = lax.broadcasted_iota(jnp.int32, (nh * c, nh * c * c), 1)
    sel = ((sel_c >= sel_r * c) & (sel_c < sel_r * c + c)).astype(BF16)
    zero8 = jnp.zeros((SUBLANE, nh * HG_DK), F32)

    def scores_of(d, rows):
        q = q_s[rows, :]
        k = k_s[d, rows, :]
        b = b_s[d, rows, :]
        halves = [(b[:SUBLANE], k[:SUBLANE]), (b[SUBLANE:], k[SUBLANE:])]
        parts = []
        for t in range(c):
            bt, qt = b[t:t + 1, :], q[t:t + 1, :]
            own = t // SUBLANE
            row = []
            for half, (bh, kh) in enumerate(halves):
                if half == own:
                    tt = t - own * SUBLANE
                    vis = (s8 <= tt) if d == 0 else (s8 >= tt)
                    row.append((qt * jnp.exp(jnp.where(vis, bt - bh, NEG))) * kh)
                elif (half < own) == (d == 0):
                    row.append((qt * jnp.exp(bt - bh)) * kh)
                else:
                    row.append(zero8)
            parts.append(row)
        p = jnp.concatenate([parts[t][half][:, hs[h]] for h in range(nh) for t in range(c) for half in range(2)],
                            axis=0).astype(BF16)
        return _dot(p, ones)

    def state_step(d, h, rows, ci):
        st = st_s[d, h]
        o_inter = _dot_nt(qb_s[d, rows, hs[h]], st.astype(BF16))
        dec8 = dec_s[d, pl.ds(pl.multiple_of(ci * c, c), SUBLANE), hs[h]]
        st_dec = (st.reshape(HG_DK // SUBLANE, SUBLANE, HG_DK) * dec8[None]).reshape(HG_DK, HG_DK)
        st_s[d, h] = st_dec + _dot_tn(zi_ref[rows, hs[h]].astype(BF16), kd_s[d, rows, hs[h]])
        return o_inter

    def body(i, carry):
        cis = (i, n_chunks - 1 - i)
        rows = [_rows(ci, c) for ci in cis]
        scores = [scores_of(d, rows[d]) for d in range(2)]
        o_inter = [[state_step(d, h, rows[d], cis[d]) for h in range(nh)] for d in range(2)]
        for d in range(2):
            v = zi_ref[rows[d], :]
            v_rep = jnp.concatenate([v[:, hs[h]] for h in range(nh) for _ in range(c)], axis=0)
            o_intra = _dot(sel, (scores[d] * v_rep).astype(BF16))
            for h in range(nh):
                (of_s if d == 0 else ob_s)[rows[d], hs[h]] = o_intra[h * c:(h + 1) * c, :] + o_inter[d][h]
        return carry
    lax.fori_loop(0, n_chunks, body, 0)

    sfin_ref[...] = st_s[...]

    def fin(r, carry):
        rows = _rows(r, EW_CHUNK)
        for h in range(HG_HEADS):
            cols = slice(h * HG_DK, (h + 1) * HG_DK)
            o = of_s[rows, cols] + ob_s[rows, cols]
            o = o * lax.rsqrt(jnp.mean(o * o, axis=-1, keepdims=True) + EPS) * ng_ref[h:h + 1, :]
            o_ref[rows, cols] = (o * _silu(zo_ref[rows, cols])).astype(o_ref.dtype)
        return carry
    lax.fori_loop(0, n // EW_CHUNK, fin, 0)


def _hgrn_call(z, lb_logits, hgrn_norm_g, s0_t, *, layer, n_seq, n_tok):
    w = HG_HEADS * HG_DK
    depth = lb_logits.shape[0]
    zspec = lambda k: pl.BlockSpec((n_tok, w), lambda s: (s, k))
    in_specs = [zspec(0), zspec(1), zspec(2), zspec(3), zspec(4),
                pl.BlockSpec((depth, 2, w), lambda s: (0, 0, 0)),
                pl.BlockSpec((None, HG_HEADS, HG_DK), lambda s: (layer, 0, 0))]
    args = [z, z, z, z, z, lb_logits, hgrn_norm_g]
    st_spec = pl.BlockSpec((None, 2, HG_HEADS, HG_DK, HG_DK), lambda s: (s, 0, 0, 0, 0))
    if s0_t is not None:
        in_specs.append(st_spec)
        args.append(s0_t)
    return pl.pallas_call(
        functools.partial(_hgrn_kernel, layer=layer, has_s0=s0_t is not None),
        grid=(n_seq,),
        in_specs=in_specs,
        out_specs=[pl.BlockSpec((n_tok, w), lambda s: (s, 0)), st_spec],
        out_shape=[jax.ShapeDtypeStruct((n_seq * n_tok, w), BF16),
                   jax.ShapeDtypeStruct((n_seq, 2, HG_HEADS, HG_DK, HG_DK), F32)],
        scratch_shapes=[pltpu.VMEM((n_tok, w), F32),
                        pltpu.VMEM((2, n_tok, w), F32),
                        pltpu.VMEM((2, n_tok, w), F32),
                        pltpu.VMEM((2, n_tok, w), BF16),
                        pltpu.VMEM((2, n_tok, w), BF16),
                        pltpu.VMEM((2, n_tok, w), F32),
                        pltpu.VMEM((n_tok, w), F32), pltpu.VMEM((n_tok, w), F32),
                        pltpu.VMEM((2, HG_HEADS, HG_DK, HG_DK), F32)],
        compiler_params=_params(("parallel",)),
    )(*args)


def _ctx_attn_kernel(q0_ref, q1_ref, k0_ref, k1_ref, v0_ref, v1_ref, o_ref, ko_ref, vo_ref, *, scale):
    q_refs, k_refs, v_refs = (q0_ref, q1_ref), (k0_ref, k1_ref), (v0_ref, v1_ref)
    half = q0_ref.shape[1]
    heads_per_block = half // NA_HEAD_DIM
    ko_ref[:, :half] = k0_ref[...]
    ko_ref[:, half:] = k1_ref[...]
    vo_ref[:, :half] = v0_ref[...]
    vo_ref[:, half:] = v1_ref[...]

    def head(refs, h):
        blk, off = divmod(h, heads_per_block)
        return refs[blk][:, off * NA_HEAD_DIM:(off + 1) * NA_HEAD_DIM].astype(BF16)

    scores = [_dot_nt(head(q_refs, h), head(k_refs, h)) * scale for h in range(NA_HEADS)]
    probs = []
    for s in scores:
        p = jnp.exp(s - jnp.max(s, axis=-1, keepdims=True))
        probs.append((p / jnp.sum(p, axis=-1, keepdims=True)).astype(BF16))
    for h in range(NA_HEADS):
        o = _dot(probs[h], head(v_refs, h))
        o_ref[:, h * NA_HEAD_DIM:(h + 1) * NA_HEAD_DIM] = o.astype(o_ref.dtype)


def _ctx_attn_call(z, *, n_seq, n_tok, col0):
    w = NA_HEADS * NA_HEAD_DIM
    half = w // 2
    zspec = lambda k: pl.BlockSpec((n_tok, half), lambda s: (s, col0 // half + k))
    return pl.pallas_call(
        functools.partial(_ctx_attn_kernel, scale=NA_HEAD_DIM ** -0.5),
        grid=(n_seq,),
        in_specs=[zspec(k) for k in range(6)],
        out_specs=[pl.BlockSpec((n_tok, w), lambda s: (s, 0))] * 3,
        out_shape=[jax.ShapeDtypeStruct((n_seq * n_tok, w), BF16),
                   jax.ShapeDtypeStruct((n_seq * n_tok, w), F32),
                   jax.ShapeDtypeStruct((n_seq * n_tok, w), F32)],
        compiler_params=_params(("parallel",)),
    )(z, z, z, z, z, z)


def _na_kernel(q_ref, k_ref, v_ref, kc_ref, vc_ref, bias_ref, o_ref, *, scale, rows, kh):
    q = q_ref[...].astype(BF16)
    k = k_ref[...].astype(BF16)
    v = v_ref[...].astype(BF16)
    qrow = lambda r: slice(r * GRID_W, (r + 1) * GRID_W)
    krows = lambda r: slice(min(max(r - kh // 2, 0), rows - kh) * GRID_W,
                            (min(max(r - kh // 2, 0), rows - kh) + kh) * GRID_W)
    s_ctx = _dot_nt(q, kc_ref[...].astype(BF16)) * scale
    s_loc = [_dot_nt(q[qrow(r)], k[krows(r)]) * scale + bias_ref[r] for r in range(rows)]
    p_loc, p_ctx = [], []
    for r in range(rows):
        sc = s_ctx[qrow(r)]
        m = jnp.maximum(jnp.max(s_loc[r], axis=-1, keepdims=True), jnp.max(sc, axis=-1, keepdims=True))
        el = jnp.exp(s_loc[r] - m)
        ec = jnp.exp(sc - m)
        den = jnp.sum(el, axis=-1, keepdims=True) + jnp.sum(ec, axis=-1, keepdims=True)
        p_loc.append((el / den).astype(BF16))
        p_ctx.append((ec / den).astype(BF16))
    o_ctx = _dot(jnp.concatenate(p_ctx, axis=0), vc_ref[...].astype(BF16))
    for r in range(rows):
        o = _dot(p_loc[r], v[krows(r)]) + o_ctx[qrow(r)]
        o_ref[qrow(r), :] = o.astype(o_ref.dtype)


def _na_bias(rpb, rows, kh):
    heads = rpb.shape[0]
    qcol = jnp.arange(GRID_W)
    kcol = jnp.arange(GRID_W)
    win_start = jnp.clip(qcol - NA_KW // 2, 0, GRID_W - NA_KW)
    in_win = (kcol[None, :] >= win_start[:, None]) & (kcol[None, :] < win_start[:, None] + NA_KW)
    lo = GRID_W - NA_KW
    pad = jnp.pad(rpb.astype(F32), ((0, 0), (0, 0), (lo, lo)))
    toep = jnp.stack([pad[:, :, GRID_W - 1 - q:2 * GRID_W - 1 - q] for q in range(GRID_W)], axis=2)
    toep = jnp.where(in_win[None, None], toep, NEG)
    per_row = []
    for r in range(rows):
        dy0 = min(max(r - kh // 2, 0), rows - kh) - r + NA_KH - 1
        per_row.append(toep[:, dy0:dy0 + kh])
    bias = jnp.stack(per_row, axis=1)
    return jnp.transpose(bias, (0, 1, 3, 2, 4)).reshape(heads, rows, GRID_W, kh * GRID_W)


def _na_call(z, cache_k, cache_v, bias, *, layer, n_seq, n_tok, col_block0):
    rows = n_tok // GRID_W
    kh = min(NA_KH, rows)
    dh = NA_HEAD_DIM
    past = cache_k.shape[2]
    zspec = lambda k: pl.BlockSpec((n_tok, dh), lambda b, h: (b, col_block0 + k * NA_HEADS + h))
    cspec = pl.BlockSpec((None, None, past, dh), lambda b, h: (b, layer, 0, h))
    return pl.pallas_call(
        functools.partial(_na_kernel, scale=dh ** -0.5, rows=rows, kh=kh),
        grid=(n_seq, NA_HEADS),
        in_specs=[zspec(0), zspec(1), zspec(2), cspec, cspec,
                  pl.BlockSpec((None, rows, GRID_W, kh * GRID_W), lambda b, h: (h, 0, 0, 0))],
        out_specs=pl.BlockSpec((n_tok, dh), lambda b, h: (b, h)),
        out_shape=jax.ShapeDtypeStruct((n_seq * n_tok, NA_HEADS * dh), BF16),
        compiler_params=_params(("parallel", "parallel")),
    )(z, z, z, cache_k, cache_v, bias)


def _lru_kernel(zx_ref, zg_ref, cw_ref, cb_ref, wa_ref, ba_ref, wx_ref, bx_ref, lam_ref, *rest,
                n_seq, n_tok, has_h0):
    if has_h0:
        h0_ref, y_ref, hfin_ref, a_s, u_s, y_s = rest
    else:
        y_ref, hfin_ref, a_s, u_s, y_s = rest
    tpos = lax.broadcasted_iota(jnp.int32, (n_tok, LANE), 0)
    gpos = tpos & (SUBLANE - 1)
    neg_lam = -lam_ref[...]
    softplus = jnp.maximum(neg_lam, 0.0) + jnp.log1p(jnp.exp(-jnp.abs(neg_lam)))
    left = LRU_CONV // 2

    def gates(g, carry):
        rows = _rows(g, n_tok)
        zx = zx_ref[rows, :]
        x = cb_ref[...] + cw_ref[left:left + 1, :] * zx
        for j in range(LRU_CONV):
            off = j - left
            if off == 0:
                continue
            shifted = pltpu.roll(zx, (-off) % n_tok, 0)
            valid = (tpos + off >= 0) & (tpos + off < n_tok)
            x = x + cw_ref[j:j + 1, :] * jnp.where(valid, shifted, 0.0)
        xb = x.astype(BF16)
        for d in range(2):
            r_gate = jax.nn.sigmoid(_dot(xb, wa_ref[d].astype(BF16)) + ba_ref[d:d + 1, :])
            i_gate = jax.nn.sigmoid(_dot(xb, wx_ref[d].astype(BF16)) + bx_ref[d:d + 1, :])
            log_a = (-LRU_C * r_gate) * softplus[d:d + 1, :]
            a = jnp.exp(log_a)
            u = jnp.sqrt(-_expm1(2.0 * log_a)) * (i_gate * x)
            sh = 1
            while sh < SUBLANE:
                if d == 0:
                    inside = gpos >= sh
                    amt = sh
                else:
                    inside = gpos < SUBLANE - sh
                    amt = n_tok - sh
                a_prev = jnp.where(inside, pltpu.roll(a, amt, 0), 1.0)
                u_prev = jnp.where(inside, pltpu.roll(u, amt, 0), 0.0)
                u = u + a * u_prev
                a = a * a_prev
                sh *= 2
            a_s[d, rows, :] = a
            u_s[d, rows, :] = u
        return carry
    lax.fori_loop(0, n_seq, gates, 0)

    n_groups = n_tok // SUBLANE
    edge = (SUBLANE - 1, 0)

    def bcast(row):
        return jnp.broadcast_to(row, (SUBLANE, LANE))

    if has_h0:
        h_init = tuple(bcast(h0_ref[d, g:g + 1, :]) for g in range(n_seq) for d in range(2))
    else:
        h_init = tuple(jnp.zeros((SUBLANE, LANE), F32) for _ in range(2 * n_seq))

    def step(j, carry):
        out = []
        for g in range(n_seq):
            for d in range(2):
                grp = j if d == 0 else n_groups - 1 - j
                rows = pl.ds(pl.multiple_of(g * n_tok + grp * SUBLANE, SUBLANE), SUBLANE)
                h = u_s[d, rows, :] + a_s[d, rows, :] * carry[2 * g + d]
                y_s[d, rows, :] = h
                out.append(bcast(h[edge[d]:edge[d] + 1, :]))
        return tuple(out)
    h_last = lax.fori_loop(0, n_groups, step, h_init)
    for g in range(n_seq):
        for d in range(2):
            hfin_ref[d, g:g + 1, :] = h_last[2 * g + d][0:1, :]

    def fin(g, carry):
        rows = _rows(g, n_tok)
        y = (y_s[0, rows, :] + y_s[1, rows, :]) * jax.nn.gelu(zg_ref[rows, :], approximate=True)
        y_ref[rows, :] = y.astype(y_ref.dtype)
        return carry
    lax.fori_loop(0, n_seq, fin, 0)


def _block_diag_pairs(w):
    depth, nd, nb, bw, _ = w.shape
    w = w.reshape(depth, nd, nb // 2, 2, bw, bw)
    eye = jnp.eye(2, dtype=w.dtype)
    out = w[:, :, :, :, :, None, :] * eye[None, None, None, :, None, :, None]
    return out.reshape(depth, nd, nb // 2, 2 * bw, 2 * bw)


def _lru_call(z, conv_w, conv_b, wa_bd, b_a, wx_bd, b_x, lam, h0_t, *, layer, n_seq, n_tok, col_block0):
    w = conv_b.shape[1]
    n_cb = w // LANE
    rows = n_seq * n_tok
    zspec = lambda k: pl.BlockSpec((rows, LANE), lambda cb: (0, col_block0 + k * n_cb + cb))
    vec2 = pl.BlockSpec((None, 2, LANE), lambda cb: (layer, 0, cb))
    wspec = pl.BlockSpec((None, 2, None, LANE, LANE), lambda cb: (layer, 0, cb, 0, 0))
    hspec = pl.BlockSpec((2, n_seq, LANE), lambda cb: (0, 0, cb))
    in_specs = [zspec(0), zspec(1),
                pl.BlockSpec((None, LRU_CONV, LANE), lambda cb: (layer, 0, cb)),
                pl.BlockSpec((None, 1, LANE), lambda cb: (layer, 0, cb)),
                wspec, vec2, wspec, vec2, vec2]
    args = [z, z, conv_w, conv_b.reshape(conv_b.shape[0], 1, w), wa_bd, b_a, wx_bd, b_x, lam]
    if h0_t is not None:
        in_specs.append(hspec)
        args.append(h0_t)
    return pl.pallas_call(
        functools.partial(_lru_kernel, n_seq=n_seq, n_tok=n_tok, has_h0=h0_t is not None),
        grid=(n_cb,),
        in_specs=in_specs,
        out_specs=[pl.BlockSpec((rows, LANE), lambda cb: (0, cb)), hspec],
        out_shape=[jax.ShapeDtypeStruct((rows, w), BF16), jax.ShapeDtypeStruct((2, n_seq, w), F32)],
        scratch_shapes=[pltpu.VMEM((2, rows, LANE), F32)] * 3,
        compiler_params=_params(("parallel",)),
    )(*args)


def _merge_kernel(x_ref, mods_ref, oa_ref, ob_ref, oc_ref, ga_ref, gb_ref, gc_ref,
                  wa_ref, wb_ref, wc_ref, wo_ref, o_ref, wa_s, wb_s, wc_s, wo_s):
    j = pl.program_id(1)
    tm = x_ref.shape[0]
    gate = mods_ref[5:6, :]

    @pl.when(j == 0)
    def _():
        def body(r, carry):
            o_ref[_rows(r, EW_CHUNK), :] = jnp.zeros((EW_CHUNK, o_ref.shape[1]), F32)
            return carry
        lax.fori_loop(0, tm // EW_CHUNK, body, 0)

    wa_s[...] = wa_ref[...].astype(BF16)
    wb_s[...] = wb_ref[...].astype(BF16)
    wc_s[...] = wc_ref[...].astype(BF16)
    wo_s[...] = wo_ref[...].astype(BF16)

    for r in range(tm // ROW_CHUNK):
        sl = slice(r * ROW_CHUNK, (r + 1) * ROW_CHUNK)

        def branch(o_r, w_s, g_r):
            return jax.nn.sigmoid(g_r[sl, :]) * _dot(o_r[sl, :], w_s[...])

        m = branch(oa_ref, wa_s, ga_ref) + branch(ob_ref, wb_s, gb_ref) + branch(oc_ref, wc_s, gc_ref)
        o_ref[sl, :] += _dot(m.astype(BF16), wo_s[...])

    @pl.when(j == pl.num_programs(1) - 1)
    def _():
        def body(r, carry):
            sl = _rows(r, EW_CHUNK)
            o_ref[sl, :] = x_ref[sl, :] + gate * o_ref[sl, :]
            return carry
        lax.fori_loop(0, tm // EW_CHUNK, body, 0)


def _merge_call(x, mods, o_a, o_b, o_c, z, w_proj_a, w_proj_b, w_proj_c, w_out, *, layer, group0,
                rows_per_group, gate_col0):
    t, d = x.shape
    tm, tn = 1024, 256
    nj = d // tn
    g0 = gate_col0 // tn
    group = _group_index(tm, group0, rows_per_group)
    gspec = lambda k: pl.BlockSpec((tm, tn), lambda i, j: (i, g0 + k * nj + j))
    ospec = lambda w: pl.BlockSpec((tm, w), lambda i, j: (i, 0))
    wspec = lambda w: pl.BlockSpec((None, w, tn), lambda i, j: (layer, 0, j))
    return pl.pallas_call(
        _merge_kernel,
        grid=(t // tm, nj),
        in_specs=[
            pl.BlockSpec((tm, d), lambda i, j: (i, 0), pipeline_mode=pl.Buffered(1)),
            pl.BlockSpec((None, None, N_MOD, d), lambda i, j: (layer, group(i), 0, 0)),
            ospec(o_a.shape[1]), ospec(o_b.shape[1]), ospec(o_c.shape[1]),
            gspec(0), gspec(1), gspec(2),
            wspec(o_a.shape[1]), wspec(o_b.shape[1]), wspec(o_c.shape[1]),
            pl.BlockSpec((None, tn, d), lambda i, j: (layer, j, 0)),
        ],
        out_specs=pl.BlockSpec((tm, d), lambda i, j: (i, 0)),
        out_shape=jax.ShapeDtypeStruct((t, d), F32),
        scratch_shapes=[pltpu.VMEM((o_a.shape[1], tn), BF16), pltpu.VMEM((o_b.shape[1], tn), BF16),
                        pltpu.VMEM((o_c.shape[1], tn), BF16), pltpu.VMEM((tn, d), BF16)],
        compiler_params=_params(("parallel", "arbitrary")),
    )(x, mods, o_a, o_b, o_c, z, z, z, w_proj_a, w_proj_b, w_proj_c, w_out)


def kernel(x_prompt, x_sample, cache_na_k, cache_na_v, state_hgrn, state_lru, c, c_ctx, mod_w, mod_b, norm_g, ffn1_w_up, ffn1_w_down, ffn2_w_up, ffn2_w_down, w_in, hgrn_lb_logits, hgrn_norm_g, na_rpb, lru_conv_w, lru_conv_b, lru_w_a, lru_b_a, lru_w_x, lru_b_x, lru_lambda, w_proj_a, w_proj_b, w_proj_c, w_out, final_norm_g):
    b_ctx, n_ctx, d = x_prompt.shape
    b_lat, n_lat, _ = x_sample.shape
    depth = mod_w.shape[0]
    t_ctx, t_lat = b_ctx * n_ctx, b_lat * n_lat
    hg_w = HG_HEADS * HG_DK
    na_w = NA_HEADS * NA_HEAD_DIM
    lru_w = lru_conv_b.shape[1]
    na_col0 = 5 * hg_w
    lru_col0 = na_col0 + 3 * na_w
    gate_col0 = lru_col0 + 2 * lru_w

    ctx = dict(group0=0, rows_per_group=t_ctx)
    lat = dict(group0=1, rows_per_group=n_lat)
    xc = x_prompt.reshape(t_ctx, d)
    xl = x_sample.reshape(t_lat, d)
    cond = jnp.concatenate([c_ctx[None], c, jnp.zeros((N_COND_ROWS - 1 - b_lat, d), F32)], axis=0)
    mods = _mods_call(cond, mod_w, mod_b).reshape(depth, N_COND_ROWS, N_MOD, d)

    rows = n_lat // GRID_W
    kh = min(NA_KH, rows)
    wa_bd = _block_diag_pairs(lru_w_a)
    wx_bd = _block_diag_pairs(lru_w_x)
    cache_k = cache_na_k.reshape(b_lat, depth, cache_na_k.shape[2], na_w)
    cache_v = cache_na_v.reshape(b_lat, depth, cache_na_v.shape[2], na_w)
    s0_t = jnp.swapaxes(state_hgrn, -1, -2)
    h0_t = jnp.transpose(state_lru, (1, 2, 0, 3))

    ks, vs, hgs, lrus = [], [], [], []
    for l in range(depth):
        last = l == depth - 1
        final_g = final_norm_g if last else None
        lru_args = (lru_conv_w, lru_conv_b, wa_bd, lru_b_a, wx_bd, lru_b_x, lru_lambda)
        merge_w = (w_proj_a, w_proj_b, w_proj_c, w_out)

        xc = _ffn_call(xc, mods, norm_g, ffn1_w_up, ffn1_w_down, layer=l, slot=0, **ctx)
        zc = _inproj_call(xc, mods, norm_g, w_in, layer=l, **ctx)
        oa, s_c = _hgrn_call(zc, hgrn_lb_logits, hgrn_norm_g, None, layer=l, n_seq=b_ctx, n_tok=n_ctx)
        ob, k_l, v_l = _ctx_attn_call(zc, n_seq=b_ctx, n_tok=n_ctx, col0=na_col0)
        oc, h_c = _lru_call(zc, *lru_args, None, layer=l, n_seq=b_ctx, n_tok=n_ctx, col_block0=lru_col0 // LANE)
        xc = _merge_call(xc, mods, oa, ob, oc, zc, *merge_w, layer=l, gate_col0=gate_col0, **ctx)
        xc = _ffn_call(xc, mods, norm_g, ffn2_w_up, ffn2_w_down, layer=l, slot=2, final_g=final_g, **ctx)

        xl = _ffn_call(xl, mods, norm_g, ffn1_w_up, ffn1_w_down, layer=l, slot=0, **lat)
        zl = _inproj_call(xl, mods, norm_g, w_in, layer=l, **lat)
        oa, _ = _hgrn_call(zl, hgrn_lb_logits, hgrn_norm_g, s0_t[:, l], layer=l, n_seq=b_lat, n_tok=n_lat)
        bias = _na_bias(na_rpb[l], rows, kh)
        ob = _na_call(zl, cache_k, cache_v, bias, layer=l, n_seq=b_lat, n_tok=n_lat,
                      col_block0=na_col0 // NA_HEAD_DIM)
        oc, _ = _lru_call(zl, *lru_args, h0_t[l], layer=l, n_seq=b_lat, n_tok=n_lat, col_block0=lru_col0 // LANE)
        xl = _merge_call(xl, mods, oa, ob, oc, zl, *merge_w, layer=l, gate_col0=gate_col0, **lat)
        xl = _ffn_call(xl, mods, norm_g, ffn2_w_up, ffn2_w_down, layer=l, slot=2, final_g=final_g, **lat)

        ks.append(k_l.reshape(b_ctx, n_ctx, NA_HEADS, NA_HEAD_DIM))
        vs.append(v_l.reshape(b_ctx, n_ctx, NA_HEADS, NA_HEAD_DIM))
        hgs.append(jnp.swapaxes(s_c, -1, -2))
        lrus.append(jnp.transpose(h_c, (1, 0, 2)))

    return (xc.reshape(b_ctx, n_ctx, d), xl.reshape(b_lat, n_lat, d), jnp.stack(ks, axis=1),
            jnp.stack(vs, axis=1), jnp.stack(hgs, axis=1), jnp.stack(lrus, axis=1))
```

```python
import functools

import jax
import jax.numpy as jnp
from jax import lax
from jax.experimental import pallas as pl
from jax.experimental.pallas import tpu as pltpu

F32 = jnp.float32
BF16 = jnp.bfloat16

EPS = 1e-6
NEG = -1e30
LOG2E = 1.4426950408889634
N_MOD = 9
N_COND_ROWS = 8
HG_HEADS = 4
HG_DK = 128
HG_CHUNK = 16
HG_BLOCK = 128
HG_CHUNKS_PER_TRIP = 1
NA_HEADS = 8
NA_HEAD_DIM = 128
NA_KH = 8
NA_KW = 16
NA_QBW = 8
NA_BAND = NA_QBW + NA_KW
GRID_W = 64
LRU_BLOCKS = 8
LRU_CONV = 4
LRU_C = 8.0
LANE = 128
SUBLANE = 8
VMEM_LIMIT = 56 * 1024 * 1024
FFN_VMEM_LIMIT = 60 * 1024 * 1024


def _params(sem, vmem=VMEM_LIMIT):
    return pltpu.CompilerParams(dimension_semantics=sem, vmem_limit_bytes=vmem)


def _silu(x):
    return x * jax.nn.sigmoid(x)


def _expm1(x):
    u = jnp.exp(x)
    um1 = u - 1.0
    y = um1 * x / jnp.where(u == 1.0, 1.0, jnp.log(u))
    return jnp.where(u == 1.0, x, jnp.where(um1 == -1.0, -1.0, y))


def _dot(a, b):
    return jnp.dot(a, b, preferred_element_type=F32)


def _dot_nt(a, b):
    return lax.dot_general(a, b, (((1,), (1,)), ((), ())), preferred_element_type=F32)


def _dot_tn(a, b):
    return lax.dot_general(a, b, (((0,), (0,)), ((), ())), preferred_element_type=F32)


def _mod_norm(x, g, shift, scale):
    y = x * lax.rsqrt(jnp.mean(x * x, axis=-1, keepdims=True) + EPS)
    return (y * g) * (1.0 + scale) + shift


def _group_index(tm, group0, rows_per_group):
    tiles_per_group = rows_per_group // tm
    return lambda i: group0 + i // tiles_per_group


def _mods_kernel(c_ref, w_ref, b_ref, o_ref):
    s = _silu(c_ref[...]).astype(BF16)
    o_ref[...] = _dot(s, w_ref[...].astype(BF16)) + b_ref[...]


def _mods_call(cond, mod_w, mod_b):
    depth, d, n = mod_w.shape
    tn = 1024
    return pl.pallas_call(
        _mods_kernel,
        grid=(depth, n // tn),
        in_specs=[
            pl.BlockSpec((N_COND_ROWS, d), lambda l, j: (0, 0)),
            pl.BlockSpec((None, d, tn), lambda l, j: (l, 0, j)),
            pl.BlockSpec((None, 1, tn), lambda l, j: (l, 0, j)),
        ],
        out_specs=pl.BlockSpec((None, N_COND_ROWS, tn), lambda l, j: (l, 0, j)),
        out_shape=jax.ShapeDtypeStruct((depth, N_COND_ROWS, n), F32),
        compiler_params=_params(("parallel", "parallel")),
    )(cond, mod_w, mod_b.reshape(depth, 1, n))


ROW_CHUNK = 256
EW_CHUNK = 64


def _rows(r, n):
    return pl.ds(pl.multiple_of(r * n, n), n)


def _ffn_kernel(x_ref, mods_ref, g_ref, wa_ref, wu_ref, wd_ref, *rest, slot, final):
    if final:
        fg_ref, o_ref, h_ref = rest
    else:
        o_ref, h_ref = rest
    j = pl.program_id(1)
    tm = x_ref.shape[0]
    shift = mods_ref[3 * slot:3 * slot + 1, :]
    scale = mods_ref[3 * slot + 1:3 * slot + 2, :]
    gate = mods_ref[3 * slot + 2:3 * slot + 3, :]

    @pl.when(j == 0)
    def _():
        def body(r, carry):
            sl = _rows(r, EW_CHUNK)
            h_ref[sl, :] = _mod_norm(x_ref[sl, :], g_ref[...], shift, scale).astype(BF16)
            o_ref[sl, :] = jnp.zeros((EW_CHUNK, o_ref.shape[1]), F32)
            return carry
        lax.fori_loop(0, tm // EW_CHUNK, body, 0)

    for r in range(tm // ROW_CHUNK):
        sl = slice(r * ROW_CHUNK, (r + 1) * ROW_CHUNK)
        h = h_ref[sl, :]
        a = _dot(h, wa_ref[...].astype(BF16))
        u = _dot(h, wu_ref[...].astype(BF16))
        act = (_silu(a) * u).astype(BF16)
        o_ref[sl, :] += _dot(act, wd_ref[...].astype(BF16))

    @pl.when(j == pl.num_programs(1) - 1)
    def _():
        def body(r, carry):
            sl = _rows(r, EW_CHUNK)
            y = x_ref[sl, :] + (0.5 * gate) * o_ref[sl, :]
            if final:
                y = y * lax.rsqrt(jnp.mean(y * y, axis=-1, keepdims=True) + EPS) * fg_ref[...]
            o_ref[sl, :] = y
            return carry
        lax.fori_loop(0, tm // EW_CHUNK, body, 0)


def _ffn_call(x, mods, norm_g, w_up, w_down, *, layer, slot, group0, rows_per_group, final_g=None):
    t, d = x.shape
    f = w_down.shape[1]
    tm, tf = 1024, 512
    nf = f // tf
    group = _group_index(tm, group0, rows_per_group)
    norm_slot = slot
    in_specs = [
        pl.BlockSpec((tm, d), lambda i, j: (i, 0), pipeline_mode=pl.Buffered(1)),
        pl.BlockSpec((None, None, N_MOD, d), lambda i, j: (layer, group(i), 0, 0)),
        pl.BlockSpec((None, None, 1, d), lambda i, j: (layer, norm_slot, 0, 0)),
        pl.BlockSpec((None, d, tf), lambda i, j: (layer, 0, j)),
        pl.BlockSpec((None, d, tf), lambda i, j: (layer, 0, nf + j)),
        pl.BlockSpec((None, tf, d), lambda i, j: (layer, j, 0)),
    ]
    args = [x, mods, norm_g.reshape(norm_g.shape[0], norm_g.shape[1], 1, d), w_up, w_up, w_down]
    if final_g is not None:
        in_specs.append(pl.BlockSpec((1, d), lambda i, j: (0, 0)))
        args.append(final_g.reshape(1, d))
    return pl.pallas_call(
        functools.partial(_ffn_kernel, slot=slot, final=final_g is not None),
        grid=(t // tm, nf),
        in_specs=in_specs,
        out_specs=pl.BlockSpec((tm, d), lambda i, j: (i, 0)),
        out_shape=jax.ShapeDtypeStruct((t, d), F32),
        scratch_shapes=[pltpu.VMEM((tm, d), BF16)],
        compiler_params=_params(("parallel", "arbitrary"), vmem=FFN_VMEM_LIMIT),
    )(*args)


def _inproj_kernel(x_ref, mods_ref, g_ref, w_ref, z_ref, h_ref, w_s):
    j = pl.program_id(1)
    tm = x_ref.shape[0]
    shift = mods_ref[3:4, :]
    scale = mods_ref[4:5, :]

    @pl.when(j == 0)
    def _():
        def body(r, carry):
            sl = _rows(r, EW_CHUNK)
            h_ref[sl, :] = _mod_norm(x_ref[sl, :], g_ref[...], shift, scale).astype(BF16)
            return carry
        lax.fori_loop(0, tm // EW_CHUNK, body, 0)

    w_s[...] = w_ref[...].astype(BF16)

    for r in range(tm // ROW_CHUNK):
        sl = slice(r * ROW_CHUNK, (r + 1) * ROW_CHUNK)
        z_ref[sl, :] = _dot(h_ref[sl, :], w_s[...])


def _inproj_call(x, mods, norm_g, w_in, *, layer, group0, rows_per_group):
    t, d = x.shape
    n = w_in.shape[2]
    tm, tn = 1024, 1280
    group = _group_index(tm, group0, rows_per_group)
    return pl.pallas_call(
        _inproj_kernel,
        grid=(t // tm, n // tn),
        in_specs=[
            pl.BlockSpec((tm, d), lambda i, j: (i, 0), pipeline_mode=pl.Buffered(1)),
            pl.BlockSpec((None, None, N_MOD, d), lambda i, j: (layer, group(i), 0, 0)),
            pl.BlockSpec((None, None, 1, d), lambda i, j: (layer, 1, 0, 0)),
            pl.BlockSpec((None, d, tn), lambda i, j: (layer, 0, j)),
        ],
        out_specs=pl.BlockSpec((tm, tn), lambda i, j: (i, j)),
        out_shape=jax.ShapeDtypeStruct((t, n), F32),
        scratch_shapes=[pltpu.VMEM((tm, d), BF16), pltpu.VMEM((d, tn), BF16)],
        compiler_params=_params(("parallel", "arbitrary")),
    )(x, mods, norm_g.reshape(norm_g.shape[0], norm_g.shape[1], 1, d), w_in)


def _log_forget(zf, log_lb, log1m_lb):
    ls = jnp.minimum(zf, 0.0) - jnp.log1p(jnp.exp(-jnp.abs(zf)))
    b = log1m_lb + ls
    hi = jnp.maximum(log_lb, b)
    return hi + jnp.log1p(jnp.exp(-jnp.abs(log_lb - b)))


def _hgrn_kernel(zq_ref, zff_ref, zfb_ref, zi_ref, zo_ref, lg_ref, ng_ref, *rest, layer, has_s0):
    if has_s0:
        s0_ref, o_ref, sfin_ref, q_s, c_s, b_s, qb_s, kd_s, dec_s, of_s, ob_s, st_s = rest
    else:
        o_ref, sfin_ref, q_s, c_s, b_s, qb_s, kd_s, dec_s, of_s, ob_s, st_s = rest
    n = zq_ref.shape[0]
    c = HG_CHUNK
    n_chunks = n // c
    blk = HG_BLOCK
    cpb = blk // c
    sh = c.bit_length() - 1

    logits = lg_ref[...]
    depth = logits.shape[0]
    mx = logits[0]
    for i in range(1, depth):
        mx = jnp.maximum(mx, logits[i])
    ex = [jnp.exp(logits[i] - mx) for i in range(depth)]
    tot = ex[0]
    for i in range(1, depth):
        tot = tot + ex[i]
    lb = jnp.zeros_like(mx)
    for i in range(1, layer + 1):
        lb = lb + ex[i] / tot
    log_lb = jnp.log(lb)
    log1m_lb = jnp.log1p(-lb)

    if has_s0:
        st_s[...] = s0_ref[...]
    else:
        st_s[...] = jnp.zeros(st_s.shape, F32)

    br = lax.broadcasted_iota(jnp.int32, (blk, blk), 0)
    bc = lax.broadcasted_iota(jnp.int32, (blk, blk), 1)
    same = (br >> sh) == (bc >> sh)
    cum_mat = [(same & (bc <= br)).astype(BF16), (same & (bc >= br)).astype(BF16)]
    tot_mat = same.astype(BF16)

    def gates(i, carry):
        rows = _rows(i, blk)
        q = _silu(zq_ref[rows, :])
        q_s[rows, :] = q
        for d in range(2):
            zf = (zff_ref if d == 0 else zfb_ref)[rows, :]
            lf = _log_forget(zf, log_lb[d:d + 1, :], log1m_lb[d:d + 1, :])
            k = -_expm1(lf)
            hi = lf.astype(BF16)
            lo = (lf - hi.astype(F32)).astype(BF16)
            b = _dot(cum_mat[d], hi) + _dot(cum_mat[d], lo)
            b_tot = _dot(tot_mat, hi) + _dot(tot_mat, lo)
            b2 = b * LOG2E
            b_s[d, rows, :] = b2
            c_s[d, rows, :] = b2 - jnp.log(k) * LOG2E
            qb_s[d, rows, :] = (q * jnp.exp(b)).astype(BF16)
            kd_s[d, rows, :] = (k * jnp.exp(b_tot - b)).astype(BF16)
            dec_s[d, rows, :] = jnp.exp(b_tot)
        return carry
    lax.fori_loop(0, n // blk, gates, 0)

    nh = HG_HEADS
    hs = [slice(h * HG_DK, (h + 1) * HG_DK) for h in range(nh)]
    s8 = lax.broadcasted_iota(jnp.int32, (SUBLANE, nh * HG_DK), 0)
    pair = 2 * HG_DK
    pr = lax.broadcasted_iota(jnp.int32, (pair, pair), 0)
    pc = lax.broadcasted_iota(jnp.int32, (pair, pair), 1)
    ones_pair = ((pr >= HG_DK) == (pc >= HG_DK)).astype(BF16)
    sel_r = lax.broadcasted_iota(jnp.int32, (c, c * c), 0)
    sel_c = lax.broadcasted_iota(jnp.int32, (c, c * c), 1)
    sel = ((sel_c >= sel_r * c) & (sel_c < sel_r * c + c)).astype(BF16)
    zero8 = jnp.zeros((SUBLANE, nh * HG_DK), F32)

    def scores_of(d, rows):
        q = q_s[rows, :]
        b2 = b_s[d, rows, :]
        c2 = c_s[d, rows, :]
        halves = [c2[:SUBLANE], c2[SUBLANE:]]
        parts = []
        for t in range(c):
            bt, qt = b2[t:t + 1, :], q[t:t + 1, :]
            own = t // SUBLANE
            row = []
            for half, ch in enumerate(halves):
                if half == own:
                    tt = t - own * SUBLANE
                    vis = (s8 <= tt) if d == 0 else (s8 >= tt)
                    row.append(qt * jnp.exp2(jnp.where(vis, bt - ch, NEG)))
                elif (half < own) == (d == 0):
                    row.append(qt * jnp.exp2(bt - ch))
                else:
                    row.append(zero8)
            parts.append(row)
        p = jnp.concatenate([parts[t][half] for t in range(c) for half in range(2)], axis=0).astype(BF16)
        return jnp.concatenate([_dot(p[:, j * pair:(j + 1) * pair], ones_pair) for j in range(nh * HG_DK // pair)],
                               axis=1)

    def state_step(d, h, rows, ci):
        st = st_s[d, h]
        o_inter = _dot_nt(qb_s[d, rows, hs[h]], st.astype(BF16))
        dec8 = dec_s[d, pl.ds(pl.multiple_of(ci * c, c), SUBLANE), hs[h]]
        st_dec = (st.reshape(HG_DK // SUBLANE, SUBLANE, HG_DK) * dec8[None]).reshape(HG_DK, HG_DK)
        st_s[d, h] = st_dec + _dot_tn(zi_ref[rows, hs[h]].astype(BF16), kd_s[d, rows, hs[h]])
        return o_inter

    per_trip = HG_CHUNKS_PER_TRIP

    def body(i, carry):
        cis = [[per_trip * i + u for u in range(per_trip)],
               [n_chunks - 1 - per_trip * i - u for u in range(per_trip)]]
        rows = [[_rows(ci, c) for ci in cis[d]] for d in range(2)]
        scores = [[scores_of(d, rows[d][u]) for u in range(per_trip)] for d in range(2)]
        o_inter = [[[state_step(d, h, rows[d][u], cis[d][u]) for h in range(nh)] for u in range(per_trip)]
                   for d in range(2)]
        for d in range(2):
            for u in range(per_trip):
                v_rep = jnp.concatenate([zi_ref[rows[d][u], :]] * c, axis=0)
                o_intra = _dot(sel, (scores[d][u] * v_rep).astype(BF16))
                for h in range(nh):
                    (of_s if d == 0 else ob_s)[rows[d][u], hs[h]] = o_intra[:, hs[h]] + o_inter[d][u][h]
        return carry
    lax.fori_loop(0, n_chunks // per_trip, body, 0)

    sfin_ref[...] = st_s[...]

    def fin(r, carry):
        rows = _rows(r, EW_CHUNK)
        for h in range(HG_HEADS):
            cols = slice(h * HG_DK, (h + 1) * HG_DK)
            o = of_s[rows, cols] + ob_s[rows, cols]
            o = o * lax.rsqrt(jnp.mean(o * o, axis=-1, keepdims=True) + EPS) * ng_ref[h:h + 1, :]
            o_ref[rows, cols] = (o * _silu(zo_ref[rows, cols])).astype(o_ref.dtype)
        return carry
    lax.fori_loop(0, n // EW_CHUNK, fin, 0)


def _hgrn_call(z, lb_logits, hgrn_norm_g, s0_t, *, layer, n_seq, n_tok):
    w = HG_HEADS * HG_DK
    depth = lb_logits.shape[0]
    zspec = lambda k: pl.BlockSpec((n_tok, w), lambda s: (s, k))
    in_specs = [zspec(0), zspec(1), zspec(2), zspec(3), zspec(4),
                pl.BlockSpec((depth, 2, w), lambda s: (0, 0, 0)),
                pl.BlockSpec((None, HG_HEADS, HG_DK), lambda s: (layer, 0, 0))]
    args = [z, z, z, z, z, lb_logits, hgrn_norm_g]
    st_spec = pl.BlockSpec((None, 2, HG_HEADS, HG_DK, HG_DK), lambda s: (s, 0, 0, 0, 0))
    if s0_t is not None:
        in_specs.append(st_spec)
        args.append(s0_t)
    return pl.pallas_call(
        functools.partial(_hgrn_kernel, layer=layer, has_s0=s0_t is not None),
        grid=(n_seq,),
        in_specs=in_specs,
        out_specs=[pl.BlockSpec((n_tok, w), lambda s: (s, 0)), st_spec],
        out_shape=[jax.ShapeDtypeStruct((n_seq * n_tok, w), BF16),
                   jax.ShapeDtypeStruct((n_seq, 2, HG_HEADS, HG_DK, HG_DK), F32)],
        scratch_shapes=[pltpu.VMEM((n_tok, w), F32),
                        pltpu.VMEM((2, n_tok, w), F32),
                        pltpu.VMEM((2, n_tok, w), F32),
                        pltpu.VMEM((2, n_tok, w), BF16),
                        pltpu.VMEM((2, n_tok, w), BF16),
                        pltpu.VMEM((2, n_tok, w), F32),
                        pltpu.VMEM((n_tok, w), F32), pltpu.VMEM((n_tok, w), F32),
                        pltpu.VMEM((2, HG_HEADS, HG_DK, HG_DK), F32)],
        compiler_params=_params(("parallel",)),
    )(*args)


def _ctx_attn_kernel(q0_ref, q1_ref, k0_ref, k1_ref, v0_ref, v1_ref, o_ref, ko_ref, vo_ref, *, scale):
    q_refs, k_refs, v_refs = (q0_ref, q1_ref), (k0_ref, k1_ref), (v0_ref, v1_ref)
    half = q0_ref.shape[1]
    heads_per_block = half // NA_HEAD_DIM
    ko_ref[:, :half] = k0_ref[...]
    ko_ref[:, half:] = k1_ref[...]
    vo_ref[:, :half] = v0_ref[...]
    vo_ref[:, half:] = v1_ref[...]

    def head(refs, h):
        blk, off = divmod(h, heads_per_block)
        return refs[blk][:, off * NA_HEAD_DIM:(off + 1) * NA_HEAD_DIM].astype(BF16)

    scores = [_dot_nt(head(q_refs, h), head(k_refs, h)) * scale for h in range(NA_HEADS)]
    probs = []
    for s in scores:
        p = jnp.exp(s - jnp.max(s, axis=-1, keepdims=True))
        probs.append((p / jnp.sum(p, axis=-1, keepdims=True)).astype(BF16))
    for h in range(NA_HEADS):
        o = _dot(probs[h], head(v_refs, h))
        o_ref[:, h * NA_HEAD_DIM:(h + 1) * NA_HEAD_DIM] = o.astype(o_ref.dtype)


def _ctx_attn_call(z, *, n_seq, n_tok, col0):
    w = NA_HEADS * NA_HEAD_DIM
    half = w // 2
    zspec = lambda k: pl.BlockSpec((n_tok, half), lambda s: (s, col0 // half + k))
    return pl.pallas_call(
        functools.partial(_ctx_attn_kernel, scale=NA_HEAD_DIM ** -0.5),
        grid=(n_seq,),
        in_specs=[zspec(k) for k in range(6)],
        out_specs=[pl.BlockSpec((n_tok, w), lambda s: (s, 0))] * 3,
        out_shape=[jax.ShapeDtypeStruct((n_seq * n_tok, w), BF16),
                   jax.ShapeDtypeStruct((n_seq * n_tok, w), F32),
                   jax.ShapeDtypeStruct((n_seq * n_tok, w), F32)],
        compiler_params=_params(("parallel",)),
    )(z, z, z, z, z, z)


def _na_kernel(q_ref, k_ref, v_ref, kc_ref, vc_ref, bias_ref, o_ref, *, scale, rows, kh):
    q = q_ref[...].astype(BF16)
    k = k_ref[...].astype(BF16)
    v = v_ref[...].astype(BF16)
    qrow = lambda r: slice(r * GRID_W, (r + 1) * GRID_W)
    krows = lambda r: slice(min(max(r - kh // 2, 0), rows - kh) * GRID_W,
                            (min(max(r - kh // 2, 0), rows - kh) + kh) * GRID_W)
    s_ctx = _dot_nt(q, kc_ref[...].astype(BF16)) * scale
    s_loc = [_dot_nt(q[qrow(r)], k[krows(r)]) * scale + bias_ref[r] for r in range(rows)]
    p_loc, p_ctx = [], []
    for r in range(rows):
        sc = s_ctx[qrow(r)]
        m = jnp.maximum(jnp.max(s_loc[r], axis=-1, keepdims=True), jnp.max(sc, axis=-1, keepdims=True))
        el = jnp.exp(s_loc[r] - m)
        ec = jnp.exp(sc - m)
        den = jnp.sum(el, axis=-1, keepdims=True) + jnp.sum(ec, axis=-1, keepdims=True)
        p_loc.append((el / den).astype(BF16))
        p_ctx.append((ec / den).astype(BF16))
    o_ctx = _dot(jnp.concatenate(p_ctx, axis=0), vc_ref[...].astype(BF16))
    for r in range(rows):
        o = _dot(p_loc[r], v[krows(r)]) + o_ctx[qrow(r)]
        o_ref[qrow(r), :] = o.astype(o_ref.dtype)


def _na_bias(rpb, rows, kh):
    heads = rpb.shape[0]
    qcol = jnp.arange(GRID_W)
    kcol = jnp.arange(GRID_W)
    win_start = jnp.clip(qcol - NA_KW // 2, 0, GRID_W - NA_KW)
    in_win = (kcol[None, :] >= win_start[:, None]) & (kcol[None, :] < win_start[:, None] + NA_KW)
    lo = GRID_W - NA_KW
    pad = jnp.pad(rpb.astype(F32), ((0, 0), (0, 0), (lo, lo)))
    toep = jnp.stack([pad[:, :, GRID_W - 1 - q:2 * GRID_W - 1 - q] for q in range(GRID_W)], axis=2)
    toep = jnp.where(in_win[None, None], toep, NEG)
    per_row = []
    for r in range(rows):
        dy0 = min(max(r - kh // 2, 0), rows - kh) - r + NA_KH - 1
        per_row.append(toep[:, dy0:dy0 + kh])
    bias = jnp.stack(per_row, axis=1)
    return jnp.transpose(bias, (0, 1, 3, 2, 4)).reshape(heads, rows, GRID_W, kh * GRID_W)


def _na_call(z, cache_k, cache_v, bias, *, layer, n_seq, n_tok, col_block0):
    rows = n_tok // GRID_W
    kh = min(NA_KH, rows)
    dh = NA_HEAD_DIM
    past = cache_k.shape[2]
    zspec = lambda k: pl.BlockSpec((n_tok, dh), lambda b, h: (b, col_block0 + k * NA_HEADS + h))
    cspec = pl.BlockSpec((None, None, past, dh), lambda b, h: (b, layer, 0, h))
    return pl.pallas_call(
        functools.partial(_na_kernel, scale=dh ** -0.5, rows=rows, kh=kh),
        grid=(n_seq, NA_HEADS),
        in_specs=[zspec(0), zspec(1), zspec(2), cspec, cspec,
                  pl.BlockSpec((None, rows, GRID_W, kh * GRID_W), lambda b, h: (h, 0, 0, 0))],
        out_specs=pl.BlockSpec((n_tok, dh), lambda b, h: (b, h)),
        out_shape=jax.ShapeDtypeStruct((n_seq * n_tok, NA_HEADS * dh), BF16),
        compiler_params=_params(("parallel", "parallel")),
    )(z, z, z, cache_k, cache_v, bias)


def _lru_kernel(zx_ref, zg_ref, cw_ref, cb_ref, wa_ref, ba_ref, wx_ref, bx_ref, lam_ref, *rest,
                n_seq, n_tok, has_h0):
    if has_h0:
        h0_ref, y_ref, hfin_ref, a_s, u_s, y_s = rest
    else:
        y_ref, hfin_ref, a_s, u_s, y_s = rest
    tpos = lax.broadcasted_iota(jnp.int32, (n_tok, LANE), 0)
    gpos = tpos & (SUBLANE - 1)
    neg_lam = -lam_ref[...]
    softplus = jnp.maximum(neg_lam, 0.0) + jnp.log1p(jnp.exp(-jnp.abs(neg_lam)))
    left = LRU_CONV // 2

    def gates(g, carry):
        rows = _rows(g, n_tok)
        zx = zx_ref[rows, :]
        x = cb_ref[...] + cw_ref[left:left + 1, :] * zx
        for j in range(LRU_CONV):
            off = j - left
            if off == 0:
                continue
            shifted = pltpu.roll(zx, (-off) % n_tok, 0)
            valid = (tpos + off >= 0) & (tpos + off < n_tok)
            x = x + cw_ref[j:j + 1, :] * jnp.where(valid, shifted, 0.0)
        xb = x.astype(BF16)
        for d in range(2):
            r_gate = jax.nn.sigmoid(_dot(xb, wa_ref[d].astype(BF16)) + ba_ref[d:d + 1, :])
            i_gate = jax.nn.sigmoid(_dot(xb, wx_ref[d].astype(BF16)) + bx_ref[d:d + 1, :])
            log_a = (-LRU_C * r_gate) * softplus[d:d + 1, :]
            a = jnp.exp(log_a)
            u = jnp.sqrt(-_expm1(2.0 * log_a)) * (i_gate * x)
            sh = 1
            while sh < SUBLANE:
                if d == 0:
                    inside = gpos >= sh
                    amt = sh
                else:
                    inside = gpos < SUBLANE - sh
                    amt = n_tok - sh
                a_prev = jnp.where(inside, pltpu.roll(a, amt, 0), 1.0)
                u_prev = jnp.where(inside, pltpu.roll(u, amt, 0), 0.0)
                u = u + a * u_prev
                a = a * a_prev
                sh *= 2
            a_s[d, rows, :] = a
            u_s[d, rows, :] = u
        return carry
    lax.fori_loop(0, n_seq, gates, 0)

    n_groups = n_tok // SUBLANE
    edge = (SUBLANE - 1, 0)

    def bcast(row):
        return jnp.broadcast_to(row, (SUBLANE, LANE))

    if has_h0:
        h_init = tuple(bcast(h0_ref[d, g:g + 1, :]) for g in range(n_seq) for d in range(2))
    else:
        h_init = tuple(jnp.zeros((SUBLANE, LANE), F32) for _ in range(2 * n_seq))

    def step(j, carry):
        out = []
        for g in range(n_seq):
            for d in range(2):
                grp = j if d == 0 else n_groups - 1 - j
                rows = pl.ds(pl.multiple_of(g * n_tok + grp * SUBLANE, SUBLANE), SUBLANE)
                h = u_s[d, rows, :] + a_s[d, rows, :] * carry[2 * g + d]
                y_s[d, rows, :] = h
                out.append(bcast(h[edge[d]:edge[d] + 1, :]))
        return tuple(out)
    h_last = lax.fori_loop(0, n_groups, step, h_init)
    for g in range(n_seq):
        for d in range(2):
            hfin_ref[d, g:g + 1, :] = h_last[2 * g + d][0:1, :]

    def fin(g, carry):
        rows = _rows(g, n_tok)
        y = (y_s[0, rows, :] + y_s[1, rows, :]) * jax.nn.gelu(zg_ref[rows, :], approximate=True)
        y_ref[rows, :] = y.astype(y_ref.dtype)
        return carry
    lax.fori_loop(0, n_seq, fin, 0)


def _block_diag_pairs(w):
    depth, nd, nb, bw, _ = w.shape
    w = w.reshape(depth, nd, nb // 2, 2, bw, bw)
    eye = jnp.eye(2, dtype=w.dtype)
    out = w[:, :, :, :, :, None, :] * eye[None, None, None, :, None, :, None]
    return out.reshape(depth, nd, nb // 2, 2 * bw, 2 * bw)


def _lru_call(z, conv_w, conv_b, wa_bd, b_a, wx_bd, b_x, lam, h0_t, *, layer, n_seq, n_tok, col_block0):
    w = conv_b.shape[1]
    n_cb = w // LANE
    rows = n_seq * n_tok
    zspec = lambda k: pl.BlockSpec((rows, LANE), lambda cb: (0, col_block0 + k * n_cb + cb))
    vec2 = pl.BlockSpec((None, 2, LANE), lambda cb: (layer, 0, cb))
    wspec = pl.BlockSpec((None, 2, None, LANE, LANE), lambda cb: (layer, 0, cb, 0, 0))
    hspec = pl.BlockSpec((2, n_seq, LANE), lambda cb: (0, 0, cb))
    in_specs = [zspec(0), zspec(1),
                pl.BlockSpec((None, LRU_CONV, LANE), lambda cb: (layer, 0, cb)),
                pl.BlockSpec((None, 1, LANE), lambda cb: (layer, 0, cb)),
                wspec, vec2, wspec, vec2, vec2]
    args = [z, z, conv_w, conv_b.reshape(conv_b.shape[0], 1, w), wa_bd, b_a, wx_bd, b_x, lam]
    if h0_t is not None:
        in_specs.append(hspec)
        args.append(h0_t)
    return pl.pallas_call(
        functools.partial(_lru_kernel, n_seq=n_seq, n_tok=n_tok, has_h0=h0_t is not None),
        grid=(n_cb,),
        in_specs=in_specs,
        out_specs=[pl.BlockSpec((rows, LANE), lambda cb: (0, cb)), hspec],
        out_shape=[jax.ShapeDtypeStruct((rows, w), BF16), jax.ShapeDtypeStruct((2, n_seq, w), F32)],
        scratch_shapes=[pltpu.VMEM((2, rows, LANE), F32)] * 3,
        compiler_params=_params(("parallel",)),
    )(*args)


def _merge_kernel(x_ref, mods_ref, oa_ref, ob_ref, oc_ref, ga_ref, gb_ref, gc_ref,
                  wa_ref, wb_ref, wc_ref, wo_ref, o_ref, wa_s, wb_s, wc_s, wo_s):
    j = pl.program_id(1)
    tm = x_ref.shape[0]
    gate = mods_ref[5:6, :]

    @pl.when(j == 0)
    def _():
        def body(r, carry):
            o_ref[_rows(r, EW_CHUNK), :] = jnp.zeros((EW_CHUNK, o_ref.shape[1]), F32)
            return carry
        lax.fori_loop(0, tm // EW_CHUNK, body, 0)

    wa_s[...] = wa_ref[...].astype(BF16)
    wb_s[...] = wb_ref[...].astype(BF16)
    wc_s[...] = wc_ref[...].astype(BF16)
    wo_s[...] = wo_ref[...].astype(BF16)

    for r in range(tm // ROW_CHUNK):
        sl = slice(r * ROW_CHUNK, (r + 1) * ROW_CHUNK)

        def branch(o_r, w_s, g_r):
            return jax.nn.sigmoid(g_r[sl, :]) * _dot(o_r[sl, :], w_s[...])

        m = branch(oa_ref, wa_s, ga_ref) + branch(ob_ref, wb_s, gb_ref) + branch(oc_ref, wc_s, gc_ref)
        o_ref[sl, :] += _dot(m.astype(BF16), wo_s[...])

    @pl.when(j == pl.num_programs(1) - 1)
    def _():
        def body(r, carry):
            sl = _rows(r, EW_CHUNK)
            o_ref[sl, :] = x_ref[sl, :] + gate * o_ref[sl, :]
            return carry
        lax.fori_loop(0, tm // EW_CHUNK, body, 0)


def _merge_call(x, mods, o_a, o_b, o_c, z, w_proj_a, w_proj_b, w_proj_c, w_out, *, layer, group0,
                rows_per_group, gate_col0):
    t, d = x.shape
    tm, tn = 1024, 256
    nj = d // tn
    g0 = gate_col0 // tn
    group = _group_index(tm, group0, rows_per_group)
    gspec = lambda k: pl.BlockSpec((tm, tn), lambda i, j: (i, g0 + k * nj + j))
    ospec = lambda w: pl.BlockSpec((tm, w), lambda i, j: (i, 0))
    wspec = lambda w: pl.BlockSpec((None, w, tn), lambda i, j: (layer, 0, j))
    return pl.pallas_call(
        _merge_kernel,
        grid=(t // tm, nj),
        in_specs=[
            pl.BlockSpec((tm, d), lambda i, j: (i, 0), pipeline_mode=pl.Buffered(1)),
            pl.BlockSpec((None, None, N_MOD, d), lambda i, j: (layer, group(i), 0, 0)),
            ospec(o_a.shape[1]), ospec(o_b.shape[1]), ospec(o_c.shape[1]),
            gspec(0), gspec(1), gspec(2),
            wspec(o_a.shape[1]), wspec(o_b.shape[1]), wspec(o_c.shape[1]),
            pl.BlockSpec((None, tn, d), lambda i, j: (layer, j, 0)),
        ],
        out_specs=pl.BlockSpec((tm, d), lambda i, j: (i, 0)),
        out_shape=jax.ShapeDtypeStruct((t, d), F32),
        scratch_shapes=[pltpu.VMEM((o_a.shape[1], tn), BF16), pltpu.VMEM((o_b.shape[1], tn), BF16),
                        pltpu.VMEM((o_c.shape[1], tn), BF16), pltpu.VMEM((tn, d), BF16)],
        compiler_params=_params(("parallel", "arbitrary")),
    )(x, mods, o_a, o_b, o_c, z, z, z, w_proj_a, w_proj_b, w_proj_c, w_out)


def kernel(x_prompt, x_sample, cache_na_k, cache_na_v, state_hgrn, state_lru, c, c_ctx, mod_w, mod_b, norm_g, ffn1_w_up, ffn1_w_down, ffn2_w_up, ffn2_w_down, w_in, hgrn_lb_logits, hgrn_norm_g, na_rpb, lru_conv_w, lru_conv_b, lru_w_a, lru_b_a, lru_w_x, lru_b_x, lru_lambda, w_proj_a, w_proj_b, w_proj_c, w_out, final_norm_g):
    b_ctx, n_ctx, d = x_prompt.shape
    b_lat, n_lat, _ = x_sample.shape
    depth = mod_w.shape[0]
    t_ctx, t_lat = b_ctx * n_ctx, b_lat * n_lat
    hg_w = HG_HEADS * HG_DK
    na_w = NA_HEADS * NA_HEAD_DIM
    lru_w = lru_conv_b.shape[1]
    na_col0 = 5 * hg_w
    lru_col0 = na_col0 + 3 * na_w
    gate_col0 = lru_col0 + 2 * lru_w

    ctx = dict(group0=0, rows_per_group=t_ctx)
    lat = dict(group0=1, rows_per_group=n_lat)
    xc = x_prompt.reshape(t_ctx, d)
    xl = x_sample.reshape(t_lat, d)
    cond = jnp.concatenate([c_ctx[None], c, jnp.zeros((N_COND_ROWS - 1 - b_lat, d), F32)], axis=0)
    mods = _mods_call(cond, mod_w, mod_b).reshape(depth, N_COND_ROWS, N_MOD, d)

    rows = n_lat // GRID_W
    kh = min(NA_KH, rows)
    wa_bd = _block_diag_pairs(lru_w_a)
    wx_bd = _block_diag_pairs(lru_w_x)
    cache_k = cache_na_k.reshape(b_lat, depth, cache_na_k.shape[2], na_w)
    cache_v = cache_na_v.reshape(b_lat, depth, cache_na_v.shape[2], na_w)
    s0_t = jnp.swapaxes(state_hgrn, -1, -2)
    h0_t = jnp.transpose(state_lru, (1, 2, 0, 3))

    ks, vs, hgs, lrus = [], [], [], []
    for l in range(depth):
        last = l == depth - 1
        final_g = final_norm_g if last else None
        lru_args = (lru_conv_w, lru_conv_b, wa_bd, lru_b_a, wx_bd, lru_b_x, lru_lambda)
        merge_w = (w_proj_a, w_proj_b, w_proj_c, w_out)

        xc = _ffn_call(xc, mods, norm_g, ffn1_w_up, ffn1_w_down, layer=l, slot=0, **ctx)
        zc = _inproj_call(xc, mods, norm_g, w_in, layer=l, **ctx)
        oa, s_c = _hgrn_call(zc, hgrn_lb_logits, hgrn_norm_g, None, layer=l, n_seq=b_ctx, n_tok=n_ctx)
        ob, k_l, v_l = _ctx_attn_call(zc, n_seq=b_ctx, n_tok=n_ctx, col0=na_col0)
        oc, h_c = _lru_call(zc, *lru_args, None, layer=l, n_seq=b_ctx, n_tok=n_ctx, col_block0=lru_col0 // LANE)
        xc = _merge_call(xc, mods, oa, ob, oc, zc, *merge_w, layer=l, gate_col0=gate_col0, **ctx)
        xc = _ffn_call(xc, mods, norm_g, ffn2_w_up, ffn2_w_down, layer=l, slot=2, final_g=final_g, **ctx)

        xl = _ffn_call(xl, mods, norm_g, ffn1_w_up, ffn1_w_down, layer=l, slot=0, **lat)
        zl = _inproj_call(xl, mods, norm_g, w_in, layer=l, **lat)
        oa, _ = _hgrn_call(zl, hgrn_lb_logits, hgrn_norm_g, s0_t[:, l], layer=l, n_seq=b_lat, n_tok=n_lat)
        bias = _na_bias(na_rpb[l], rows, kh)
        ob = _na_call(zl, cache_k, cache_v, bias, layer=l, n_seq=b_lat, n_tok=n_lat,
                      col_block0=na_col0 // NA_HEAD_DIM)
        oc, _ = _lru_call(zl, *lru_args, h0_t[l], layer=l, n_seq=b_lat, n_tok=n_lat, col_block0=lru_col0 // LANE)
        xl = _merge_call(xl, mods, oa, ob, oc, zl, *merge_w, layer=l, gate_col0=gate_col0, **lat)
        xl = _ffn_call(xl, mods, norm_g, ffn2_w_up, ffn2_w_down, layer=l, slot=2, final_g=final_g, **lat)

        ks.append(k_l.reshape(b_ctx, n_ctx, NA_HEADS, NA_HEAD_DIM))
        vs.append(v_l.reshape(b_ctx, n_ctx, NA_HEADS, NA_HEAD_DIM))
        hgs.append(jnp.swapaxes(s_c, -1, -2))
        lrus.append(jnp.transpose(h_c, (1, 0, 2)))

    return (xc.reshape(b_ctx, n_ctx, d), xl.reshape(b_lat, n_lat, d), jnp.stack(ks, axis=1),
            jnp.stack(vs, axis=1), jnp.stack(hgs, axis=1), jnp.stack(lrus, axis=1))
```

```python
import functools

import jax
import jax.numpy as jnp
from jax import lax
from jax.experimental import pallas as pl
from jax.experimental.pallas import tpu as pltpu

F32 = jnp.float32
BF16 = jnp.bfloat16

EPS = 1e-6
NEG = -1e30
LOG2E = 1.4426950408889634
N_MOD = 9
N_COND_ROWS = 8
HG_HEADS = 4
HG_DK = 128
HG_CHUNK = 16
HG_BLOCK = 128
HG_CHUNKS_PER_TRIP = 1
NA_HEADS = 8
NA_HEAD_DIM = 128
NA_KH = 8
NA_KW = 16
NA_QBW = 8
NA_BAND = NA_QBW + NA_KW
GRID_W = 64
LRU_BLOCKS = 8
LRU_CONV = 4
LRU_C = 8.0
LANE = 128
SUBLANE = 8
VMEM_LIMIT = 56 * 1024 * 1024
FFN_VMEM_LIMIT = 60 * 1024 * 1024


def _params(sem, vmem=VMEM_LIMIT):
    return pltpu.CompilerParams(dimension_semantics=sem, vmem_limit_bytes=vmem)


def _silu(x):
    return x * jax.nn.sigmoid(x)


def _expm1(x):
    u = jnp.exp(x)
    um1 = u - 1.0
    y = um1 * x / jnp.where(u == 1.0, 1.0, jnp.log(u))
    return jnp.where(u == 1.0, x, jnp.where(um1 == -1.0, -1.0, y))


def _dot(a, b):
    return jnp.dot(a, b, preferred_element_type=F32)


def _dot_nt(a, b):
    return lax.dot_general(a, b, (((1,), (1,)), ((), ())), preferred_element_type=F32)


def _dot_tn(a, b):
    return lax.dot_general(a, b, (((0,), (0,)), ((), ())), preferred_element_type=F32)


def _group_index(tm, group0, rows_per_group):
    tiles_per_group = rows_per_group // tm
    return lambda i: group0 + i // tiles_per_group


def _mods_kernel(c_ref, w_ref, b_ref, o_ref):
    s = _silu(c_ref[...]).astype(BF16)
    o_ref[...] = _dot(s, w_ref[...].astype(BF16)) + b_ref[...]


def _mods_call(cond, mod_w, mod_b):
    depth, d, n = mod_w.shape
    tn = 1024
    return pl.pallas_call(
        _mods_kernel,
        grid=(depth, n // tn),
        in_specs=[
            pl.BlockSpec((N_COND_ROWS, d), lambda l, j: (0, 0)),
            pl.BlockSpec((None, d, tn), lambda l, j: (l, 0, j)),
            pl.BlockSpec((None, 1, tn), lambda l, j: (l, 0, j)),
        ],
        out_specs=pl.BlockSpec((None, N_COND_ROWS, tn), lambda l, j: (l, 0, j)),
        out_shape=jax.ShapeDtypeStruct((depth, N_COND_ROWS, n), F32),
        compiler_params=_params(("parallel", "parallel")),
    )(cond, mod_w, mod_b.reshape(depth, 1, n))


ROW_CHUNK = 256
EW_CHUNK = 64


def _rows(r, n):
    return pl.ds(pl.multiple_of(r * n, n), n)


def _modulated_norm_tile(h_ref, x_ref, g_ref, shift, scale, rs_ref, zero_ref=None):
    tm = x_ref.shape[0]

    def stats(r, carry):
        sl = _rows(r, EW_CHUNK)
        x = x_ref[sl, :]
        rs_ref[sl, :] = lax.rsqrt(jnp.mean(x * x, axis=-1, keepdims=True) + EPS)
        return carry
    lax.fori_loop(0, tm // EW_CHUNK, stats, 0, unroll=4)

    gain = g_ref[...] * (1.0 + scale)

    def apply(r, carry):
        sl = _rows(r, EW_CHUNK)
        h_ref[sl, :] = ((x_ref[sl, :] * rs_ref[sl, :]) * gain + shift).astype(BF16)
        if zero_ref is not None:
            zero_ref[sl, :] = jnp.zeros((EW_CHUNK, zero_ref.shape[1]), F32)
        return carry
    lax.fori_loop(0, tm // EW_CHUNK, apply, 0)


def _ffn_kernel(x_ref, mods_ref, g_ref, wa_ref, wu_ref, wd_ref, *rest, slot, final):
    if final:
        fg_ref, o_ref, h_ref, rs_ref = rest
    else:
        o_ref, h_ref, rs_ref = rest
    j = pl.program_id(1)
    tm = x_ref.shape[0]
    shift = mods_ref[3 * slot:3 * slot + 1, :]
    scale = mods_ref[3 * slot + 1:3 * slot + 2, :]
    gate = mods_ref[3 * slot + 2:3 * slot + 3, :]

    @pl.when(j == 0)
    def _():
        _modulated_norm_tile(h_ref, x_ref, g_ref, shift, scale, rs_ref, zero_ref=o_ref)

    for r in range(tm // ROW_CHUNK):
        sl = slice(r * ROW_CHUNK, (r + 1) * ROW_CHUNK)
        h = h_ref[sl, :]
        a = _dot(h, wa_ref[...].astype(BF16))
        u = _dot(h, wu_ref[...].astype(BF16))
        act = (_silu(a) * u).astype(BF16)
        o_ref[sl, :] += _dot(act, wd_ref[...].astype(BF16))

    @pl.when(j == pl.num_programs(1) - 1)
    def _():
        def body(r, carry):
            sl = _rows(r, EW_CHUNK)
            y = x_ref[sl, :] + (0.5 * gate) * o_ref[sl, :]
            if final:
                y = y * lax.rsqrt(jnp.mean(y * y, axis=-1, keepdims=True) + EPS) * fg_ref[...]
            o_ref[sl, :] = y
            return carry
        lax.fori_loop(0, tm // EW_CHUNK, body, 0)


def _ffn_call(x, mods, norm_g, w_up, w_down, *, layer, slot, group0, rows_per_group, final_g=None):
    t, d = x.shape
    f = w_down.shape[1]
    tm, tf = 1024, 512
    nf = f // tf
    group = _group_index(tm, group0, rows_per_group)
    norm_slot = slot
    in_specs = [
        pl.BlockSpec((tm, d), lambda i, j: (i, 0), pipeline_mode=pl.Buffered(1)),
        pl.BlockSpec((None, None, N_MOD, d), lambda i, j: (layer, group(i), 0, 0)),
        pl.BlockSpec((None, None, 1, d), lambda i, j: (layer, norm_slot, 0, 0)),
        pl.BlockSpec((None, d, tf), lambda i, j: (layer, 0, j)),
        pl.BlockSpec((None, d, tf), lambda i, j: (layer, 0, nf + j)),
        pl.BlockSpec((None, tf, d), lambda i, j: (layer, j, 0)),
    ]
    args = [x, mods, norm_g.reshape(norm_g.shape[0], norm_g.shape[1], 1, d), w_up, w_up, w_down]
    if final_g is not None:
        in_specs.append(pl.BlockSpec((1, d), lambda i, j: (0, 0)))
        args.append(final_g.reshape(1, d))
    return pl.pallas_call(
        functools.partial(_ffn_kernel, slot=slot, final=final_g is not None),
        grid=(t // tm, nf),
        in_specs=in_specs,
        out_specs=pl.BlockSpec((tm, d), lambda i, j: (i, 0)),
        out_shape=jax.ShapeDtypeStruct((t, d), F32),
        scratch_shapes=[pltpu.VMEM((tm, d), BF16), pltpu.VMEM((tm, 1), F32)],
        compiler_params=_params(("parallel", "arbitrary"), vmem=FFN_VMEM_LIMIT),
    )(*args)


def _inproj_kernel(x_ref, mods_ref, g_ref, w_ref, z_ref, h_ref, w_s, rs_ref):
    j = pl.program_id(1)
    tm = x_ref.shape[0]
    shift = mods_ref[3:4, :]
    scale = mods_ref[4:5, :]

    @pl.when(j == 0)
    def _():
        _modulated_norm_tile(h_ref, x_ref, g_ref, shift, scale, rs_ref)

    w_s[...] = w_ref[...].astype(BF16)

    for r in range(tm // ROW_CHUNK):
        sl = slice(r * ROW_CHUNK, (r + 1) * ROW_CHUNK)
        z_ref[sl, :] = _dot(h_ref[sl, :], w_s[...])


def _inproj_call(x, mods, norm_g, w_in, *, layer, group0, rows_per_group):
    t, d = x.shape
    n = w_in.shape[2]
    tm, tn = 1024, 1280
    group = _group_index(tm, group0, rows_per_group)
    return pl.pallas_call(
        _inproj_kernel,
        grid=(t // tm, n // tn),
        in_specs=[
            pl.BlockSpec((tm, d), lambda i, j: (i, 0), pipeline_mode=pl.Buffered(1)),
            pl.BlockSpec((None, None, N_MOD, d), lambda i, j: (layer, group(i), 0, 0)),
            pl.BlockSpec((None, None, 1, d), lambda i, j: (layer, 1, 0, 0)),
            pl.BlockSpec((None, d, tn), lambda i, j: (layer, 0, j)),
        ],
        out_specs=pl.BlockSpec((tm, tn), lambda i, j: (i, j)),
        out_shape=jax.ShapeDtypeStruct((t, n), F32),
        scratch_shapes=[pltpu.VMEM((tm, d), BF16), pltpu.VMEM((d, tn), BF16), pltpu.VMEM((tm, 1), F32)],
        compiler_params=_params(("parallel", "arbitrary")),
    )(x, mods, norm_g.reshape(norm_g.shape[0], norm_g.shape[1], 1, d), w_in)


def _log_forget_and_key(zf, log_lb, log1m_lb):
    l1p = jnp.log1p(jnp.exp(-jnp.abs(zf)))
    b = log1m_lb + (jnp.minimum(zf, 0.0) - l1p)
    log_k = log1m_lb + (jnp.minimum(-zf, 0.0) - l1p)
    hi = jnp.maximum(log_lb, b)
    return hi + jnp.log1p(jnp.exp(-jnp.abs(log_lb - b))), log_k


def _hgrn_kernel(zq_ref, zff_ref, zfb_ref, zi_ref, zo_ref, lg_ref, ng_ref, *rest, layer, has_s0):
    if has_s0:
        s0_ref, o_ref, sfin_ref, q_s, c_s, b_s, qb_s, kd_s, dec_s, of_s, ob_s, st_s = rest
    else:
        o_ref, sfin_ref, q_s, c_s, b_s, qb_s, kd_s, dec_s, of_s, ob_s, st_s = rest
    n = zq_ref.shape[0]
    c = HG_CHUNK
    n_chunks = n // c
    blk = HG_BLOCK
    cpb = blk // c
    sh = c.bit_length() - 1

    logits = lg_ref[...]
    depth = logits.shape[0]
    mx = logits[0]
    for i in range(1, depth):
        mx = jnp.maximum(mx, logits[i])
    ex = [jnp.exp(logits[i] - mx) for i in range(depth)]
    tot = ex[0]
    for i in range(1, depth):
        tot = tot + ex[i]
    lb = jnp.zeros_like(mx)
    for i in range(1, layer + 1):
        lb = lb + ex[i] / tot
    log_lb = jnp.log(lb)
    log1m_lb = jnp.log1p(-lb)

    if has_s0:
        st_s[...] = s0_ref[...]
    else:
        st_s[...] = jnp.zeros(st_s.shape, F32)

    br = lax.broadcasted_iota(jnp.int32, (blk, blk), 0)
    bc = lax.broadcasted_iota(jnp.int32, (blk, blk), 1)
    same = (br >> sh) == (bc >> sh)
    cum_mat = [(same & (bc <= br)).astype(BF16), (same & (bc >= br)).astype(BF16)]
    tot_mat = same.astype(BF16)

    def gates(i, carry):
        rows = _rows(i, blk)
        q = _silu(zq_ref[rows, :])
        q_s[rows, :] = q
        for d in range(2):
            zf = (zff_ref if d == 0 else zfb_ref)[rows, :]
            lf, log_k = _log_forget_and_key(zf, log_lb[d:d + 1, :], log1m_lb[d:d + 1, :])
            k = jnp.exp(log_k)
            hi = lf.astype(BF16)
            lo = (lf - hi.astype(F32)).astype(BF16)
            b = _dot(cum_mat[d], hi) + _dot(cum_mat[d], lo)
            b_tot = _dot(tot_mat, hi) + _dot(tot_mat, lo)
            b2 = b * LOG2E
            b_s[d, rows, :] = b2
            c_s[d, rows, :] = b2 - log_k * LOG2E
            qb_s[d, rows, :] = (q * jnp.exp(b)).astype(BF16)
            kd_s[d, rows, :] = (k * jnp.exp(b_tot - b)).astype(BF16)
            dec_s[d, rows, :] = jnp.exp(b_tot)
        return carry
    lax.fori_loop(0, n // blk, gates, 0)

    nh = HG_HEADS
    hs = [slice(h * HG_DK, (h + 1) * HG_DK) for h in range(nh)]
    s8 = lax.broadcasted_iota(jnp.int32, (SUBLANE, nh * HG_DK), 0)
    pair = 2 * HG_DK
    pr = lax.broadcasted_iota(jnp.int32, (pair, pair), 0)
    pc = lax.broadcasted_iota(jnp.int32, (pair, pair), 1)
    ones_pair = ((pr >= HG_DK) == (pc >= HG_DK)).astype(BF16)
    sel_r = lax.broadcasted_iota(jnp.int32, (c, c * c), 0)
    sel_c = lax.broadcasted_iota(jnp.int32, (c, c * c), 1)
    sel = ((sel_c >= sel_r * c) & (sel_c < sel_r * c + c)).astype(BF16)
    zero8 = jnp.zeros((SUBLANE, nh * HG_DK), F32)

    def scores_of(d, rows):
        q = q_s[rows, :]
        b2 = b_s[d, rows, :]
        c2 = c_s[d, rows, :]
        halves = [c2[:SUBLANE], c2[SUBLANE:]]
        parts = []
        for t in range(c):
            bt, qt = b2[t:t + 1, :], q[t:t + 1, :]
            own = t // SUBLANE
            row = []
            for half, ch in enumerate(halves):
                if half == own:
                    tt = t - own * SUBLANE
                    vis = (s8 <= tt) if d == 0 else (s8 >= tt)
                    row.append(qt * jnp.exp2(jnp.where(vis, bt - ch, NEG)))
                elif (half < own) == (d == 0):
                    row.append(qt * jnp.exp2(bt - ch))
                else:
                    row.append(zero8)
            parts.append(row)
        p = jnp.concatenate([parts[t][half] for t in range(c) for half in range(2)], axis=0).astype(BF16)
        return jnp.concatenate([_dot(p[:, j * pair:(j + 1) * pair], ones_pair) for j in range(nh * HG_DK // pair)],
                               axis=1)

    def state_step(d, h, rows, ci):
        st = st_s[d, h]
        o_inter = _dot_nt(qb_s[d, rows, hs[h]], st.astype(BF16))
        dec8 = dec_s[d, pl.ds(pl.multiple_of(ci * c, c), SUBLANE), hs[h]]
        st_dec = (st.reshape(HG_DK // SUBLANE, SUBLANE, HG_DK) * dec8[None]).reshape(HG_DK, HG_DK)
        st_s[d, h] = st_dec + _dot_tn(zi_ref[rows, hs[h]].astype(BF16), kd_s[d, rows, hs[h]])
        return o_inter

    per_trip = HG_CHUNKS_PER_TRIP

    def body(i, carry):
        cis = [[per_trip * i + u for u in range(per_trip)],
               [n_chunks - 1 - per_trip * i - u for u in range(per_trip)]]
        rows = [[_rows(ci, c) for ci in cis[d]] for d in range(2)]
        scores = [[scores_of(d, rows[d][u]) for u in range(per_trip)] for d in range(2)]
        o_inter = [[[state_step(d, h, rows[d][u], cis[d][u]) for h in range(nh)] for u in range(per_trip)]
                   for d in range(2)]
        for d in range(2):
            for u in range(per_trip):
                v_rep = jnp.concatenate([zi_ref[rows[d][u], :]] * c, axis=0)
                o_intra = _dot(sel, (scores[d][u] * v_rep).astype(BF16))
                for h in range(nh):
                    (of_s if d == 0 else ob_s)[rows[d][u], hs[h]] = o_intra[:, hs[h]] + o_inter[d][u][h]
        return carry
    lax.fori_loop(0, n_chunks // per_trip, body, 0)

    sfin_ref[...] = st_s[...]

    def fin(r, carry):
        rows = _rows(r, EW_CHUNK)
        for h in range(HG_HEADS):
            cols = slice(h * HG_DK, (h + 1) * HG_DK)
            o = of_s[rows, cols] + ob_s[rows, cols]
            o = o * lax.rsqrt(jnp.mean(o * o, axis=-1, keepdims=True) + EPS) * ng_ref[h:h + 1, :]
            o_ref[rows, cols] = (o * _silu(zo_ref[rows, cols])).astype(o_ref.dtype)
        return carry
    lax.fori_loop(0, n // EW_CHUNK, fin, 0)


def _hgrn_call(z, lb_logits, hgrn_norm_g, s0_t, *, layer, n_seq, n_tok):
    w = HG_HEADS * HG_DK
    depth = lb_logits.shape[0]
    zspec = lambda k: pl.BlockSpec((n_tok, w), lambda s: (s, k))
    in_specs = [zspec(0), zspec(1), zspec(2), zspec(3), zspec(4),
                pl.BlockSpec((depth, 2, w), lambda s: (0, 0, 0)),
                pl.BlockSpec((None, HG_HEADS, HG_DK), lambda s: (layer, 0, 0))]
    args = [z, z, z, z, z, lb_logits, hgrn_norm_g]
    st_spec = pl.BlockSpec((None, 2, HG_HEADS, HG_DK, HG_DK), lambda s: (s, 0, 0, 0, 0))
    if s0_t is not None:
        in_specs.append(st_spec)
        args.append(s0_t)
    return pl.pallas_call(
        functools.partial(_hgrn_kernel, layer=layer, has_s0=s0_t is not None),
        grid=(n_seq,),
        in_specs=in_specs,
        out_specs=[pl.BlockSpec((n_tok, w), lambda s: (s, 0)), st_spec],
        out_shape=[jax.ShapeDtypeStruct((n_seq * n_tok, w), BF16),
                   jax.ShapeDtypeStruct((n_seq, 2, HG_HEADS, HG_DK, HG_DK), F32)],
        scratch_shapes=[pltpu.VMEM((n_tok, w), F32),
                        pltpu.VMEM((2, n_tok, w), F32),
                        pltpu.VMEM((2, n_tok, w), F32),
                        pltpu.VMEM((2, n_tok, w), BF16),
                        pltpu.VMEM((2, n_tok, w), BF16),
                        pltpu.VMEM((2, n_tok, w), F32),
                        pltpu.VMEM((n_tok, w), F32), pltpu.VMEM((n_tok, w), F32),
                        pltpu.VMEM((2, HG_HEADS, HG_DK, HG_DK), F32)],
        compiler_params=_params(("parallel",)),
    )(*args)


def _ctx_attn_kernel(q0_ref, q1_ref, k0_ref, k1_ref, v0_ref, v1_ref, o_ref, ko_ref, vo_ref, *, scale):
    q_refs, k_refs, v_refs = (q0_ref, q1_ref), (k0_ref, k1_ref), (v0_ref, v1_ref)
    half = q0_ref.shape[1]
    heads_per_block = half // NA_HEAD_DIM
    ko_ref[:, :half] = k0_ref[...]
    ko_ref[:, half:] = k1_ref[...]
    vo_ref[:, :half] = v0_ref[...]
    vo_ref[:, half:] = v1_ref[...]

    def head(refs, h):
        blk, off = divmod(h, heads_per_block)
        return refs[blk][:, off * NA_HEAD_DIM:(off + 1) * NA_HEAD_DIM].astype(BF16)

    scores = [_dot_nt(head(q_refs, h), head(k_refs, h)) * scale for h in range(NA_HEADS)]
    probs = []
    for s in scores:
        p = jnp.exp(s - jnp.max(s, axis=-1, keepdims=True))
        probs.append((p / jnp.sum(p, axis=-1, keepdims=True)).astype(BF16))
    for h in range(NA_HEADS):
        o = _dot(probs[h], head(v_refs, h))
        o_ref[:, h * NA_HEAD_DIM:(h + 1) * NA_HEAD_DIM] = o.astype(o_ref.dtype)


def _ctx_attn_call(z, *, n_seq, n_tok, col0):
    w = NA_HEADS * NA_HEAD_DIM
    half = w // 2
    zspec = lambda k: pl.BlockSpec((n_tok, half), lambda s: (s, col0 // half + k))
    return pl.pallas_call(
        functools.partial(_ctx_attn_kernel, scale=NA_HEAD_DIM ** -0.5),
        grid=(n_seq,),
        in_specs=[zspec(k) for k in range(6)],
        out_specs=[pl.BlockSpec((n_tok, w), lambda s: (s, 0))] * 3,
        out_shape=[jax.ShapeDtypeStruct((n_seq * n_tok, w), BF16),
                   jax.ShapeDtypeStruct((n_seq * n_tok, w), F32),
                   jax.ShapeDtypeStruct((n_seq * n_tok, w), F32)],
        compiler_params=_params(("parallel",)),
    )(z, z, z, z, z, z)


def _na_kernel(q_ref, k_ref, v_ref, kc_ref, vc_ref, bias_ref, o_ref, *, scale, rows, kh):
    q = q_ref[...].astype(BF16)
    k = k_ref[...].astype(BF16)
    v = v_ref[...].astype(BF16)
    qrow = lambda r: slice(r * GRID_W, (r + 1) * GRID_W)
    krows = lambda r: slice(min(max(r - kh // 2, 0), rows - kh) * GRID_W,
                            (min(max(r - kh // 2, 0), rows - kh) + kh) * GRID_W)
    s_ctx = _dot_nt(q, kc_ref[...].astype(BF16)) * scale
    s_loc = [_dot_nt(q[qrow(r)], k[krows(r)]) * scale + bias_ref[r] for r in range(rows)]
    p_loc, p_ctx = [], []
    for r in range(rows):
        sc = s_ctx[qrow(r)]
        m = jnp.maximum(jnp.max(s_loc[r], axis=-1, keepdims=True), jnp.max(sc, axis=-1, keepdims=True))
        el = jnp.exp(s_loc[r] - m)
        ec = jnp.exp(sc - m)
        den = jnp.sum(el, axis=-1, keepdims=True) + jnp.sum(ec, axis=-1, keepdims=True)
        p_loc.append((el / den).astype(BF16))
        p_ctx.append((ec / den).astype(BF16))
    o_ctx = _dot(jnp.concatenate(p_ctx, axis=0), vc_ref[...].astype(BF16))
    for r in range(rows):
        o = _dot(p_loc[r], v[krows(r)]) + o_ctx[qrow(r)]
        o_ref[qrow(r), :] = o.astype(o_ref.dtype)


def _na_bias(rpb, rows, kh):
    heads = rpb.shape[0]
    qcol = jnp.arange(GRID_W)
    kcol = jnp.arange(GRID_W)
    win_start = jnp.clip(qcol - NA_KW // 2, 0, GRID_W - NA_KW)
    in_win = (kcol[None, :] >= win_start[:, None]) & (kcol[None, :] < win_start[:, None] + NA_KW)
    lo = GRID_W - NA_KW
    pad = jnp.pad(rpb.astype(F32), ((0, 0), (0, 0), (lo, lo)))
    toep = jnp.stack([pad[:, :, GRID_W - 1 - q:2 * GRID_W - 1 - q] for q in range(GRID_W)], axis=2)
    toep = jnp.where(in_win[None, None], toep, NEG)
    per_row = []
    for r in range(rows):
        dy0 = min(max(r - kh // 2, 0), rows - kh) - r + NA_KH - 1
        per_row.append(toep[:, dy0:dy0 + kh])
    bias = jnp.stack(per_row, axis=1)
    return jnp.transpose(bias, (0, 1, 3, 2, 4)).reshape(heads, rows, GRID_W, kh * GRID_W)


def _na_call(z, cache_k, cache_v, bias, *, layer, n_seq, n_tok, col_block0):
    rows = n_tok // GRID_W
    kh = min(NA_KH, rows)
    dh = NA_HEAD_DIM
    past = cache_k.shape[2]
    zspec = lambda k: pl.BlockSpec((n_tok, dh), lambda b, h: (b, col_block0 + k * NA_HEADS + h))
    cspec = pl.BlockSpec((None, None, past, dh), lambda b, h: (b, layer, 0, h))
    return pl.pallas_call(
        functools.partial(_na_kernel, scale=dh ** -0.5, rows=rows, kh=kh),
        grid=(n_seq, NA_HEADS),
        in_specs=[zspec(0), zspec(1), zspec(2), cspec, cspec,
                  pl.BlockSpec((None, rows, GRID_W, kh * GRID_W), lambda b, h: (h, 0, 0, 0))],
        out_specs=pl.BlockSpec((n_tok, dh), lambda b, h: (b, h)),
        out_shape=jax.ShapeDtypeStruct((n_seq * n_tok, NA_HEADS * dh), BF16),
        compiler_params=_params(("parallel", "parallel")),
    )(z, z, z, cache_k, cache_v, bias)


def _lru_kernel(zx_ref, zg_ref, cw_ref, cb_ref, wa_ref, ba_ref, wx_ref, bx_ref, lam_ref, *rest,
                n_seq, n_tok, has_h0):
    if has_h0:
        h0_ref, y_ref, hfin_ref, a_s, u_s, y_s = rest
    else:
        y_ref, hfin_ref, a_s, u_s, y_s = rest
    tpos = lax.broadcasted_iota(jnp.int32, (n_tok, LANE), 0)
    gpos = tpos & (SUBLANE - 1)
    neg_lam = -lam_ref[...]
    softplus = jnp.maximum(neg_lam, 0.0) + jnp.log1p(jnp.exp(-jnp.abs(neg_lam)))
    left = LRU_CONV // 2

    def gates(g, carry):
        rows = _rows(g, n_tok)
        zx = zx_ref[rows, :]
        x = cb_ref[...] + cw_ref[left:left + 1, :] * zx
        for j in range(LRU_CONV):
            off = j - left
            if off == 0:
                continue
            shifted = pltpu.roll(zx, (-off) % n_tok, 0)
            valid = (tpos + off >= 0) & (tpos + off < n_tok)
            x = x + cw_ref[j:j + 1, :] * jnp.where(valid, shifted, 0.0)
        xb = x.astype(BF16)
        for d in range(2):
            r_gate = jax.nn.sigmoid(_dot(xb, wa_ref[d].astype(BF16)) + ba_ref[d:d + 1, :])
            i_gate = jax.nn.sigmoid(_dot(xb, wx_ref[d].astype(BF16)) + bx_ref[d:d + 1, :])
            log_a = (-LRU_C * r_gate) * softplus[d:d + 1, :]
            a = jnp.exp(log_a)
            u = jnp.sqrt(-_expm1(2.0 * log_a)) * (i_gate * x)
            sh = 1
            while sh < SUBLANE:
                if d == 0:
                    inside = gpos >= sh
                    amt = sh
                else:
                    inside = gpos < SUBLANE - sh
                    amt = n_tok - sh
                a_prev = jnp.where(inside, pltpu.roll(a, amt, 0), 1.0)
                u_prev = jnp.where(inside, pltpu.roll(u, amt, 0), 0.0)
                u = u + a * u_prev
                a = a * a_prev
                sh *= 2
            a_s[d, rows, :] = a
            u_s[d, rows, :] = u
        return carry
    lax.fori_loop(0, n_seq, gates, 0)

    n_groups = n_tok // SUBLANE
    edge = (SUBLANE - 1, 0)

    def bcast(row):
        return jnp.broadcast_to(row, (SUBLANE, LANE))

    if has_h0:
        h_init = tuple(bcast(h0_ref[d, g:g + 1, :]) for g in range(n_seq) for d in range(2))
    else:
        h_init = tuple(jnp.zeros((SUBLANE, LANE), F32) for _ in range(2 * n_seq))

    def step(j, carry):
        out = []
        for g in range(n_seq):
            for d in range(2):
                grp = j if d == 0 else n_groups - 1 - j
                rows = pl.ds(pl.multiple_of(g * n_tok + grp * SUBLANE, SUBLANE), SUBLANE)
                h = u_s[d, rows, :] + a_s[d, rows, :] * carry[2 * g + d]
                y_s[d, rows, :] = h
                out.append(bcast(h[edge[d]:edge[d] + 1, :]))
        return tuple(out)
    h_last = lax.fori_loop(0, n_groups, step, h_init)
    for g in range(n_seq):
        for d in range(2):
            hfin_ref[d, g:g + 1, :] = h_last[2 * g + d][0:1, :]

    def fin(g, carry):
        rows = _rows(g, n_tok)
        y = (y_s[0, rows, :] + y_s[1, rows, :]) * jax.nn.gelu(zg_ref[rows, :], approximate=True)
        y_ref[rows, :] = y.astype(y_ref.dtype)
        return carry
    lax.fori_loop(0, n_seq, fin, 0)


def _block_diag_pairs(w):
    depth, nd, nb, bw, _ = w.shape
    w = w.reshape(depth, nd, nb // 2, 2, bw, bw)
    eye = jnp.eye(2, dtype=w.dtype)
    out = w[:, :, :, :, :, None, :] * eye[None, None, None, :, None, :, None]
    return out.reshape(depth, nd, nb // 2, 2 * bw, 2 * bw)


def _lru_call(z, conv_w, conv_b, wa_bd, b_a, wx_bd, b_x, lam, h0_t, *, layer, n_seq, n_tok, col_block0):
    w = conv_b.shape[1]
    n_cb = w // LANE
    rows = n_seq * n_tok
    zspec = lambda k: pl.BlockSpec((rows, LANE), lambda cb: (0, col_block0 + k * n_cb + cb))
    vec2 = pl.BlockSpec((None, 2, LANE), lambda cb: (layer, 0, cb))
    wspec = pl.BlockSpec((None, 2, None, LANE, LANE), lambda cb: (layer, 0, cb, 0, 0))
    hspec = pl.BlockSpec((2, n_seq, LANE), lambda cb: (0, 0, cb))
    in_specs = [zspec(0), zspec(1),
                pl.BlockSpec((None, LRU_CONV, LANE), lambda cb: (layer, 0, cb)),
                pl.BlockSpec((None, 1, LANE), lambda cb: (layer, 0, cb)),
                wspec, vec2, wspec, vec2, vec2]
    args = [z, z, conv_w, conv_b.reshape(conv_b.shape[0], 1, w), wa_bd, b_a, wx_bd, b_x, lam]
    if h0_t is not None:
        in_specs.append(hspec)
        args.append(h0_t)
    return pl.pallas_call(
        functools.partial(_lru_kernel, n_seq=n_seq, n_tok=n_tok, has_h0=h0_t is not None),
        grid=(n_cb,),
        in_specs=in_specs,
        out_specs=[pl.BlockSpec((rows, LANE), lambda cb: (0, cb)), hspec],
        out_shape=[jax.ShapeDtypeStruct((rows, w), BF16), jax.ShapeDtypeStruct((2, n_seq, w), F32)],
        scratch_shapes=[pltpu.VMEM((2, rows, LANE), F32)] * 3,
        compiler_params=_params(("parallel",)),
    )(*args)


def _merge_kernel(x_ref, mods_ref, oa_ref, ob_ref, oc_ref, ga_ref, gb_ref, gc_ref,
                  wa_ref, wb_ref, wc_ref, wo_ref, o_ref, wa_s, wb_s, wc_s, wo_s):
    j = pl.program_id(1)
    tm = x_ref.shape[0]
    gate = mods_ref[5:6, :]

    @pl.when(j == 0)
    def _():
        def body(r, carry):
            o_ref[_rows(r, EW_CHUNK), :] = jnp.zeros((EW_CHUNK, o_ref.shape[1]), F32)
            return carry
        lax.fori_loop(0, tm // EW_CHUNK, body, 0)

    wa_s[...] = wa_ref[...].astype(BF16)
    wb_s[...] = wb_ref[...].astype(BF16)
    wc_s[...] = wc_ref[...].astype(BF16)
    wo_s[...] = wo_ref[...].astype(BF16)

    for r in range(tm // ROW_CHUNK):
        sl = slice(r * ROW_CHUNK, (r + 1) * ROW_CHUNK)

        def branch(o_r, w_s, g_r):
            return jax.nn.sigmoid(g_r[sl, :]) * _dot(o_r[sl, :], w_s[...])

        m = branch(oa_ref, wa_s, ga_ref) + branch(ob_ref, wb_s, gb_ref) + branch(oc_ref, wc_s, gc_ref)
        o_ref[sl, :] += _dot(m.astype(BF16), wo_s[...])

    @pl.when(j == pl.num_programs(1) - 1)
    def _():
        def body(r, carry):
            sl = _rows(r, EW_CHUNK)
            o_ref[sl, :] = x_ref[sl, :] + gate * o_ref[sl, :]
            return carry
        lax.fori_loop(0, tm // EW_CHUNK, body, 0)


def _merge_call(x, mods, o_a, o_b, o_c, z, w_proj_a, w_proj_b, w_proj_c, w_out, *, layer, group0,
                rows_per_group, gate_col0):
    t, d = x.shape
    tm, tn = 1024, 256
    nj = d // tn
    g0 = gate_col0 // tn
    group = _group_index(tm, group0, rows_per_group)
    gspec = lambda k: pl.BlockSpec((tm, tn), lambda i, j: (i, g0 + k * nj + j))
    ospec = lambda w: pl.BlockSpec((tm, w), lambda i, j: (i, 0))
    wspec = lambda w: pl.BlockSpec((None, w, tn), lambda i, j: (layer, 0, j))
    return pl.pallas_call(
        _merge_kernel,
        grid=(t // tm, nj),
        in_specs=[
            pl.BlockSpec((tm, d), lambda i, j: (i, 0), pipeline_mode=pl.Buffered(1)),
            pl.BlockSpec((None, None, N_MOD, d), lambda i, j: (layer, group(i), 0, 0)),
            ospec(o_a.shape[1]), ospec(o_b.shape[1]), ospec(o_c.shape[1]),
            gspec(0), gspec(1), gspec(2),
            wspec(o_a.shape[1]), wspec(o_b.shape[1]), wspec(o_c.shape[1]),
            pl.BlockSpec((None, tn, d), lambda i, j: (layer, j, 0)),
        ],
        out_specs=pl.BlockSpec((tm, d), lambda i, j: (i, 0)),
        out_shape=jax.ShapeDtypeStruct((t, d), F32),
        scratch_shapes=[pltpu.VMEM((o_a.shape[1], tn), BF16), pltpu.VMEM((o_b.shape[1], tn), BF16),
                        pltpu.VMEM((o_c.shape[1], tn), BF16), pltpu.VMEM((tn, d), BF16)],
        compiler_params=_params(("parallel", "arbitrary")),
    )(x, mods, o_a, o_b, o_c, z, z, z, w_proj_a, w_proj_b, w_proj_c, w_out)


def kernel(x_prompt, x_sample, cache_na_k, cache_na_v, state_hgrn, state_lru, c, c_ctx, mod_w, mod_b, norm_g, ffn1_w_up, ffn1_w_down, ffn2_w_up, ffn2_w_down, w_in, hgrn_lb_logits, hgrn_norm_g, na_rpb, lru_conv_w, lru_conv_b, lru_w_a, lru_b_a, lru_w_x, lru_b_x, lru_lambda, w_proj_a, w_proj_b, w_proj_c, w_out, final_norm_g):
    b_ctx, n_ctx, d = x_prompt.shape
    b_lat, n_lat, _ = x_sample.shape
    depth = mod_w.shape[0]
    t_ctx, t_lat = b_ctx * n_ctx, b_lat * n_lat
    hg_w = HG_HEADS * HG_DK
    na_w = NA_HEADS * NA_HEAD_DIM
    lru_w = lru_conv_b.shape[1]
    na_col0 = 5 * hg_w
    lru_col0 = na_col0 + 3 * na_w
    gate_col0 = lru_col0 + 2 * lru_w

    ctx = dict(group0=0, rows_per_group=t_ctx)
    lat = dict(group0=1, rows_per_group=n_lat)
    xc = x_prompt.reshape(t_ctx, d)
    xl = x_sample.reshape(t_lat, d)
    cond = jnp.concatenate([c_ctx[None], c, jnp.zeros((N_COND_ROWS - 1 - b_lat, d), F32)], axis=0)
    mods = _mods_call(cond, mod_w, mod_b).reshape(depth, N_COND_ROWS, N_MOD, d)

    rows = n_lat // GRID_W
    kh = min(NA_KH, rows)
    wa_bd = _block_diag_pairs(lru_w_a)
    wx_bd = _block_diag_pairs(lru_w_x)
    cache_k = cache_na_k.reshape(b_lat, depth, cache_na_k.shape[2], na_w)
    cache_v = cache_na_v.reshape(b_lat, depth, cache_na_v.shape[2], na_w)
    s0_t = jnp.swapaxes(state_hgrn, -1, -2)
    h0_t = jnp.transpose(state_lru, (1, 2, 0, 3))

    ks, vs, hgs, lrus = [], [], [], []
    for l in range(depth):
        last = l == depth - 1
        final_g = final_norm_g if last else None
        lru_args = (lru_conv_w, lru_conv_b, wa_bd, lru_b_a, wx_bd, lru_b_x, lru_lambda)
        merge_w = (w_proj_a, w_proj_b, w_proj_c, w_out)

        xc = _ffn_call(xc, mods, norm_g, ffn1_w_up, ffn1_w_down, layer=l, slot=0, **ctx)
        zc = _inproj_call(xc, mods, norm_g, w_in, layer=l, **ctx)
        oa, s_c = _hgrn_call(zc, hgrn_lb_logits, hgrn_norm_g, None, layer=l, n_seq=b_ctx, n_tok=n_ctx)
        ob, k_l, v_l = _ctx_attn_call(zc, n_seq=b_ctx, n_tok=n_ctx, col0=na_col0)
        oc, h_c = _lru_call(zc, *lru_args, None, layer=l, n_seq=b_ctx, n_tok=n_ctx, col_block0=lru_col0 // LANE)
        xc = _merge_call(xc, mods, oa, ob, oc, zc, *merge_w, layer=l, gate_col0=gate_col0, **ctx)
        xc = _ffn_call(xc, mods, norm_g, ffn2_w_up, ffn2_w_down, layer=l, slot=2, final_g=final_g, **ctx)

        xl = _ffn_call(xl, mods, norm_g, ffn1_w_up, ffn1_w_down, layer=l, slot=0, **lat)
        zl = _inproj_call(xl, mods, norm_g, w_in, layer=l, **lat)
        oa, _ = _hgrn_call(zl, hgrn_lb_logits, hgrn_norm_g, s0_t[:, l], layer=l, n_seq=b_lat, n_tok=n_lat)
        bias = _na_bias(na_rpb[l], rows, kh)
        ob = _na_call(zl, cache_k, cache_v, bias, layer=l, n_seq=b_lat, n_tok=n_lat,
                      col_block0=na_col0 // NA_HEAD_DIM)
        oc, _ = _lru_call(zl, *lru_args, h0_t[l], layer=l, n_seq=b_lat, n_tok=n_lat, col_block0=lru_col0 // LANE)
        xl = _merge_call(xl, mods, oa, ob, oc, zl, *merge_w, layer=l, gate_col0=gate_col0, **lat)
        xl = _ffn_call(xl, mods, norm_g, ffn2_w_up, ffn2_w_down, layer=l, slot=2, final_g=final_g, **lat)

        ks.append(k_l.reshape(b_ctx, n_ctx, NA_HEADS, NA_HEAD_DIM))
        vs.append(v_l.reshape(b_ctx, n_ctx, NA_HEADS, NA_HEAD_DIM))
        hgs.append(jnp.swapaxes(s_c, -1, -2))
        lrus.append(jnp.transpose(h_c, (1, 0, 2)))

    return (xc.reshape(b_ctx, n_ctx, d), xl.reshape(b_lat, n_lat, d), jnp.stack(ks, axis=1),
            jnp.stack(vs, axis=1), jnp.stack(hgs, axis=1), jnp.stack(lrus, axis=1))
```

```python
import functools

import jax
import jax.numpy as jnp
from jax import lax
from jax.experimental import pallas as pl
from jax.experimental.pallas import tpu as pltpu

F32 = jnp.float32
BF16 = jnp.bfloat16

EPS = 1e-6
NEG = -1e30
LOG2E = 1.4426950408889634
N_MOD = 9
N_COND_ROWS = 8
HG_HEADS = 4
HG_DK = 128
HG_CHUNK = 16
HG_BLOCK = 128
HG_CHUNKS_PER_TRIP = 1
NA_HEADS = 8
NA_HEAD_DIM = 128
NA_KH = 8
NA_KW = 16
NA_QBW = 8
NA_BAND = NA_QBW + NA_KW
GRID_W = 64
LRU_BLOCKS = 8
LRU_CONV = 4
LRU_C = 8.0
LANE = 128
SUBLANE = 8
VMEM_LIMIT = 56 * 1024 * 1024
FFN_VMEM_LIMIT = 60 * 1024 * 1024


def _params(sem, vmem=VMEM_LIMIT):
    return pltpu.CompilerParams(dimension_semantics=sem, vmem_limit_bytes=vmem)


def _silu(x):
    return x * jax.nn.sigmoid(x)


def _expm1(x):
    u = jnp.exp(x)
    um1 = u - 1.0
    y = um1 * x / jnp.where(u == 1.0, 1.0, jnp.log(u))
    return jnp.where(u == 1.0, x, jnp.where(um1 == -1.0, -1.0, y))


def _dot(a, b):
    return jnp.dot(a, b, preferred_element_type=F32)


def _dot_nt(a, b):
    return lax.dot_general(a, b, (((1,), (1,)), ((), ())), preferred_element_type=F32)


def _dot_tn(a, b):
    return lax.dot_general(a, b, (((0,), (0,)), ((), ())), preferred_element_type=F32)


def _group_index(tm, group0, rows_per_group):
    tiles_per_group = rows_per_group // tm
    return lambda i: group0 + i // tiles_per_group


def _mods_kernel(c_ref, w_ref, b_ref, o_ref):
    s = _silu(c_ref[...]).astype(BF16)
    o_ref[...] = _dot(s, w_ref[...].astype(BF16)) + b_ref[...]


def _mods_call(cond, mod_w, mod_b):
    depth, d, n = mod_w.shape
    tn = 1024
    return pl.pallas_call(
        _mods_kernel,
        grid=(depth, n // tn),
        in_specs=[
            pl.BlockSpec((N_COND_ROWS, d), lambda l, j: (0, 0)),
            pl.BlockSpec((None, d, tn), lambda l, j: (l, 0, j)),
            pl.BlockSpec((None, 1, tn), lambda l, j: (l, 0, j)),
        ],
        out_specs=pl.BlockSpec((None, N_COND_ROWS, tn), lambda l, j: (l, 0, j)),
        out_shape=jax.ShapeDtypeStruct((depth, N_COND_ROWS, n), F32),
        compiler_params=_params(("parallel", "parallel")),
    )(cond, mod_w, mod_b.reshape(depth, 1, n))


ROW_CHUNK = 256
EW_CHUNK = 64


def _rows(r, n):
    return pl.ds(pl.multiple_of(r * n, n), n)


def _modulated_norm_tile(h_ref, x_ref, g_ref, shift, scale, rs_ref, zero_ref=None):
    tm = x_ref.shape[0]

    def stats(r, carry):
        sl = _rows(r, EW_CHUNK)
        x = x_ref[sl, :]
        rs_ref[sl, :] = lax.rsqrt(jnp.mean(x * x, axis=-1, keepdims=True) + EPS)
        return carry
    lax.fori_loop(0, tm // EW_CHUNK, stats, 0, unroll=4)

    gain = g_ref[...] * (1.0 + scale)

    def apply(r, carry):
        sl = _rows(r, EW_CHUNK)
        h_ref[sl, :] = ((x_ref[sl, :] * rs_ref[sl, :]) * gain + shift).astype(BF16)
        if zero_ref is not None:
            zero_ref[sl, :] = jnp.zeros((EW_CHUNK, zero_ref.shape[1]), F32)
        return carry
    lax.fori_loop(0, tm // EW_CHUNK, apply, 0)


def _ffn_kernel(x_ref, mods_ref, g_ref, wa_ref, wu_ref, wd_ref, *rest, slot, final):
    if final:
        fg_ref, o_ref, h_ref, rs_ref = rest
    else:
        o_ref, h_ref, rs_ref = rest
    j = pl.program_id(1)
    tm = x_ref.shape[0]
    shift = mods_ref[3 * slot:3 * slot + 1, :]
    scale = mods_ref[3 * slot + 1:3 * slot + 2, :]
    gate = mods_ref[3 * slot + 2:3 * slot + 3, :]

    @pl.when(j == 0)
    def _():
        _modulated_norm_tile(h_ref, x_ref, g_ref, shift, scale, rs_ref, zero_ref=o_ref)

    for r in range(tm // ROW_CHUNK):
        sl = slice(r * ROW_CHUNK, (r + 1) * ROW_CHUNK)
        h = h_ref[sl, :]
        a = _dot(h, wa_ref[...].astype(BF16))
        u = _dot(h, wu_ref[...].astype(BF16))
        act = (_silu(a) * u).astype(BF16)
        o_ref[sl, :] += _dot(act, wd_ref[...].astype(BF16))

    @pl.when(j == pl.num_programs(1) - 1)
    def _():
        def body(r, carry):
            sl = _rows(r, EW_CHUNK)
            y = x_ref[sl, :] + (0.5 * gate) * o_ref[sl, :]
            if final:
                y = y * lax.rsqrt(jnp.mean(y * y, axis=-1, keepdims=True) + EPS) * fg_ref[...]
            o_ref[sl, :] = y
            return carry
        lax.fori_loop(0, tm // EW_CHUNK, body, 0)


def _ffn_call(x, mods, norm_g, w_up, w_down, *, layer, slot, group0, rows_per_group, final_g=None):
    t, d = x.shape
    f = w_down.shape[1]
    tm, tf = 1024, 512
    nf = f // tf
    group = _group_index(tm, group0, rows_per_group)
    norm_slot = slot
    in_specs = [
        pl.BlockSpec((tm, d), lambda i, j: (i, 0), pipeline_mode=pl.Buffered(1)),
        pl.BlockSpec((None, None, N_MOD, d), lambda i, j: (layer, group(i), 0, 0)),
        pl.BlockSpec((None, None, 1, d), lambda i, j: (layer, norm_slot, 0, 0)),
        pl.BlockSpec((None, d, tf), lambda i, j: (layer, 0, j)),
        pl.BlockSpec((None, d, tf), lambda i, j: (layer, 0, nf + j)),
        pl.BlockSpec((None, tf, d), lambda i, j: (layer, j, 0)),
    ]
    args = [x, mods, norm_g.reshape(norm_g.shape[0], norm_g.shape[1], 1, d), w_up, w_up, w_down]
    if final_g is not None:
        in_specs.append(pl.BlockSpec((1, d), lambda i, j: (0, 0)))
        args.append(final_g.reshape(1, d))
    return pl.pallas_call(
        functools.partial(_ffn_kernel, slot=slot, final=final_g is not None),
        grid=(t // tm, nf),
        in_specs=in_specs,
        out_specs=pl.BlockSpec((tm, d), lambda i, j: (i, 0)),
        out_shape=jax.ShapeDtypeStruct((t, d), F32),
        scratch_shapes=[pltpu.VMEM((tm, d), BF16), pltpu.VMEM((tm, 1), F32)],
        compiler_params=_params(("parallel", "arbitrary"), vmem=FFN_VMEM_LIMIT),
    )(*args)


def _inproj_kernel(x_ref, mods_ref, g_ref, w_ref, z_ref, h_ref, w_s, rs_ref):
    j = pl.program_id(1)
    tm = x_ref.shape[0]
    shift = mods_ref[3:4, :]
    scale = mods_ref[4:5, :]

    @pl.when(j == 0)
    def _():
        _modulated_norm_tile(h_ref, x_ref, g_ref, shift, scale, rs_ref)

    w_s[...] = w_ref[...].astype(BF16)

    for r in range(tm // ROW_CHUNK):
        sl = slice(r * ROW_CHUNK, (r + 1) * ROW_CHUNK)
        z_ref[sl, :] = _dot(h_ref[sl, :], w_s[...])


def _inproj_call(x, mods, norm_g, w_in, *, layer, group0, rows_per_group):
    t, d = x.shape
    n = w_in.shape[2]
    tm, tn = 1024, 1280
    group = _group_index(tm, group0, rows_per_group)
    return pl.pallas_call(
        _inproj_kernel,
        grid=(t // tm, n // tn),
        in_specs=[
            pl.BlockSpec((tm, d), lambda i, j: (i, 0), pipeline_mode=pl.Buffered(1)),
            pl.BlockSpec((None, None, N_MOD, d), lambda i, j: (layer, group(i), 0, 0)),
            pl.BlockSpec((None, None, 1, d), lambda i, j: (layer, 1, 0, 0)),
            pl.BlockSpec((None, d, tn), lambda i, j: (layer, 0, j)),
        ],
        out_specs=pl.BlockSpec((tm, tn), lambda i, j: (i, j)),
        out_shape=jax.ShapeDtypeStruct((t, n), F32),
        scratch_shapes=[pltpu.VMEM((tm, d), BF16), pltpu.VMEM((d, tn), BF16), pltpu.VMEM((tm, 1), F32)],
        compiler_params=_params(("parallel", "arbitrary")),
    )(x, mods, norm_g.reshape(norm_g.shape[0], norm_g.shape[1], 1, d), w_in)


def _log_forget_and_key(zf, log_lb, log1m_lb):
    l1p = jnp.log1p(jnp.exp(-jnp.abs(zf)))
    b = log1m_lb + (jnp.minimum(zf, 0.0) - l1p)
    log_k = log1m_lb + (jnp.minimum(-zf, 0.0) - l1p)
    hi = jnp.maximum(log_lb, b)
    return hi + jnp.log1p(jnp.exp(-jnp.abs(log_lb - b))), log_k


def _hgrn_kernel(zq_ref, zff_ref, zfb_ref, zi_ref, zo_ref, lg_ref, ng_ref, *rest, layer, has_s0):
    if has_s0:
        s0_ref, o_ref, sfin_ref, q_s, c_s, b_s, qb_s, kd_s, dec_s, of_s, ob_s, st_s = rest
    else:
        o_ref, sfin_ref, q_s, c_s, b_s, qb_s, kd_s, dec_s, of_s, ob_s, st_s = rest
    n = zq_ref.shape[0]
    c = HG_CHUNK
    n_chunks = n // c
    blk = HG_BLOCK
    cpb = blk // c
    sh = c.bit_length() - 1

    logits = lg_ref[...]
    depth = logits.shape[0]
    mx = logits[0]
    for i in range(1, depth):
        mx = jnp.maximum(mx, logits[i])
    ex = [jnp.exp(logits[i] - mx) for i in range(depth)]
    tot = ex[0]
    for i in range(1, depth):
        tot = tot + ex[i]
    lb = jnp.zeros_like(mx)
    for i in range(1, layer + 1):
        lb = lb + ex[i] / tot
    log_lb = jnp.log(lb)
    log1m_lb = jnp.log1p(-lb)

    if has_s0:
        st_s[...] = s0_ref[...]
    else:
        st_s[...] = jnp.zeros(st_s.shape, F32)

    br = lax.broadcasted_iota(jnp.int32, (blk, blk), 0)
    bc = lax.broadcasted_iota(jnp.int32, (blk, blk), 1)
    same = (br >> sh) == (bc >> sh)
    cum_mat = [(same & (bc <= br)).astype(BF16), (same & (bc >= br)).astype(BF16)]
    tot_mat = same.astype(BF16)

    def gates(i, carry):
        rows = _rows(i, blk)
        q = _silu(zq_ref[rows, :])
        q_s[rows, :] = q
        for d in range(2):
            zf = (zff_ref if d == 0 else zfb_ref)[rows, :]
            lf, log_k = _log_forget_and_key(zf, log_lb[d:d + 1, :], log1m_lb[d:d + 1, :])
            k = jnp.exp(log_k)
            hi = lf.astype(BF16)
            lo = (lf - hi.astype(F32)).astype(BF16)
            b = _dot(cum_mat[d], hi) + _dot(cum_mat[d], lo)
            b_tot = _dot(tot_mat, hi) + _dot(tot_mat, lo)
            b2 = b * LOG2E
            b_s[d, rows, :] = b2
            c_s[d, rows, :] = b2 - log_k * LOG2E
            qb_s[d, rows, :] = (q * jnp.exp(b)).astype(BF16)
            kd_s[d, rows, :] = (k * jnp.exp(b_tot - b)).astype(BF16)
            dec_s[d, rows, :] = jnp.exp(b_tot)
        return carry
    lax.fori_loop(0, n // blk, gates, 0)

    nh = HG_HEADS
    hs = [slice(h * HG_DK, (h + 1) * HG_DK) for h in range(nh)]
    s8 = lax.broadcasted_iota(jnp.int32, (SUBLANE, nh * HG_DK), 0)
    pair = 2 * HG_DK
    pr = lax.broadcasted_iota(jnp.int32, (pair, pair), 0)
    pc = lax.broadcasted_iota(jnp.int32, (pair, pair), 1)
    ones_pair = ((pr >= HG_DK) == (pc >= HG_DK)).astype(BF16)
    sel_r = lax.broadcasted_iota(jnp.int32, (c, c * c), 0)
    sel_c = lax.broadcasted_iota(jnp.int32, (c, c * c), 1)
    sel = ((sel_c >= sel_r * c) & (sel_c < sel_r * c + c)).astype(BF16)
    zero8 = jnp.zeros((SUBLANE, nh * HG_DK), F32)

    def scores_of(d, rows):
        q = q_s[rows, :]
        b2 = b_s[d, rows, :]
        c2 = c_s[d, rows, :]
        halves = [c2[:SUBLANE], c2[SUBLANE:]]
        parts = []
        for t in range(c):
            bt, qt = b2[t:t + 1, :], q[t:t + 1, :]
            own = t // SUBLANE
            row = []
            for half, ch in enumerate(halves):
                if half == own:
                    tt = t - own * SUBLANE
                    vis = (s8 <= tt) if d == 0 else (s8 >= tt)
                    row.append(qt * jnp.exp2(jnp.where(vis, bt - ch, NEG)))
                elif (half < own) == (d == 0):
                    row.append(qt * jnp.exp2(bt - ch))
                else:
                    row.append(zero8)
            parts.append(row)
        p = jnp.concatenate([parts[t][half] for t in range(c) for half in range(2)], axis=0).astype(BF16)
        return jnp.concatenate([_dot(p[:, j * pair:(j + 1) * pair], ones_pair) for j in range(nh * HG_DK // pair)],
                               axis=1)

    def state_step(d, h, rows, ci):
        st = st_s[d, h]
        o_inter = _dot_nt(qb_s[d, rows, hs[h]], st.astype(BF16))
        dec8 = dec_s[d, pl.ds(pl.multiple_of(ci * c, c), SUBLANE), hs[h]]
        st_dec = (st.reshape(HG_DK // SUBLANE, SUBLANE, HG_DK) * dec8[None]).reshape(HG_DK, HG_DK)
        st_s[d, h] = st_dec + _dot_tn(zi_ref[rows, hs[h]].astype(BF16), kd_s[d, rows, hs[h]])
        return o_inter

    per_trip = HG_CHUNKS_PER_TRIP

    def body(i, carry):
        cis = [[per_trip * i + u for u in range(per_trip)],
               [n_chunks - 1 - per_trip * i - u for u in range(per_trip)]]
        rows = [[_rows(ci, c) for ci in cis[d]] for d in range(2)]
        scores = [[scores_of(d, rows[d][u]) for u in range(per_trip)] for d in range(2)]
        o_inter = [[[state_step(d, h, rows[d][u], cis[d][u]) for h in range(nh)] for u in range(per_trip)]
                   for d in range(2)]
        for d in range(2):
            for u in range(per_trip):
                v_rep = jnp.concatenate([zi_ref[rows[d][u], :]] * c, axis=0)
                o_intra = _dot(sel, (scores[d][u] * v_rep).astype(BF16))
                for h in range(nh):
                    (of_s if d == 0 else ob_s)[rows[d][u], hs[h]] = o_intra[:, hs[h]] + o_inter[d][u][h]
        return carry
    lax.fori_loop(0, n_chunks // per_trip, body, 0)

    sfin_ref[...] = st_s[...]

    def fin(r, carry):
        rows = _rows(r, EW_CHUNK)
        for h in range(HG_HEADS):
            cols = slice(h * HG_DK, (h + 1) * HG_DK)
            o = of_s[rows, cols] + ob_s[rows, cols]
            o = o * lax.rsqrt(jnp.mean(o * o, axis=-1, keepdims=True) + EPS) * ng_ref[h:h + 1, :]
            o_ref[rows, cols] = (o * _silu(zo_ref[rows, cols])).astype(o_ref.dtype)
        return carry
    lax.fori_loop(0, n // EW_CHUNK, fin, 0)


def _hgrn_call(z, lb_logits, hgrn_norm_g, s0_t, *, layer, n_seq, n_tok):
    w = HG_HEADS * HG_DK
    depth = lb_logits.shape[0]
    zspec = lambda k: pl.BlockSpec((n_tok, w), lambda s: (s, k))
    in_specs = [zspec(0), zspec(1), zspec(2), zspec(3), zspec(4),
                pl.BlockSpec((depth, 2, w), lambda s: (0, 0, 0)),
                pl.BlockSpec((None, HG_HEADS, HG_DK), lambda s: (layer, 0, 0))]
    args = [z, z, z, z, z, lb_logits, hgrn_norm_g]
    st_spec = pl.BlockSpec((None, 2, HG_HEADS, HG_DK, HG_DK), lambda s: (s, 0, 0, 0, 0))
    if s0_t is not None:
        in_specs.append(st_spec)
        args.append(s0_t)
    return pl.pallas_call(
        functools.partial(_hgrn_kernel, layer=layer, has_s0=s0_t is not None),
        grid=(n_seq,),
        in_specs=in_specs,
        out_specs=[pl.BlockSpec((n_tok, w), lambda s: (s, 0)), st_spec],
        out_shape=[jax.ShapeDtypeStruct((n_seq * n_tok, w), BF16),
                   jax.ShapeDtypeStruct((n_seq, 2, HG_HEADS, HG_DK, HG_DK), F32)],
        scratch_shapes=[pltpu.VMEM((n_tok, w), F32),
                        pltpu.VMEM((2, n_tok, w), F32),
                        pltpu.VMEM((2, n_tok, w), F32),
                        pltpu.VMEM((2, n_tok, w), BF16),
                        pltpu.VMEM((2, n_tok, w), BF16),
                        pltpu.VMEM((2, n_tok, w), F32),
                        pltpu.VMEM((n_tok, w), F32), pltpu.VMEM((n_tok, w), F32),
                        pltpu.VMEM((2, HG_HEADS, HG_DK, HG_DK), F32)],
        compiler_params=_params(("parallel",)),
    )(*args)


def _ctx_attn_kernel(q0_ref, q1_ref, k0_ref, k1_ref, v0_ref, v1_ref, o_ref, ko_ref, vo_ref, *, scale):
    q_refs, k_refs, v_refs = (q0_ref, q1_ref), (k0_ref, k1_ref), (v0_ref, v1_ref)
    half = q0_ref.shape[1]
    heads_per_block = half // NA_HEAD_DIM
    ko_ref[:, :half] = k0_ref[...]
    ko_ref[:, half:] = k1_ref[...]
    vo_ref[:, :half] = v0_ref[...]
    vo_ref[:, half:] = v1_ref[...]

    def head(refs, h):
        blk, off = divmod(h, heads_per_block)
        return refs[blk][:, off * NA_HEAD_DIM:(off + 1) * NA_HEAD_DIM].astype(BF16)

    scores = [_dot_nt(head(q_refs, h), head(k_refs, h)) * scale for h in range(NA_HEADS)]
    probs = []
    for s in scores:
        p = jnp.exp(s - jnp.max(s, axis=-1, keepdims=True))
        probs.append((p / jnp.sum(p, axis=-1, keepdims=True)).astype(BF16))
    for h in range(NA_HEADS):
        o = _dot(probs[h], head(v_refs, h))
        o_ref[:, h * NA_HEAD_DIM:(h + 1) * NA_HEAD_DIM] = o.astype(o_ref.dtype)


def _ctx_attn_call(z, *, n_seq, n_tok, col0):
    w = NA_HEADS * NA_HEAD_DIM
    half = w // 2
    zspec = lambda k: pl.BlockSpec((n_tok, half), lambda s: (s, col0 // half + k))
    return pl.pallas_call(
        functools.partial(_ctx_attn_kernel, scale=NA_HEAD_DIM ** -0.5),
        grid=(n_seq,),
        in_specs=[zspec(k) for k in range(6)],
        out_specs=[pl.BlockSpec((n_tok, w), lambda s: (s, 0))] * 3,
        out_shape=[jax.ShapeDtypeStruct((n_seq * n_tok, w), BF16),
                   jax.ShapeDtypeStruct((n_seq * n_tok, w), F32),
                   jax.ShapeDtypeStruct((n_seq * n_tok, w), F32)],
        compiler_params=_params(("parallel",)),
    )(z, z, z, z, z, z)


def _na_kernel(q_ref, k_ref, v_ref, kc_ref, vc_ref, bias_ref, o_ref, *, scale, rows, kh):
    q = q_ref[...].astype(BF16)
    k = k_ref[...].astype(BF16)
    v = v_ref[...].astype(BF16)
    qrow = lambda r: slice(r * GRID_W, (r + 1) * GRID_W)
    krows = lambda r: slice(min(max(r - kh // 2, 0), rows - kh) * GRID_W,
                            (min(max(r - kh // 2, 0), rows - kh) + kh) * GRID_W)
    s_ctx = _dot_nt(q, kc_ref[...].astype(BF16)) * scale
    s_loc = [_dot_nt(q[qrow(r)], k[krows(r)]) * scale + bias_ref[r] for r in range(rows)]
    p_loc, p_ctx = [], []
    for r in range(rows):
        sc = s_ctx[qrow(r)]
        m = jnp.maximum(jnp.max(s_loc[r], axis=-1, keepdims=True), jnp.max(sc, axis=-1, keepdims=True))
        el = jnp.exp(s_loc[r] - m)
        ec = jnp.exp(sc - m)
        den = jnp.sum(el, axis=-1, keepdims=True) + jnp.sum(ec, axis=-1, keepdims=True)
        p_loc.append((el / den).astype(BF16))
        p_ctx.append((ec / den).astype(BF16))
    o_ctx = _dot(jnp.concatenate(p_ctx, axis=0), vc_ref[...].astype(BF16))
    for r in range(rows):
        o = _dot(p_loc[r], v[krows(r)]) + o_ctx[qrow(r)]
        o_ref[qrow(r), :] = o.astype(o_ref.dtype)


def _na_bias(rpb, rows, kh):
    heads = rpb.shape[0]
    qcol = jnp.arange(GRID_W)
    kcol = jnp.arange(GRID_W)
    win_start = jnp.clip(qcol - NA_KW // 2, 0, GRID_W - NA_KW)
    in_win = (kcol[None, :] >= win_start[:, None]) & (kcol[None, :] < win_start[:, None] + NA_KW)
    lo = GRID_W - NA_KW
    pad = jnp.pad(rpb.astype(F32), ((0, 0), (0, 0), (lo, lo)))
    toep = jnp.stack([pad[:, :, GRID_W - 1 - q:2 * GRID_W - 1 - q] for q in range(GRID_W)], axis=2)
    toep = jnp.where(in_win[None, None], toep, NEG)
    per_row = []
    for r in range(rows):
        dy0 = min(max(r - kh // 2, 0), rows - kh) - r + NA_KH - 1
        per_row.append(toep[:, dy0:dy0 + kh])
    bias = jnp.stack(per_row, axis=1)
    return jnp.transpose(bias, (0, 1, 3, 2, 4)).reshape(heads, rows, GRID_W, kh * GRID_W)


def _na_call(z, cache_k, cache_v, bias, *, layer, n_seq, n_tok, col_block0):
    rows = n_tok // GRID_W
    kh = min(NA_KH, rows)
    dh = NA_HEAD_DIM
    past = cache_k.shape[2]
    zspec = lambda k: pl.BlockSpec((n_tok, dh), lambda b, h: (b, col_block0 + k * NA_HEADS + h))
    cspec = pl.BlockSpec((None, None, past, dh), lambda b, h: (b, layer, 0, h))
    return pl.pallas_call(
        functools.partial(_na_kernel, scale=dh ** -0.5, rows=rows, kh=kh),
        grid=(n_seq, NA_HEADS),
        in_specs=[zspec(0), zspec(1), zspec(2), cspec, cspec,
                  pl.BlockSpec((None, rows, GRID_W, kh * GRID_W), lambda b, h: (h, 0, 0, 0))],
        out_specs=pl.BlockSpec((n_tok, dh), lambda b, h: (b, h)),
        out_shape=jax.ShapeDtypeStruct((n_seq * n_tok, NA_HEADS * dh), BF16),
        compiler_params=_params(("parallel", "parallel")),
    )(z, z, z, cache_k, cache_v, bias)


def _lru_kernel(zx_ref, zg_ref, cw_ref, cb_ref, wa_ref, ba_ref, wx_ref, bx_ref, lam_ref, *rest,
                n_seq, n_tok, has_h0):
    if has_h0:
        h0_ref, y_ref, hfin_ref, a_s, u_s, y_s = rest
    else:
        y_ref, hfin_ref, a_s, u_s, y_s = rest
    tpos = lax.broadcasted_iota(jnp.int32, (n_tok, LANE), 0)
    gpos = tpos & (SUBLANE - 1)
    neg_lam = -lam_ref[...]
    softplus = jnp.maximum(neg_lam, 0.0) + jnp.log1p(jnp.exp(-jnp.abs(neg_lam)))
    left = LRU_CONV // 2

    def gates(g, carry):
        rows = _rows(g, n_tok)
        zx = zx_ref[rows, :]
        x = cb_ref[...] + cw_ref[left:left + 1, :] * zx
        for j in range(LRU_CONV):
            off = j - left
            if off == 0:
                continue
            shifted = pltpu.roll(zx, (-off) % n_tok, 0)
            valid = (tpos + off >= 0) & (tpos + off < n_tok)
            x = x + cw_ref[j:j + 1, :] * jnp.where(valid, shifted, 0.0)
        xb = x.astype(BF16)
        for d in range(2):
            r_gate = jax.nn.sigmoid(_dot(xb, wa_ref[d].astype(BF16)) + ba_ref[d:d + 1, :])
            i_gate = jax.nn.sigmoid(_dot(xb, wx_ref[d].astype(BF16)) + bx_ref[d:d + 1, :])
            log_a = (-LRU_C * r_gate) * softplus[d:d + 1, :]
            a = jnp.exp(log_a)
            u = jnp.sqrt(-_expm1(2.0 * log_a)) * (i_gate * x)
            sh = 1
            while sh < SUBLANE:
                if d == 0:
                    inside = gpos >= sh
                    amt = sh
                else:
                    inside = gpos < SUBLANE - sh
                    amt = n_tok - sh
                a_prev = jnp.where(inside, pltpu.roll(a, amt, 0), 1.0)
                u_prev = jnp.where(inside, pltpu.roll(u, amt, 0), 0.0)
                u = u + a * u_prev
                a = a * a_prev
                sh *= 2
            a_s[d, rows, :] = a
            u_s[d, rows, :] = u
        return carry
    lax.fori_loop(0, n_seq, gates, 0)

    n_groups = n_tok // SUBLANE
    edge = (SUBLANE - 1, 0)

    def bcast(row):
        return jnp.broadcast_to(row, (SUBLANE, LANE))

    if has_h0:
        h_init = tuple(bcast(h0_ref[d, g:g + 1, :]) for g in range(n_seq) for d in range(2))
    else:
        h_init = tuple(jnp.zeros((SUBLANE, LANE), F32) for _ in range(2 * n_seq))

    def step(j, carry):
        out = []
        for g in range(n_seq):
            for d in range(2):
                grp = j if d == 0 else n_groups - 1 - j
                rows = pl.ds(pl.multiple_of(g * n_tok + grp * SUBLANE, SUBLANE), SUBLANE)
                h = u_s[d, rows, :] + a_s[d, rows, :] * carry[2 * g + d]
                y_s[d, rows, :] = h
                out.append(bcast(h[edge[d]:edge[d] + 1, :]))
        return tuple(out)
    h_last = lax.fori_loop(0, n_groups, step, h_init)
    for g in range(n_seq):
        for d in range(2):
            hfin_ref[d, g:g + 1, :] = h_last[2 * g + d][0:1, :]

    def fin(g, carry):
        rows = _rows(g, n_tok)
        y = (y_s[0, rows, :] + y_s[1, rows, :]) * jax.nn.gelu(zg_ref[rows, :], approximate=True)
        y_ref[rows, :] = y.astype(y_ref.dtype)
        return carry
    lax.fori_loop(0, n_seq, fin, 0)


def _block_diag_pairs(w):
    depth, nd, nb, bw, _ = w.shape
    w = w.reshape(depth, nd, nb // 2, 2, bw, bw)
    eye = jnp.eye(2, dtype=w.dtype)
    out = w[:, :, :, :, :, None, :] * eye[None, None, None, :, None, :, None]
    return out.reshape(depth, nd, nb // 2, 2 * bw, 2 * bw)


def _lru_call(z, conv_w, conv_b, wa_bd, b_a, wx_bd, b_x, lam, h0_t, *, layer, n_seq, n_tok, col_block0):
    w = conv_b.shape[1]
    n_cb = w // LANE
    rows = n_seq * n_tok
    zspec = lambda k: pl.BlockSpec((rows, LANE), lambda cb: (0, col_block0 + k * n_cb + cb))
    vec2 = pl.BlockSpec((None, 2, LANE), lambda cb: (layer, 0, cb))
    wspec = pl.BlockSpec((None, 2, None, LANE, LANE), lambda cb: (layer, 0, cb, 0, 0))
    hspec = pl.BlockSpec((2, n_seq, LANE), lambda cb: (0, 0, cb))
    in_specs = [zspec(0), zspec(1),
                pl.BlockSpec((None, LRU_CONV, LANE), lambda cb: (layer, 0, cb)),
                pl.BlockSpec((None, 1, LANE), lambda cb: (layer, 0, cb)),
                wspec, vec2, wspec, vec2, vec2]
    args = [z, z, conv_w, conv_b.reshape(conv_b.shape[0], 1, w), wa_bd, b_a, wx_bd, b_x, lam]
    if h0_t is not None:
        in_specs.append(hspec)
        args.append(h0_t)
    return pl.pallas_call(
        functools.partial(_lru_kernel, n_seq=n_seq, n_tok=n_tok, has_h0=h0_t is not None),
        grid=(n_cb,),
        in_specs=in_specs,
        out_specs=[pl.BlockSpec((rows, LANE), lambda cb: (0, cb)), hspec],
        out_shape=[jax.ShapeDtypeStruct((rows, w), BF16), jax.ShapeDtypeStruct((2, n_seq, w), F32)],
        scratch_shapes=[pltpu.VMEM((2, rows, LANE), F32)] * 3,
        compiler_params=_params(("parallel",)),
    )(*args)


MERGE_GATE_BLOCK = 512


def _cast_rows(w_ref, w_s):
    def body(r, carry):
        sl = _rows(r, ROW_CHUNK)
        w_s[sl, :] = w_ref[sl, :].astype(BF16)
        return carry
    lax.fori_loop(0, w_ref.shape[0] // ROW_CHUNK, body, 0)


def _branch_sum_kernel(oa_ref, ob_ref, oc_ref, *rest, n_gate_blocks):
    gate_refs = rest[:3 * n_gate_blocks]
    wa_ref, wb_ref, wc_ref, m_ref, wa_s, wb_s, wc_s = rest[3 * n_gate_blocks:]

    @pl.when(pl.program_id(0) == 0)
    def _():
        _cast_rows(wa_ref, wa_s)
        _cast_rows(wb_ref, wb_s)
        _cast_rows(wc_ref, wc_s)

    gw = MERGE_GATE_BLOCK
    for n in range(n_gate_blocks):
        cols = slice(n * gw, (n + 1) * gw)
        m = None
        for k, (o_ref, w_s) in enumerate(((oa_ref, wa_s), (ob_ref, wb_s), (oc_ref, wc_s))):
            term = jax.nn.sigmoid(gate_refs[k * n_gate_blocks + n][...]) * _dot(o_ref[...], w_s[:, cols])
            m = term if m is None else m + term
        m_ref[:, cols] = m.astype(m_ref.dtype)


def _out_proj_kernel(x_ref, mods_ref, m_ref, wo_ref, o_ref, wo_s):
    @pl.when(pl.program_id(0) == 0)
    def _():
        _cast_rows(wo_ref, wo_s)

    gate = mods_ref[5:6, :]
    for r in range(x_ref.shape[0] // ROW_CHUNK):
        sl = slice(r * ROW_CHUNK, (r + 1) * ROW_CHUNK)
        o_ref[sl, :] = x_ref[sl, :] + gate * _dot(m_ref[sl, :], wo_s[...])


def _merge_call(x, mods, o_a, o_b, o_c, z, w_proj_a, w_proj_b, w_proj_c, w_out, *, layer, group0,
                rows_per_group, gate_col0):
    t, d = x.shape
    gw = MERGE_GATE_BLOCK
    nb = d // gw
    g0 = gate_col0 // gw
    tm1, tm2 = 256, 512
    widths = (o_a.shape[1], o_b.shape[1], o_c.shape[1])
    gate_specs = [pl.BlockSpec((tm1, gw), lambda i, col=g0 + k * nb + n: (i, col))
                  for k in range(3) for n in range(nb)]
    resident = lambda w: pl.BlockSpec((None, w, d), lambda i: (layer, 0, 0), pipeline_mode=pl.Buffered(1))
    m = pl.pallas_call(
        functools.partial(_branch_sum_kernel, n_gate_blocks=nb),
        grid=(t // tm1,),
        in_specs=([pl.BlockSpec((tm1, w), lambda i: (i, 0)) for w in widths] + gate_specs
                  + [resident(w) for w in widths]),
        out_specs=pl.BlockSpec((tm1, d), lambda i: (i, 0)),
        out_shape=jax.ShapeDtypeStruct((t, d), BF16),
        scratch_shapes=[pltpu.VMEM((w, d), BF16) for w in widths],
        compiler_params=_params(("arbitrary",)),
    )(o_a, o_b, o_c, *([z] * (3 * nb)), w_proj_a, w_proj_b, w_proj_c)
    group = _group_index(tm2, group0, rows_per_group)
    return pl.pallas_call(
        _out_proj_kernel,
        grid=(t // tm2,),
        in_specs=[
            pl.BlockSpec((tm2, d), lambda i: (i, 0)),
            pl.BlockSpec((None, None, N_MOD, d), lambda i: (layer, group(i), 0, 0)),
            pl.BlockSpec((tm2, d), lambda i: (i, 0)),
            resident(d),
        ],
        out_specs=pl.BlockSpec((tm2, d), lambda i: (i, 0)),
        out_shape=jax.ShapeDtypeStruct((t, d), F32),
        scratch_shapes=[pltpu.VMEM((d, d), BF16)],
        compiler_params=_params(("arbitrary",)),
    )(x, mods, m, w_out)


def kernel(x_prompt, x_sample, cache_na_k, cache_na_v, state_hgrn, state_lru, c, c_ctx, mod_w, mod_b, norm_g, ffn1_w_up, ffn1_w_down, ffn2_w_up, ffn2_w_down, w_in, hgrn_lb_logits, hgrn_norm_g, na_rpb, lru_conv_w, lru_conv_b, lru_w_a, lru_b_a, lru_w_x, lru_b_x, lru_lambda, w_proj_a, w_proj_b, w_proj_c, w_out, final_norm_g):
    b_ctx, n_ctx, d = x_prompt.shape
    b_lat, n_lat, _ = x_sample.shape
    depth = mod_w.shape[0]
    t_ctx, t_lat = b_ctx * n_ctx, b_lat * n_lat
    hg_w = HG_HEADS * HG_DK
    na_w = NA_HEADS * NA_HEAD_DIM
    lru_w = lru_conv_b.shape[1]
    na_col0 = 5 * hg_w
    lru_col0 = na_col0 + 3 * na_w
    gate_col0 = lru_col0 + 2 * lru_w

    ctx = dict(group0=0, rows_per_group=t_ctx)
    lat = dict(group0=1, rows_per_group=n_lat)
    xc = x_prompt.reshape(t_ctx, d)
    xl = x_sample.reshape(t_lat, d)
    cond = jnp.concatenate([c_ctx[None], c, jnp.zeros((N_COND_ROWS - 1 - b_lat, d), F32)], axis=0)
    mods = _mods_call(cond, mod_w, mod_b).reshape(depth, N_COND_ROWS, N_MOD, d)

    rows = n_lat // GRID_W
    kh = min(NA_KH, rows)
    wa_bd = _block_diag_pairs(lru_w_a)
    wx_bd = _block_diag_pairs(lru_w_x)
    cache_k = cache_na_k.reshape(b_lat, depth, cache_na_k.shape[2], na_w)
    cache_v = cache_na_v.reshape(b_lat, depth, cache_na_v.shape[2], na_w)
    s0_t = jnp.swapaxes(state_hgrn, -1, -2)
    h0_t = jnp.transpose(state_lru, (1, 2, 0, 3))

    ks, vs, hgs, lrus = [], [], [], []
    for l in range(depth):
        last = l == depth - 1
        final_g = final_norm_g if last else None
        lru_args = (lru_conv_w, lru_conv_b, wa_bd, lru_b_a, wx_bd, lru_b_x, lru_lambda)
        merge_w = (w_proj_a, w_proj_b, w_proj_c, w_out)

        xc = _ffn_call(xc, mods, norm_g, ffn1_w_up, ffn1_w_down, layer=l, slot=0, **ctx)
        zc = _inproj_call(xc, mods, norm_g, w_in, layer=l, **ctx)
        oa, s_c = _hgrn_call(zc, hgrn_lb_logits, hgrn_norm_g, None, layer=l, n_seq=b_ctx, n_tok=n_ctx)
        ob, k_l, v_l = _ctx_attn_call(zc, n_seq=b_ctx, n_tok=n_ctx, col0=na_col0)
        oc, h_c = _lru_call(zc, *lru_args, None, layer=l, n_seq=b_ctx, n_tok=n_ctx, col_block0=lru_col0 // LANE)
        xc = _merge_call(xc, mods, oa, ob, oc, zc, *merge_w, layer=l, gate_col0=gate_col0, **ctx)
        xc = _ffn_call(xc, mods, norm_g, ffn2_w_up, ffn2_w_down, layer=l, slot=2, final_g=final_g, **ctx)

        xl = _ffn_call(xl, mods, norm_g, ffn1_w_up, ffn1_w_down, layer=l, slot=0, **lat)
        zl = _inproj_call(xl, mods, norm_g, w_in, layer=l, **lat)
        oa, _ = _hgrn_call(zl, hgrn_lb_logits, hgrn_norm_g, s0_t[:, l], layer=l, n_seq=b_lat, n_tok=n_lat)
        bias = _na_bias(na_rpb[l], rows, kh)
        ob = _na_call(zl, cache_k, cache_v, bias, layer=l, n_seq=b_lat, n_tok=n_lat,
                      col_block0=na_col0 // NA_HEAD_DIM)
        oc, _ = _lru_call(zl, *lru_args, h0_t[l], layer=l, n_seq=b_lat, n_tok=n_lat, col_block0=lru_col0 // LANE)
        xl = _merge_call(xl, mods, oa, ob, oc, zl, *merge_w, layer=l, gate_col0=gate_col0, **lat)
        xl = _ffn_call(xl, mods, norm_g, ffn2_w_up, ffn2_w_down, layer=l, slot=2, final_g=final_g, **lat)

        ks.append(k_l.reshape(b_ctx, n_ctx, NA_HEADS, NA_HEAD_DIM))
        vs.append(v_l.reshape(b_ctx, n_ctx, NA_HEADS, NA_HEAD_DIM))
        hgs.append(jnp.swapaxes(s_c, -1, -2))
        lrus.append(jnp.transpose(h_c, (1, 0, 2)))

    return (xc.reshape(b_ctx, n_ctx, d), xl.reshape(b_lat, n_lat, d), jnp.stack(ks, axis=1),
            jnp.stack(vs, axis=1), jnp.stack(hgs, axis=1), jnp.stack(lrus, axis=1))
```

```python
import functools

import jax
import jax.numpy as jnp
from jax import lax
from jax.experimental import pallas as pl
from jax.experimental.pallas import tpu as pltpu

F32 = jnp.float32
BF16 = jnp.bfloat16

EPS = 1e-6
NEG = -1e30
LOG2E = 1.4426950408889634
N_MOD = 9
N_COND_ROWS = 8
HG_HEADS = 4
HG_DK = 128
HG_CHUNK = 16
HG_BLOCK = 128
HG_CHUNKS_PER_TRIP = 1
NA_HEADS = 8
NA_HEAD_DIM = 128
NA_KH = 8
NA_KW = 16
GRID_W = 64
LRU_CONV = 4
LRU_C = 8.0
LANE = 128
SUBLANE = 8
VMEM_LIMIT = 56 * 1024 * 1024
FFN_VMEM_LIMIT = 60 * 1024 * 1024


def _params(sem, vmem=VMEM_LIMIT):
    return pltpu.CompilerParams(dimension_semantics=sem, vmem_limit_bytes=vmem)


def _silu(x):
    return x * jax.nn.sigmoid(x)


def _expm1(x):
    u = jnp.exp(x)
    um1 = u - 1.0
    y = um1 * x / jnp.where(u == 1.0, 1.0, jnp.log(u))
    return jnp.where(u == 1.0, x, jnp.where(um1 == -1.0, -1.0, y))


def _dot(a, b):
    return jnp.dot(a, b, preferred_element_type=F32)


def _dot_nt(a, b):
    return lax.dot_general(a, b, (((1,), (1,)), ((), ())), preferred_element_type=F32)


def _dot_tn(a, b):
    return lax.dot_general(a, b, (((0,), (0,)), ((), ())), preferred_element_type=F32)


def _group_index(tm, group0, rows_per_group):
    tiles_per_group = rows_per_group // tm
    return lambda i: group0 + i // tiles_per_group


def _mods_kernel(c_ref, w_ref, b_ref, o_ref):
    s = _silu(c_ref[...]).astype(BF16)
    o_ref[...] = _dot(s, w_ref[...].astype(BF16)) + b_ref[...]


def _mods_call(cond, mod_w, mod_b):
    depth, d, n = mod_w.shape
    tn = 1024
    return pl.pallas_call(
        _mods_kernel,
        grid=(depth, n // tn),
        in_specs=[
            pl.BlockSpec((N_COND_ROWS, d), lambda l, j: (0, 0)),
            pl.BlockSpec((None, d, tn), lambda l, j: (l, 0, j)),
            pl.BlockSpec((None, 1, tn), lambda l, j: (l, 0, j)),
        ],
        out_specs=pl.BlockSpec((None, N_COND_ROWS, tn), lambda l, j: (l, 0, j)),
        out_shape=jax.ShapeDtypeStruct((depth, N_COND_ROWS, n), F32),
        compiler_params=_params(("parallel", "parallel")),
    )(cond, mod_w, mod_b.reshape(depth, 1, n))


ROW_CHUNK = 256
EW_CHUNK = 64


def _rows(r, n):
    return pl.ds(pl.multiple_of(r * n, n), n)


def _modulated_norm_tile(h_ref, x_ref, g_ref, shift, scale, rs_ref, zero_ref=None):
    tm = x_ref.shape[0]

    def stats(r, carry):
        sl = _rows(r, EW_CHUNK)
        x = x_ref[sl, :]
        rs_ref[sl, :] = lax.rsqrt(jnp.mean(x * x, axis=-1, keepdims=True) + EPS)
        return carry
    lax.fori_loop(0, tm // EW_CHUNK, stats, 0, unroll=4)

    gain = g_ref[...] * (1.0 + scale)

    def apply(r, carry):
        sl = _rows(r, EW_CHUNK)
        h_ref[sl, :] = ((x_ref[sl, :] * rs_ref[sl, :]) * gain + shift).astype(BF16)
        if zero_ref is not None:
            zero_ref[sl, :] = jnp.zeros((EW_CHUNK, zero_ref.shape[1]), F32)
        return carry
    lax.fori_loop(0, tm // EW_CHUNK, apply, 0)


def _ffn_kernel(x_ref, mods_ref, g_ref, wa_ref, wu_ref, wd_ref, *rest, slot, final):
    if final:
        fg_ref, o_ref, h_ref, rs_ref = rest
    else:
        o_ref, h_ref, rs_ref = rest
    j = pl.program_id(1)
    tm = x_ref.shape[0]
    shift = mods_ref[3 * slot:3 * slot + 1, :]
    scale = mods_ref[3 * slot + 1:3 * slot + 2, :]
    gate = mods_ref[3 * slot + 2:3 * slot + 3, :]

    @pl.when(j == 0)
    def _():
        _modulated_norm_tile(h_ref, x_ref, g_ref, shift, scale, rs_ref, zero_ref=o_ref)

    for r in range(tm // ROW_CHUNK):
        sl = slice(r * ROW_CHUNK, (r + 1) * ROW_CHUNK)
        h = h_ref[sl, :]
        a = _dot(h, wa_ref[...].astype(BF16))
        u = _dot(h, wu_ref[...].astype(BF16))
        act = (_silu(a) * u).astype(BF16)
        o_ref[sl, :] += _dot(act, wd_ref[...].astype(BF16))

    @pl.when(j == pl.num_programs(1) - 1)
    def _():
        def body(r, carry):
            sl = _rows(r, EW_CHUNK)
            y = x_ref[sl, :] + (0.5 * gate) * o_ref[sl, :]
            if final:
                y = y * lax.rsqrt(jnp.mean(y * y, axis=-1, keepdims=True) + EPS) * fg_ref[...]
            o_ref[sl, :] = y
            return carry
        lax.fori_loop(0, tm // EW_CHUNK, body, 0)


def _ffn_call(x, mods, norm_g, w_up, w_down, *, layer, slot, group0, rows_per_group, final_g=None):
    t, d = x.shape
    f = w_down.shape[1]
    tm, tf = 1024, 512
    nf = f // tf
    group = _group_index(tm, group0, rows_per_group)
    norm_slot = slot
    in_specs = [
        pl.BlockSpec((tm, d), lambda i, j: (i, 0), pipeline_mode=pl.Buffered(1)),
        pl.BlockSpec((None, None, N_MOD, d), lambda i, j: (layer, group(i), 0, 0)),
        pl.BlockSpec((None, None, 1, d), lambda i, j: (layer, norm_slot, 0, 0)),
        pl.BlockSpec((None, d, tf), lambda i, j: (layer, 0, j)),
        pl.BlockSpec((None, d, tf), lambda i, j: (layer, 0, nf + j)),
        pl.BlockSpec((None, tf, d), lambda i, j: (layer, j, 0)),
    ]
    args = [x, mods, norm_g.reshape(norm_g.shape[0], norm_g.shape[1], 1, d), w_up, w_up, w_down]
    if final_g is not None:
        in_specs.append(pl.BlockSpec((1, d), lambda i, j: (0, 0)))
        args.append(final_g.reshape(1, d))
    return pl.pallas_call(
        functools.partial(_ffn_kernel, slot=slot, final=final_g is not None),
        grid=(t // tm, nf),
        in_specs=in_specs,
        out_specs=pl.BlockSpec((tm, d), lambda i, j: (i, 0)),
        out_shape=jax.ShapeDtypeStruct((t, d), F32),
        scratch_shapes=[pltpu.VMEM((tm, d), BF16), pltpu.VMEM((tm, 1), F32)],
        compiler_params=_params(("parallel", "arbitrary"), vmem=FFN_VMEM_LIMIT),
    )(*args)


def _inproj_kernel(x_ref, mods_ref, g_ref, w_ref, z_ref, h_ref, w_s, rs_ref):
    j = pl.program_id(1)
    tm = x_ref.shape[0]
    shift = mods_ref[3:4, :]
    scale = mods_ref[4:5, :]

    @pl.when(j == 0)
    def _():
        _modulated_norm_tile(h_ref, x_ref, g_ref, shift, scale, rs_ref)

    w_s[...] = w_ref[...].astype(BF16)

    for r in range(tm // ROW_CHUNK):
        sl = slice(r * ROW_CHUNK, (r + 1) * ROW_CHUNK)
        z_ref[sl, :] = _dot(h_ref[sl, :], w_s[...])


def _inproj_call(x, mods, norm_g, w_in, *, layer, group0, rows_per_group):
    t, d = x.shape
    n = w_in.shape[2]
    tm, tn = 1024, 1280
    group = _group_index(tm, group0, rows_per_group)
    return pl.pallas_call(
        _inproj_kernel,
        grid=(t // tm, n // tn),
        in_specs=[
            pl.BlockSpec((tm, d), lambda i, j: (i, 0), pipeline_mode=pl.Buffered(1)),
            pl.BlockSpec((None, None, N_MOD, d), lambda i, j: (layer, group(i), 0, 0)),
            pl.BlockSpec((None, None, 1, d), lambda i, j: (layer, 1, 0, 0)),
            pl.BlockSpec((None, d, tn), lambda i, j: (layer, 0, j)),
        ],
        out_specs=pl.BlockSpec((tm, tn), lambda i, j: (i, j)),
        out_shape=jax.ShapeDtypeStruct((t, n), F32),
        scratch_shapes=[pltpu.VMEM((tm, d), BF16), pltpu.VMEM((d, tn), BF16), pltpu.VMEM((tm, 1), F32)],
        compiler_params=_params(("parallel", "arbitrary")),
    )(x, mods, norm_g.reshape(norm_g.shape[0], norm_g.shape[1], 1, d), w_in)


def _log_forget_and_key(zf, log_lb, log1m_lb):
    l1p = jnp.log1p(jnp.exp(-jnp.abs(zf)))
    b = log1m_lb + (jnp.minimum(zf, 0.0) - l1p)
    log_k = log1m_lb + (jnp.minimum(-zf, 0.0) - l1p)
    hi = jnp.maximum(log_lb, b)
    return hi + jnp.log1p(jnp.exp(-jnp.abs(log_lb - b))), log_k


def _hgrn_kernel(zq_ref, zff_ref, zfb_ref, zi_ref, zo_ref, lg_ref, ng_ref, *rest, layer, has_s0):
    if has_s0:
        s0_ref, o_ref, sfin_ref, q_s, c_s, b_s, qb_s, kd_s, dec_s, of_s, ob_s, st_s = rest
    else:
        o_ref, sfin_ref, q_s, c_s, b_s, qb_s, kd_s, dec_s, of_s, ob_s, st_s = rest
    n = zq_ref.shape[0]
    c = HG_CHUNK
    n_chunks = n // c
    blk = HG_BLOCK
    sh = c.bit_length() - 1

    logits = lg_ref[...]
    depth = logits.shape[0]
    mx = logits[0]
    for i in range(1, depth):
        mx = jnp.maximum(mx, logits[i])
    ex = [jnp.exp(logits[i] - mx) for i in range(depth)]
    tot = ex[0]
    for i in range(1, depth):
        tot = tot + ex[i]
    lb = jnp.zeros_like(mx)
    for i in range(1, layer + 1):
        lb = lb + ex[i] / tot
    log_lb = jnp.log(lb)
    log1m_lb = jnp.log1p(-lb)

    if has_s0:
        st_s[...] = s0_ref[...]
    else:
        st_s[...] = jnp.zeros(st_s.shape, F32)

    br = lax.broadcasted_iota(jnp.int32, (blk, blk), 0)
    bc = lax.broadcasted_iota(jnp.int32, (blk, blk), 1)
    same = (br >> sh) == (bc >> sh)
    cum_mat = [(same & (bc <= br)).astype(BF16), (same & (bc >= br)).astype(BF16)]
    tot_mat = same.astype(BF16)

    def gates(i, carry):
        rows = _rows(i, blk)
        q = _silu(zq_ref[rows, :])
        q_s[rows, :] = q
        for d in range(2):
            zf = (zff_ref if d == 0 else zfb_ref)[rows, :]
            lf, log_k = _log_forget_and_key(zf, log_lb[d:d + 1, :], log1m_lb[d:d + 1, :])
            k = jnp.exp(log_k)
            hi = lf.astype(BF16)
            lo = (lf - hi.astype(F32)).astype(BF16)
            b = _dot(cum_mat[d], hi) + _dot(cum_mat[d], lo)
            b_tot = _dot(tot_mat, hi) + _dot(tot_mat, lo)
            b2 = b * LOG2E
            b_s[d, rows, :] = b2
            c_s[d, rows, :] = b2 - log_k * LOG2E
            qb_s[d, rows, :] = (q * jnp.exp(b)).astype(BF16)
            kd_s[d, rows, :] = (k * jnp.exp(b_tot - b)).astype(BF16)
            dec_s[d, rows, :] = jnp.exp(b_tot)
        return carry
    lax.fori_loop(0, n // blk, gates, 0)

    nh = HG_HEADS
    hs = [slice(h * HG_DK, (h + 1) * HG_DK) for h in range(nh)]
    s8 = lax.broadcasted_iota(jnp.int32, (SUBLANE, nh * HG_DK), 0)
    pair = 2 * HG_DK
    pr = lax.broadcasted_iota(jnp.int32, (pair, pair), 0)
    pc = lax.broadcasted_iota(jnp.int32, (pair, pair), 1)
    ones_pair = ((pr >= HG_DK) == (pc >= HG_DK)).astype(BF16)
    sel_r = lax.broadcasted_iota(jnp.int32, (c, c * c), 0)
    sel_c = lax.broadcasted_iota(jnp.int32, (c, c * c), 1)
    sel = ((sel_c >= sel_r * c) & (sel_c < sel_r * c + c)).astype(BF16)
    zero8 = jnp.zeros((SUBLANE, nh * HG_DK), F32)

    def scores_of(d, rows):
        q = q_s[rows, :]
        b2 = b_s[d, rows, :]
        c2 = c_s[d, rows, :]
        halves = [c2[:SUBLANE], c2[SUBLANE:]]
        parts = []
        for t in range(c):
            bt, qt = b2[t:t + 1, :], q[t:t + 1, :]
            own = t // SUBLANE
            row = []
            for half, ch in enumerate(halves):
                if half == own:
                    tt = t - own * SUBLANE
                    vis = (s8 <= tt) if d == 0 else (s8 >= tt)
                    row.append(qt * jnp.exp2(jnp.where(vis, bt - ch, NEG)))
                elif (half < own) == (d == 0):
                    row.append(qt * jnp.exp2(bt - ch))
                else:
                    row.append(zero8)
            parts.append(row)
        p = jnp.concatenate([parts[t][half] for t in range(c) for half in range(2)], axis=0).astype(BF16)
        return jnp.concatenate([_dot(p[:, j * pair:(j + 1) * pair], ones_pair) for j in range(nh * HG_DK // pair)],
                               axis=1)

    def state_step(d, h, rows, ci):
        st = st_s[d, h]
        o_inter = _dot_nt(qb_s[d, rows, hs[h]], st.astype(BF16))
        dec8 = dec_s[d, pl.ds(pl.multiple_of(ci * c, c), SUBLANE), hs[h]]
        st_dec = (st.reshape(HG_DK // SUBLANE, SUBLANE, HG_DK) * dec8[None]).reshape(HG_DK, HG_DK)
        st_s[d, h] = st_dec + _dot_tn(zi_ref[rows, hs[h]].astype(BF16), kd_s[d, rows, hs[h]])
        return o_inter

    per_trip = HG_CHUNKS_PER_TRIP

    def body(i, carry):
        cis = [[per_trip * i + u for u in range(per_trip)],
               [n_chunks - 1 - per_trip * i - u for u in range(per_trip)]]
        rows = [[_rows(ci, c) for ci in cis[d]] for d in range(2)]
        scores = [[scores_of(d, rows[d][u]) for u in range(per_trip)] for d in range(2)]
        o_inter = [[[state_step(d, h, rows[d][u], cis[d][u]) for h in range(nh)] for u in range(per_trip)]
                   for d in range(2)]
        for d in range(2):
            for u in range(per_trip):
                v_rep = jnp.concatenate([zi_ref[rows[d][u], :]] * c, axis=0)
                o_intra = _dot(sel, (scores[d][u] * v_rep).astype(BF16))
                for h in range(nh):
                    (of_s if d == 0 else ob_s)[rows[d][u], hs[h]] = o_intra[:, hs[h]] + o_inter[d][u][h]
        return carry
    lax.fori_loop(0, n_chunks // per_trip, body, 0)

    for d in range(2):
        for h in range(nh):
            sfin_ref[d, h] = st_s[d, h].T

    def fin(r, carry):
        rows = _rows(r, EW_CHUNK)
        for h in range(HG_HEADS):
            cols = slice(h * HG_DK, (h + 1) * HG_DK)
            o = of_s[rows, cols] + ob_s[rows, cols]
            o = o * lax.rsqrt(jnp.mean(o * o, axis=-1, keepdims=True) + EPS) * ng_ref[h:h + 1, :]
            o_ref[rows, cols] = (o * _silu(zo_ref[rows, cols])).astype(o_ref.dtype)
        return carry
    lax.fori_loop(0, n // EW_CHUNK, fin, 0)


def _hgrn_call(z, lb_logits, hgrn_norm_g, s0_t, *, layer, n_seq, n_tok):
    w = HG_HEADS * HG_DK
    depth = lb_logits.shape[0]
    zspec = lambda k: pl.BlockSpec((n_tok, w), lambda s: (s, k))
    in_specs = [zspec(0), zspec(1), zspec(2), zspec(3), zspec(4),
                pl.BlockSpec((depth, 2, w), lambda s: (0, 0, 0)),
                pl.BlockSpec((None, HG_HEADS, HG_DK), lambda s: (layer, 0, 0))]
    args = [z, z, z, z, z, lb_logits, hgrn_norm_g]
    st_spec = pl.BlockSpec((None, 2, HG_HEADS, HG_DK, HG_DK), lambda s: (s, 0, 0, 0, 0))
    if s0_t is not None:
        in_specs.append(st_spec)
        args.append(s0_t)
    return pl.pallas_call(
        functools.partial(_hgrn_kernel, layer=layer, has_s0=s0_t is not None),
        grid=(n_seq,),
        in_specs=in_specs,
        out_specs=[pl.BlockSpec((n_tok, w), lambda s: (s, 0)), st_spec],
        out_shape=[jax.ShapeDtypeStruct((n_seq * n_tok, w), BF16),
                   jax.ShapeDtypeStruct((n_seq, 2, HG_HEADS, HG_DK, HG_DK), F32)],
        scratch_shapes=[pltpu.VMEM((n_tok, w), F32),
                        pltpu.VMEM((2, n_tok, w), F32),
                        pltpu.VMEM((2, n_tok, w), F32),
                        pltpu.VMEM((2, n_tok, w), BF16),
                        pltpu.VMEM((2, n_tok, w), BF16),
                        pltpu.VMEM((2, n_tok, w), F32),
                        pltpu.VMEM((n_tok, w), F32), pltpu.VMEM((n_tok, w), F32),
                        pltpu.VMEM((2, HG_HEADS, HG_DK, HG_DK), F32)],
        compiler_params=_params(("parallel",)),
    )(*args)


def _ctx_attn_kernel(q0_ref, q1_ref, k0_ref, k1_ref, v0_ref, v1_ref, o_ref, ko_ref, vo_ref, *, scale):
    q_refs, k_refs, v_refs = (q0_ref, q1_ref), (k0_ref, k1_ref), (v0_ref, v1_ref)
    half = q0_ref.shape[1]
    heads_per_block = half // NA_HEAD_DIM
    ko_ref[:, :half] = k0_ref[...]
    ko_ref[:, half:] = k1_ref[...]
    vo_ref[:, :half] = v0_ref[...]
    vo_ref[:, half:] = v1_ref[...]

    def head(refs, h):
        blk, off = divmod(h, heads_per_block)
        return refs[blk][:, off * NA_HEAD_DIM:(off + 1) * NA_HEAD_DIM].astype(BF16)

    scores = [_dot_nt(head(q_refs, h), head(k_refs, h)) * scale for h in range(NA_HEADS)]
    probs = []
    for s in scores:
        p = jnp.exp(s - jnp.max(s, axis=-1, keepdims=True))
        probs.append((p / jnp.sum(p, axis=-1, keepdims=True)).astype(BF16))
    for h in range(NA_HEADS):
        o = _dot(probs[h], head(v_refs, h))
        o_ref[:, h * NA_HEAD_DIM:(h + 1) * NA_HEAD_DIM] = o.astype(o_ref.dtype)


def _ctx_attn_call(z, *, n_seq, n_tok, col0):
    w = NA_HEADS * NA_HEAD_DIM
    half = w // 2
    zspec = lambda k: pl.BlockSpec((n_tok, half), lambda s: (s, col0 // half + k))
    return pl.pallas_call(
        functools.partial(_ctx_attn_kernel, scale=NA_HEAD_DIM ** -0.5),
        grid=(n_seq,),
        in_specs=[zspec(k) for k in range(6)],
        out_specs=[pl.BlockSpec((n_tok, w), lambda s: (s, 0))] * 3,
        out_shape=[jax.ShapeDtypeStruct((n_seq * n_tok, w), BF16),
                   jax.ShapeDtypeStruct((n_seq * n_tok, w), F32),
                   jax.ShapeDtypeStruct((n_seq * n_tok, w), F32)],
        compiler_params=_params(("parallel",)),
    )(z, z, z, z, z, z)


def _na_kernel(q_ref, k_ref, v_ref, kc_ref, vc_ref, bias_ref, o_ref, *, scale, rows, kh):
    q = q_ref[...].astype(BF16)
    k = k_ref[...].astype(BF16)
    v = v_ref[...].astype(BF16)
    qrow = lambda r: slice(r * GRID_W, (r + 1) * GRID_W)
    krows = lambda r: slice(min(max(r - kh // 2, 0), rows - kh) * GRID_W,
                            (min(max(r - kh // 2, 0), rows - kh) + kh) * GRID_W)
    s_ctx = _dot_nt(q, kc_ref[...].astype(BF16)) * scale
    s_loc = [_dot_nt(q[qrow(r)], k[krows(r)]) * scale + bias_ref[r] for r in range(rows)]
    p_loc, p_ctx = [], []
    for r in range(rows):
        sc = s_ctx[qrow(r)]
        m = jnp.maximum(jnp.max(s_loc[r], axis=-1, keepdims=True), jnp.max(sc, axis=-1, keepdims=True))
        el = jnp.exp(s_loc[r] - m)
        ec = jnp.exp(sc - m)
        den = jnp.sum(el, axis=-1, keepdims=True) + jnp.sum(ec, axis=-1, keepdims=True)
        p_loc.append((el / den).astype(BF16))
        p_ctx.append((ec / den).astype(BF16))
    o_ctx = _dot(jnp.concatenate(p_ctx, axis=0), vc_ref[...].astype(BF16))
    for r in range(rows):
        o = _dot(p_loc[r], v[krows(r)]) + o_ctx[qrow(r)]
        o_ref[qrow(r), :] = o.astype(o_ref.dtype)


def _na_bias(rpb, rows, kh):
    heads = rpb.shape[0]
    qcol = jnp.arange(GRID_W)
    kcol = jnp.arange(GRID_W)
    win_start = jnp.clip(qcol - NA_KW // 2, 0, GRID_W - NA_KW)
    in_win = (kcol[None, :] >= win_start[:, None]) & (kcol[None, :] < win_start[:, None] + NA_KW)
    lo = GRID_W - NA_KW
    pad = jnp.pad(rpb.astype(F32), ((0, 0), (0, 0), (lo, lo)))
    toep = jnp.stack([pad[:, :, GRID_W - 1 - q:2 * GRID_W - 1 - q] for q in range(GRID_W)], axis=2)
    toep = jnp.where(in_win[None, None], toep, NEG)
    per_row = []
    for r in range(rows):
        dy0 = min(max(r - kh // 2, 0), rows - kh) - r + NA_KH - 1
        per_row.append(toep[:, dy0:dy0 + kh])
    bias = jnp.stack(per_row, axis=1)
    return jnp.transpose(bias, (0, 1, 3, 2, 4)).reshape(heads, rows, GRID_W, kh * GRID_W)


def _na_call(z, cache_k, cache_v, bias, *, layer, n_seq, n_tok, col_block0):
    rows = n_tok // GRID_W
    kh = min(NA_KH, rows)
    dh = NA_HEAD_DIM
    past = cache_k.shape[2]
    zspec = lambda k: pl.BlockSpec((n_tok, dh), lambda b, h: (b, col_block0 + k * NA_HEADS + h))
    cspec = pl.BlockSpec((None, None, past, dh), lambda b, h: (b, layer, 0, h))
    return pl.pallas_call(
        functools.partial(_na_kernel, scale=dh ** -0.5, rows=rows, kh=kh),
        grid=(n_seq, NA_HEADS),
        in_specs=[zspec(0), zspec(1), zspec(2), cspec, cspec,
                  pl.BlockSpec((None, rows, GRID_W, kh * GRID_W), lambda b, h: (h, 0, 0, 0))],
        out_specs=pl.BlockSpec((n_tok, dh), lambda b, h: (b, h)),
        out_shape=jax.ShapeDtypeStruct((n_seq * n_tok, NA_HEADS * dh), BF16),
        compiler_params=_params(("parallel", "parallel")),
    )(z, z, z, cache_k, cache_v, bias)


def _lru_kernel(zx_ref, zg_ref, cw_ref, cb_ref, wa_ref, ba_ref, wx_ref, bx_ref, lam_ref, *rest,
                n_seq, n_tok, has_h0):
    if has_h0:
        h0_ref, y_ref, hfin_ref, a_s, u_s, y_s = rest
    else:
        y_ref, hfin_ref, a_s, u_s, y_s = rest
    tpos = lax.broadcasted_iota(jnp.int32, (n_tok, LANE), 0)
    gpos = tpos & (SUBLANE - 1)
    neg_lam = -lam_ref[...]
    softplus = jnp.maximum(neg_lam, 0.0) + jnp.log1p(jnp.exp(-jnp.abs(neg_lam)))
    left = LRU_CONV // 2

    def gates(g, carry):
        rows = _rows(g, n_tok)
        zx = zx_ref[rows, :]
        x = cb_ref[...] + cw_ref[left:left + 1, :] * zx
        for j in range(LRU_CONV):
            off = j - left
            if off == 0:
                continue
            shifted = pltpu.roll(zx, (-off) % n_tok, 0)
            valid = (tpos + off >= 0) & (tpos + off < n_tok)
            x = x + cw_ref[j:j + 1, :] * jnp.where(valid, shifted, 0.0)
        xb = x.astype(BF16)
        for d in range(2):
            r_gate = jax.nn.sigmoid(_dot(xb, wa_ref[d].astype(BF16)) + ba_ref[d:d + 1, :])
            i_gate = jax.nn.sigmoid(_dot(xb, wx_ref[d].astype(BF16)) + bx_ref[d:d + 1, :])
            log_a = (-LRU_C * r_gate) * softplus[d:d + 1, :]
            a = jnp.exp(log_a)
            u = jnp.sqrt(-_expm1(2.0 * log_a)) * (i_gate * x)
            sh = 1
            while sh < SUBLANE:
                if d == 0:
                    inside = gpos >= sh
                    amt = sh
                else:
                    inside = gpos < SUBLANE - sh
                    amt = n_tok - sh
                a_prev = jnp.where(inside, pltpu.roll(a, amt, 0), 1.0)
                u_prev = jnp.where(inside, pltpu.roll(u, amt, 0), 0.0)
                u = u + a * u_prev
                a = a * a_prev
                sh *= 2
            a_s[d, rows, :] = a
            u_s[d, rows, :] = u
        return carry
    lax.fori_loop(0, n_seq, gates, 0)

    n_groups = n_tok // SUBLANE
    edge = (SUBLANE - 1, 0)

    def bcast(row):
        return jnp.broadcast_to(row, (SUBLANE, LANE))

    if has_h0:
        h_init = tuple(bcast(h0_ref[d, g:g + 1, :]) for g in range(n_seq) for d in range(2))
    else:
        h_init = tuple(jnp.zeros((SUBLANE, LANE), F32) for _ in range(2 * n_seq))

    def step(j, carry):
        out = []
        for g in range(n_seq):
            for d in range(2):
                grp = j if d == 0 else n_groups - 1 - j
                rows = pl.ds(pl.multiple_of(g * n_tok + grp * SUBLANE, SUBLANE), SUBLANE)
                h = u_s[d, rows, :] + a_s[d, rows, :] * carry[2 * g + d]
                y_s[d, rows, :] = h
                out.append(bcast(h[edge[d]:edge[d] + 1, :]))
        return tuple(out)
    h_last = lax.fori_loop(0, n_groups, step, h_init)
    for g in range(n_seq):
        for d in range(2):
            hfin_ref[d, g:g + 1, :] = h_last[2 * g + d][0:1, :]

    def fin(g, carry):
        rows = _rows(g, n_tok)
        y = (y_s[0, rows, :] + y_s[1, rows, :]) * jax.nn.gelu(zg_ref[rows, :], approximate=True)
        y_ref[rows, :] = y.astype(y_ref.dtype)
        return carry
    lax.fori_loop(0, n_seq, fin, 0)


def _block_diag_pairs(w):
    depth, nd, nb, bw, _ = w.shape
    w = w.reshape(depth, nd, nb // 2, 2, bw, bw)
    eye = jnp.eye(2, dtype=w.dtype)
    out = w[:, :, :, :, :, None, :] * eye[None, None, None, :, None, :, None]
    return out.reshape(depth, nd, nb // 2, 2 * bw, 2 * bw)


def _lru_call(z, conv_w, conv_b, wa_bd, b_a, wx_bd, b_x, lam, h0_t, *, layer, n_seq, n_tok, col_block0):
    w = conv_b.shape[1]
    n_cb = w // LANE
    rows = n_seq * n_tok
    zspec = lambda k: pl.BlockSpec((rows, LANE), lambda cb: (0, col_block0 + k * n_cb + cb))
    vec2 = pl.BlockSpec((None, 2, LANE), lambda cb: (layer, 0, cb))
    wspec = pl.BlockSpec((None, 2, None, LANE, LANE), lambda cb: (layer, 0, cb, 0, 0))
    hspec = pl.BlockSpec((2, n_seq, LANE), lambda cb: (0, 0, cb))
    in_specs = [zspec(0), zspec(1),
                pl.BlockSpec((None, LRU_CONV, LANE), lambda cb: (layer, 0, cb)),
                pl.BlockSpec((None, 1, LANE), lambda cb: (layer, 0, cb)),
                wspec, vec2, wspec, vec2, vec2]
    args = [z, z, conv_w, conv_b.reshape(conv_b.shape[0], 1, w), wa_bd, b_a, wx_bd, b_x, lam]
    if h0_t is not None:
        in_specs.append(hspec)
        args.append(h0_t)
    return pl.pallas_call(
        functools.partial(_lru_kernel, n_seq=n_seq, n_tok=n_tok, has_h0=h0_t is not None),
        grid=(n_cb,),
        in_specs=in_specs,
        out_specs=[pl.BlockSpec((rows, LANE), lambda cb: (0, cb)), hspec],
        out_shape=[jax.ShapeDtypeStruct((rows, w), BF16), jax.ShapeDtypeStruct((2, n_seq, w), F32)],
        scratch_shapes=[pltpu.VMEM((2, rows, LANE), F32)] * 3,
        compiler_params=_params(("parallel",)),
    )(*args)


MERGE_GATE_BLOCK = 512


def _cast_rows(w_ref, w_s):
    def body(r, carry):
        sl = _rows(r, ROW_CHUNK)
        w_s[sl, :] = w_ref[sl, :].astype(BF16)
        return carry
    lax.fori_loop(0, w_ref.shape[0] // ROW_CHUNK, body, 0)


def _branch_sum_kernel(oa_ref, ob_ref, oc_ref, *rest, n_gate_blocks):
    gate_refs = rest[:3 * n_gate_blocks]
    wa_ref, wb_ref, wc_ref, m_ref, wa_s, wb_s, wc_s = rest[3 * n_gate_blocks:]

    @pl.when(pl.program_id(0) == 0)
    def _():
        _cast_rows(wa_ref, wa_s)
        _cast_rows(wb_ref, wb_s)
        _cast_rows(wc_ref, wc_s)

    gw = MERGE_GATE_BLOCK
    for n in range(n_gate_blocks):
        cols = slice(n * gw, (n + 1) * gw)
        m = None
        for k, (o_ref, w_s) in enumerate(((oa_ref, wa_s), (ob_ref, wb_s), (oc_ref, wc_s))):
            term = jax.nn.sigmoid(gate_refs[k * n_gate_blocks + n][...]) * _dot(o_ref[...], w_s[:, cols])
            m = term if m is None else m + term
        m_ref[:, cols] = m.astype(m_ref.dtype)


def _out_proj_kernel(x_ref, mods_ref, m_ref, wo_ref, o_ref, wo_s):
    @pl.when(pl.program_id(0) == 0)
    def _():
        _cast_rows(wo_ref, wo_s)

    gate = mods_ref[5:6, :]
    for r in range(x_ref.shape[0] // ROW_CHUNK):
        sl = slice(r * ROW_CHUNK, (r + 1) * ROW_CHUNK)
        o_ref[sl, :] = x_ref[sl, :] + gate * _dot(m_ref[sl, :], wo_s[...])


def _merge_call(x, mods, o_a, o_b, o_c, z, w_proj_a, w_proj_b, w_proj_c, w_out, *, layer, group0,
                rows_per_group, gate_col0):
    t, d = x.shape
    gw = MERGE_GATE_BLOCK
    nb = d // gw
    g0 = gate_col0 // gw
    tm1, tm2 = 256, 512
    widths = (o_a.shape[1], o_b.shape[1], o_c.shape[1])
    gate_specs = [pl.BlockSpec((tm1, gw), lambda i, col=g0 + k * nb + n: (i, col))
                  for k in range(3) for n in range(nb)]
    resident = lambda w: pl.BlockSpec((None, w, d), lambda i: (layer, 0, 0), pipeline_mode=pl.Buffered(1))
    m = pl.pallas_call(
        functools.partial(_branch_sum_kernel, n_gate_blocks=nb),
        grid=(t // tm1,),
        in_specs=([pl.BlockSpec((tm1, w), lambda i: (i, 0)) for w in widths] + gate_specs
                  + [resident(w) for w in widths]),
        out_specs=pl.BlockSpec((tm1, d), lambda i: (i, 0)),
        out_shape=jax.ShapeDtypeStruct((t, d), BF16),
        scratch_shapes=[pltpu.VMEM((w, d), BF16) for w in widths],
        compiler_params=_params(("arbitrary",)),
    )(o_a, o_b, o_c, *([z] * (3 * nb)), w_proj_a, w_proj_b, w_proj_c)
    group = _group_index(tm2, group0, rows_per_group)
    return pl.pallas_call(
        _out_proj_kernel,
        grid=(t // tm2,),
        in_specs=[
            pl.BlockSpec((tm2, d), lambda i: (i, 0)),
            pl.BlockSpec((None, None, N_MOD, d), lambda i: (layer, group(i), 0, 0)),
            pl.BlockSpec((tm2, d), lambda i: (i, 0)),
            resident(d),
        ],
        out_specs=pl.BlockSpec((tm2, d), lambda i: (i, 0)),
        out_shape=jax.ShapeDtypeStruct((t, d), F32),
        scratch_shapes=[pltpu.VMEM((d, d), BF16)],
        compiler_params=_params(("arbitrary",)),
    )(x, mods, m, w_out)


def kernel(x_prompt, x_sample, cache_na_k, cache_na_v, state_hgrn, state_lru, c, c_ctx, mod_w, mod_b, norm_g, ffn1_w_up, ffn1_w_down, ffn2_w_up, ffn2_w_down, w_in, hgrn_lb_logits, hgrn_norm_g, na_rpb, lru_conv_w, lru_conv_b, lru_w_a, lru_b_a, lru_w_x, lru_b_x, lru_lambda, w_proj_a, w_proj_b, w_proj_c, w_out, final_norm_g):
    b_ctx, n_ctx, d = x_prompt.shape
    b_lat, n_lat, _ = x_sample.shape
    depth = mod_w.shape[0]
    t_ctx, t_lat = b_ctx * n_ctx, b_lat * n_lat
    hg_w = HG_HEADS * HG_DK
    na_w = NA_HEADS * NA_HEAD_DIM
    lru_w = lru_conv_b.shape[1]
    na_col0 = 5 * hg_w
    lru_col0 = na_col0 + 3 * na_w
    gate_col0 = lru_col0 + 2 * lru_w

    ctx = dict(group0=0, rows_per_group=t_ctx)
    lat = dict(group0=1, rows_per_group=n_lat)
    xc = x_prompt.reshape(t_ctx, d)
    xl = x_sample.reshape(t_lat, d)
    cond = jnp.concatenate([c_ctx[None], c, jnp.zeros((N_COND_ROWS - 1 - b_lat, d), F32)], axis=0)
    mods = _mods_call(cond, mod_w, mod_b).reshape(depth, N_COND_ROWS, N_MOD, d)

    rows = n_lat // GRID_W
    kh = min(NA_KH, rows)
    wa_bd = _block_diag_pairs(lru_w_a)
    wx_bd = _block_diag_pairs(lru_w_x)
    cache_k = cache_na_k.reshape(b_lat, depth, cache_na_k.shape[2], na_w)
    cache_v = cache_na_v.reshape(b_lat, depth, cache_na_v.shape[2], na_w)
    s0_t = jnp.swapaxes(state_hgrn, -1, -2)
    h0_t = jnp.transpose(state_lru, (1, 2, 0, 3))

    ks, vs, hgs, lrus = [], [], [], []
    for l in range(depth):
        last = l == depth - 1
        final_g = final_norm_g if last else None
        lru_args = (lru_conv_w, lru_conv_b, wa_bd, lru_b_a, wx_bd, lru_b_x, lru_lambda)
        merge_w = (w_proj_a, w_proj_b, w_proj_c, w_out)

        xc = _ffn_call(xc, mods, norm_g, ffn1_w_up, ffn1_w_down, layer=l, slot=0, **ctx)
        zc = _inproj_call(xc, mods, norm_g, w_in, layer=l, **ctx)
        oa, s_c = _hgrn_call(zc, hgrn_lb_logits, hgrn_norm_g, None, layer=l, n_seq=b_ctx, n_tok=n_ctx)
        ob, k_l, v_l = _ctx_attn_call(zc, n_seq=b_ctx, n_tok=n_ctx, col0=na_col0)
        oc, h_c = _lru_call(zc, *lru_args, None, layer=l, n_seq=b_ctx, n_tok=n_ctx, col_block0=lru_col0 // LANE)
        xc = _merge_call(xc, mods, oa, ob, oc, zc, *merge_w, layer=l, gate_col0=gate_col0, **ctx)
        xc = _ffn_call(xc, mods, norm_g, ffn2_w_up, ffn2_w_down, layer=l, slot=2, final_g=final_g, **ctx)

        xl = _ffn_call(xl, mods, norm_g, ffn1_w_up, ffn1_w_down, layer=l, slot=0, **lat)
        zl = _inproj_call(xl, mods, norm_g, w_in, layer=l, **lat)
        oa, _ = _hgrn_call(zl, hgrn_lb_logits, hgrn_norm_g, s0_t[:, l], layer=l, n_seq=b_lat, n_tok=n_lat)
        bias = _na_bias(na_rpb[l], rows, kh)
        ob = _na_call(zl, cache_k, cache_v, bias, layer=l, n_seq=b_lat, n_tok=n_lat,
                      col_block0=na_col0 // NA_HEAD_DIM)
        oc, _ = _lru_call(zl, *lru_args, h0_t[l], layer=l, n_seq=b_lat, n_tok=n_lat, col_block0=lru_col0 // LANE)
        xl = _merge_call(xl, mods, oa, ob, oc, zl, *merge_w, layer=l, gate_col0=gate_col0, **lat)
        xl = _ffn_call(xl, mods, norm_g, ffn2_w_up, ffn2_w_down, layer=l, slot=2, final_g=final_g, **lat)

        ks.append(k_l.reshape(b_ctx, n_ctx, NA_HEADS, NA_HEAD_DIM))
        vs.append(v_l.reshape(b_ctx, n_ctx, NA_HEADS, NA_HEAD_DIM))
        hgs.append(s_c)
        lrus.append(jnp.transpose(h_c, (1, 0, 2)))

    return (xc.reshape(b_ctx, n_ctx, d), xl.reshape(b_lat, n_lat, d), jnp.stack(ks, axis=1),
            jnp.stack(vs, axis=1), jnp.stack(hgs, axis=1), jnp.stack(lrus, axis=1))
```

```python
import functools

import jax
import jax.numpy as jnp
from jax import lax
from jax.experimental import pallas as pl
from jax.experimental.pallas import tpu as pltpu

F32 = jnp.float32
BF16 = jnp.bfloat16

EPS = 1e-6
NEG = -1e30
LOG2E = 1.4426950408889634
N_MOD = 9
N_COND_ROWS = 8
HG_HEADS = 4
HG_DK = 128
HG_CHUNK = 16
HG_BLOCK = 128
HG_CHUNKS_PER_TRIP = 1
NA_HEADS = 8
NA_HEAD_DIM = 128
NA_KH = 8
NA_KW = 16
GRID_W = 64
LRU_CONV = 4
LRU_C = 8.0
LANE = 128
SUBLANE = 8
VMEM_LIMIT = 56 * 1024 * 1024
FFN_VMEM_LIMIT = 60 * 1024 * 1024


def _params(sem, vmem=VMEM_LIMIT):
    return pltpu.CompilerParams(dimension_semantics=sem, vmem_limit_bytes=vmem)


def _silu(x):
    return x * jax.nn.sigmoid(x)


def _expm1(x):
    u = jnp.exp(x)
    um1 = u - 1.0
    y = um1 * x / jnp.where(u == 1.0, 1.0, jnp.log(u))
    return jnp.where(u == 1.0, x, jnp.where(um1 == -1.0, -1.0, y))


def _dot(a, b):
    return jnp.dot(a, b, preferred_element_type=F32)


def _dot_nt(a, b):
    return lax.dot_general(a, b, (((1,), (1,)), ((), ())), preferred_element_type=F32)


def _dot_tn(a, b):
    return lax.dot_general(a, b, (((0,), (0,)), ((), ())), preferred_element_type=F32)


def _group_index(tm, group0, rows_per_group):
    tiles_per_group = rows_per_group // tm
    return lambda i: group0 + i // tiles_per_group


def _mods_kernel(c_ref, w_ref, b_ref, o_ref):
    s = _silu(c_ref[...]).astype(BF16)
    o_ref[...] = _dot(s, w_ref[...].astype(BF16)) + b_ref[...]


def _mods_call(cond, mod_w, mod_b):
    depth, d, n = mod_w.shape
    tn = 1024
    return pl.pallas_call(
        _mods_kernel,
        grid=(depth, n // tn),
        in_specs=[
            pl.BlockSpec((N_COND_ROWS, d), lambda l, j: (0, 0)),
            pl.BlockSpec((None, d, tn), lambda l, j: (l, 0, j)),
            pl.BlockSpec((None, 1, tn), lambda l, j: (l, 0, j)),
        ],
        out_specs=pl.BlockSpec((None, N_COND_ROWS, tn), lambda l, j: (l, 0, j)),
        out_shape=jax.ShapeDtypeStruct((depth, N_COND_ROWS, n), F32),
        compiler_params=_params(("parallel", "parallel")),
    )(cond, mod_w, mod_b.reshape(depth, 1, n))


ROW_CHUNK = 256
EW_CHUNK = 64


def _rows(r, n):
    return pl.ds(pl.multiple_of(r * n, n), n)


def _modulated_norm_tile(h_ref, x_ref, g_ref, shift, scale, rs_ref, zero_ref=None):
    tm = x_ref.shape[0]

    def stats(r, carry):
        sl = _rows(r, EW_CHUNK)
        x = x_ref[sl, :]
        rs_ref[sl, :] = lax.rsqrt(jnp.mean(x * x, axis=-1, keepdims=True) + EPS)
        return carry
    lax.fori_loop(0, tm // EW_CHUNK, stats, 0, unroll=4)

    gain = g_ref[...] * (1.0 + scale)

    def apply(r, carry):
        sl = _rows(r, EW_CHUNK)
        h_ref[sl, :] = ((x_ref[sl, :] * rs_ref[sl, :]) * gain + shift).astype(BF16)
        if zero_ref is not None:
            zero_ref[sl, :] = jnp.zeros((EW_CHUNK, zero_ref.shape[1]), F32)
        return carry
    lax.fori_loop(0, tm // EW_CHUNK, apply, 0)


def _ffn_kernel(x_ref, mods_ref, g_ref, wa_ref, wu_ref, wd_ref, *rest, slot, final):
    if final:
        fg_ref, o_ref, h_ref, rs_ref = rest
    else:
        o_ref, h_ref, rs_ref = rest
    j = pl.program_id(1)
    tm = x_ref.shape[0]
    shift = mods_ref[3 * slot:3 * slot + 1, :]
    scale = mods_ref[3 * slot + 1:3 * slot + 2, :]
    gate = mods_ref[3 * slot + 2:3 * slot + 3, :]

    @pl.when(j == 0)
    def _():
        _modulated_norm_tile(h_ref, x_ref, g_ref, shift, scale, rs_ref, zero_ref=o_ref)

    for r in range(tm // ROW_CHUNK):
        sl = slice(r * ROW_CHUNK, (r + 1) * ROW_CHUNK)
        h = h_ref[sl, :]
        a = _dot(h, wa_ref[...].astype(BF16))
        u = _dot(h, wu_ref[...].astype(BF16))
        act = (_silu(a) * u).astype(BF16)
        o_ref[sl, :] += _dot(act, wd_ref[...].astype(BF16))

    @pl.when(j == pl.num_programs(1) - 1)
    def _():
        def body(r, carry):
            sl = _rows(r, EW_CHUNK)
            y = x_ref[sl, :] + (0.5 * gate) * o_ref[sl, :]
            if final:
                y = y * lax.rsqrt(jnp.mean(y * y, axis=-1, keepdims=True) + EPS) * fg_ref[...]
            o_ref[sl, :] = y
            return carry
        lax.fori_loop(0, tm // EW_CHUNK, body, 0)


def _ffn_call(x, mods, norm_g, w_up, w_down, *, layer, slot, group0, rows_per_group, final_g=None):
    t, d = x.shape
    f = w_down.shape[1]
    tm, tf = 1024, 512
    nf = f // tf
    group = _group_index(tm, group0, rows_per_group)
    norm_slot = slot
    in_specs = [
        pl.BlockSpec((tm, d), lambda i, j: (i, 0), pipeline_mode=pl.Buffered(1)),
        pl.BlockSpec((None, None, N_MOD, d), lambda i, j: (layer, group(i), 0, 0)),
        pl.BlockSpec((None, None, 1, d), lambda i, j: (layer, norm_slot, 0, 0)),
        pl.BlockSpec((None, d, tf), lambda i, j: (layer, 0, j)),
        pl.BlockSpec((None, d, tf), lambda i, j: (layer, 0, nf + j)),
        pl.BlockSpec((None, tf, d), lambda i, j: (layer, j, 0)),
    ]
    args = [x, mods, norm_g.reshape(norm_g.shape[0], norm_g.shape[1], 1, d), w_up, w_up, w_down]
    if final_g is not None:
        in_specs.append(pl.BlockSpec((1, d), lambda i, j: (0, 0)))
        args.append(final_g.reshape(1, d))
    return pl.pallas_call(
        functools.partial(_ffn_kernel, slot=slot, final=final_g is not None),
        grid=(t // tm, nf),
        in_specs=in_specs,
        out_specs=pl.BlockSpec((tm, d), lambda i, j: (i, 0)),
        out_shape=jax.ShapeDtypeStruct((t, d), F32),
        scratch_shapes=[pltpu.VMEM((tm, d), BF16), pltpu.VMEM((tm, 1), F32)],
        compiler_params=_params(("parallel", "arbitrary"), vmem=FFN_VMEM_LIMIT),
    )(*args)


def _inproj_kernel(x_ref, mods_ref, g_ref, w_ref, z_ref, h_ref, w_s, rs_ref):
    j = pl.program_id(1)
    tm = x_ref.shape[0]
    shift = mods_ref[3:4, :]
    scale = mods_ref[4:5, :]

    @pl.when(j == 0)
    def _():
        _modulated_norm_tile(h_ref, x_ref, g_ref, shift, scale, rs_ref)

    w_s[...] = w_ref[...].astype(BF16)

    for r in range(tm // ROW_CHUNK):
        sl = slice(r * ROW_CHUNK, (r + 1) * ROW_CHUNK)
        z_ref[sl, :] = _dot(h_ref[sl, :], w_s[...])


def _inproj_call(x, mods, norm_g, w_in, *, layer, group0, rows_per_group):
    t, d = x.shape
    n = w_in.shape[2]
    tm, tn = 1024, 1280
    group = _group_index(tm, group0, rows_per_group)
    return pl.pallas_call(
        _inproj_kernel,
        grid=(t // tm, n // tn),
        in_specs=[
            pl.BlockSpec((tm, d), lambda i, j: (i, 0), pipeline_mode=pl.Buffered(1)),
            pl.BlockSpec((None, None, N_MOD, d), lambda i, j: (layer, group(i), 0, 0)),
            pl.BlockSpec((None, None, 1, d), lambda i, j: (layer, 1, 0, 0)),
            pl.BlockSpec((None, d, tn), lambda i, j: (layer, 0, j)),
        ],
        out_specs=pl.BlockSpec((tm, tn), lambda i, j: (i, j)),
        out_shape=jax.ShapeDtypeStruct((t, n), F32),
        scratch_shapes=[pltpu.VMEM((tm, d), BF16), pltpu.VMEM((d, tn), BF16), pltpu.VMEM((tm, 1), F32)],
        compiler_params=_params(("parallel", "arbitrary")),
    )(x, mods, norm_g.reshape(norm_g.shape[0], norm_g.shape[1], 1, d), w_in)


def _log_forget_and_key(zf, log_lb, log1m_lb):
    l1p = jnp.log(1.0 + jnp.exp(-jnp.abs(zf)))
    b = log1m_lb + (jnp.minimum(zf, 0.0) - l1p)
    log_k = log1m_lb + (jnp.minimum(-zf, 0.0) - l1p)
    hi = jnp.maximum(log_lb, b)
    return hi + jnp.log(1.0 + jnp.exp(-jnp.abs(log_lb - b))), log_k


def _hgrn_kernel(zq_ref, zff_ref, zfb_ref, zi_ref, zo_ref, lg_ref, ng_ref, *rest, layer, has_s0):
    if has_s0:
        s0_ref, o_ref, sfin_ref, q_s, c_s, b_s, qb_s, kd_s, dec_s, of_s, ob_s, st_s = rest
    else:
        o_ref, sfin_ref, q_s, c_s, b_s, qb_s, kd_s, dec_s, of_s, ob_s, st_s = rest
    n = zq_ref.shape[0]
    c = HG_CHUNK
    n_chunks = n // c
    blk = HG_BLOCK
    sh = c.bit_length() - 1

    logits = lg_ref[...]
    depth = logits.shape[0]
    mx = logits[0]
    for i in range(1, depth):
        mx = jnp.maximum(mx, logits[i])
    ex = [jnp.exp(logits[i] - mx) for i in range(depth)]
    tot = ex[0]
    for i in range(1, depth):
        tot = tot + ex[i]
    lb = jnp.zeros_like(mx)
    for i in range(1, layer + 1):
        lb = lb + ex[i] / tot
    log_lb = jnp.log(lb)
    log1m_lb = jnp.log1p(-lb)

    if has_s0:
        st_s[...] = s0_ref[...]
    else:
        st_s[...] = jnp.zeros(st_s.shape, F32)

    br = lax.broadcasted_iota(jnp.int32, (blk, blk), 0)
    bc = lax.broadcasted_iota(jnp.int32, (blk, blk), 1)
    same = (br >> sh) == (bc >> sh)
    cum_mat = [(same & (bc <= br)).astype(BF16), (same & (bc >= br)).astype(BF16)]
    tot_mat = same.astype(BF16)

    def gates(i, carry):
        rows = _rows(i, blk)
        q = _silu(zq_ref[rows, :])
        q_s[rows, :] = q
        for d in range(2):
            zf = (zff_ref if d == 0 else zfb_ref)[rows, :]
            lf, log_k = _log_forget_and_key(zf, log_lb[d:d + 1, :], log1m_lb[d:d + 1, :])
            k = jnp.exp(log_k)
            hi = lf.astype(BF16)
            lo = (lf - hi.astype(F32)).astype(BF16)
            b = _dot(cum_mat[d], hi) + _dot(cum_mat[d], lo)
            b_tot = _dot(tot_mat, hi) + _dot(tot_mat, lo)
            b2 = b * LOG2E
            b_s[d, rows, :] = b2
            c_s[d, rows, :] = b2 - log_k * LOG2E
            qb_s[d, rows, :] = (q * jnp.exp(b)).astype(BF16)
            kd_s[d, rows, :] = (k * jnp.exp(b_tot - b)).astype(BF16)
            dec_s[d, rows, :] = jnp.exp(b_tot)
        return carry
    lax.fori_loop(0, n // blk, gates, 0)

    nh = HG_HEADS
    hs = [slice(h * HG_DK, (h + 1) * HG_DK) for h in range(nh)]
    s8 = lax.broadcasted_iota(jnp.int32, (SUBLANE, nh * HG_DK), 0)
    pair = 2 * HG_DK
    pr = lax.broadcasted_iota(jnp.int32, (pair, pair), 0)
    pc = lax.broadcasted_iota(jnp.int32, (pair, pair), 1)
    ones_pair = ((pr >= HG_DK) == (pc >= HG_DK)).astype(BF16)
    sel_r = lax.broadcasted_iota(jnp.int32, (c, c * c), 0)
    sel_c = lax.broadcasted_iota(jnp.int32, (c, c * c), 1)
    sel = ((sel_c >= sel_r * c) & (sel_c < sel_r * c + c)).astype(BF16)
    zero8 = jnp.zeros((SUBLANE, nh * HG_DK), F32)

    def scores_of(d, rows):
        q = q_s[rows, :]
        b2 = b_s[d, rows, :]
        c2 = c_s[d, rows, :]
        halves = [c2[:SUBLANE], c2[SUBLANE:]]
        parts = []
        for t in range(c):
            bt, qt = b2[t:t + 1, :], q[t:t + 1, :]
            own = t // SUBLANE
            row = []
            for half, ch in enumerate(halves):
                if half == own:
                    tt = t - own * SUBLANE
                    vis = (s8 <= tt) if d == 0 else (s8 >= tt)
                    row.append(qt * jnp.exp2(jnp.where(vis, bt - ch, NEG)))
                elif (half < own) == (d == 0):
                    row.append(qt * jnp.exp2(bt - ch))
                else:
                    row.append(zero8)
            parts.append(row)
        p = jnp.concatenate([parts[t][half] for t in range(c) for half in range(2)], axis=0).astype(BF16)
        return jnp.concatenate([_dot(p[:, j * pair:(j + 1) * pair], ones_pair) for j in range(nh * HG_DK // pair)],
                               axis=1)

    def state_step(d, h, rows, ci):
        st = st_s[d, h]
        o_inter = _dot_nt(qb_s[d, rows, hs[h]], st.astype(BF16))
        dec8 = dec_s[d, pl.ds(pl.multiple_of(ci * c, c), SUBLANE), hs[h]]
        st_dec = (st.reshape(HG_DK // SUBLANE, SUBLANE, HG_DK) * dec8[None]).reshape(HG_DK, HG_DK)
        st_s[d, h] = st_dec + _dot_tn(zi_ref[rows, hs[h]].astype(BF16), kd_s[d, rows, hs[h]])
        return o_inter

    per_trip = HG_CHUNKS_PER_TRIP

    def body(i, carry):
        cis = [[per_trip * i + u for u in range(per_trip)],
               [n_chunks - 1 - per_trip * i - u for u in range(per_trip)]]
        rows = [[_rows(ci, c) for ci in cis[d]] for d in range(2)]
        scores = [[scores_of(d, rows[d][u]) for u in range(per_trip)] for d in range(2)]
        o_inter = [[[state_step(d, h, rows[d][u], cis[d][u]) for h in range(nh)] for u in range(per_trip)]
                   for d in range(2)]
        for d in range(2):
            for u in range(per_trip):
                v_rep = jnp.concatenate([zi_ref[rows[d][u], :]] * c, axis=0)
                o_intra = _dot(sel, (scores[d][u] * v_rep).astype(BF16))
                for h in range(nh):
                    (of_s if d == 0 else ob_s)[rows[d][u], hs[h]] = o_intra[:, hs[h]] + o_inter[d][u][h]
        return carry
    lax.fori_loop(0, n_chunks // per_trip, body, 0)

    for d in range(2):
        for h in range(nh):
            sfin_ref[d, h] = st_s[d, h].T

    def fin(r, carry):
        rows = _rows(r, EW_CHUNK)
        for h in range(HG_HEADS):
            cols = slice(h * HG_DK, (h + 1) * HG_DK)
            o = of_s[rows, cols] + ob_s[rows, cols]
            o = o * lax.rsqrt(jnp.mean(o * o, axis=-1, keepdims=True) + EPS) * ng_ref[h:h + 1, :]
            o_ref[rows, cols] = (o * _silu(zo_ref[rows, cols])).astype(o_ref.dtype)
        return carry
    lax.fori_loop(0, n // EW_CHUNK, fin, 0)


def _hgrn_call(z, lb_logits, hgrn_norm_g, s0_t, *, layer, n_seq, n_tok):
    w = HG_HEADS * HG_DK
    depth = lb_logits.shape[0]
    zspec = lambda k: pl.BlockSpec((n_tok, w), lambda s: (s, k))
    in_specs = [zspec(0), zspec(1), zspec(2), zspec(3), zspec(4),
                pl.BlockSpec((depth, 2, w), lambda s: (0, 0, 0)),
                pl.BlockSpec((None, HG_HEADS, HG_DK), lambda s: (layer, 0, 0))]
    args = [z, z, z, z, z, lb_logits, hgrn_norm_g]
    st_spec = pl.BlockSpec((None, 2, HG_HEADS, HG_DK, HG_DK), lambda s: (s, 0, 0, 0, 0))
    if s0_t is not None:
        in_specs.append(st_spec)
        args.append(s0_t)
    return pl.pallas_call(
        functools.partial(_hgrn_kernel, layer=layer, has_s0=s0_t is not None),
        grid=(n_seq,),
        in_specs=in_specs,
        out_specs=[pl.BlockSpec((n_tok, w), lambda s: (s, 0)), st_spec],
        out_shape=[jax.ShapeDtypeStruct((n_seq * n_tok, w), BF16),
                   jax.ShapeDtypeStruct((n_seq, 2, HG_HEADS, HG_DK, HG_DK), F32)],
        scratch_shapes=[pltpu.VMEM((n_tok, w), F32),
                        pltpu.VMEM((2, n_tok, w), F32),
                        pltpu.VMEM((2, n_tok, w), F32),
                        pltpu.VMEM((2, n_tok, w), BF16),
                        pltpu.VMEM((2, n_tok, w), BF16),
                        pltpu.VMEM((2, n_tok, w), F32),
                        pltpu.VMEM((n_tok, w), F32), pltpu.VMEM((n_tok, w), F32),
                        pltpu.VMEM((2, HG_HEADS, HG_DK, HG_DK), F32)],
        compiler_params=_params(("parallel",)),
    )(*args)


def _ctx_attn_kernel(q0_ref, q1_ref, k0_ref, k1_ref, v0_ref, v1_ref, o_ref, ko_ref, vo_ref, *, scale):
    q_refs, k_refs, v_refs = (q0_ref, q1_ref), (k0_ref, k1_ref), (v0_ref, v1_ref)
    half = q0_ref.shape[1]
    heads_per_block = half // NA_HEAD_DIM
    ko_ref[:, :half] = k0_ref[...]
    ko_ref[:, half:] = k1_ref[...]
    vo_ref[:, :half] = v0_ref[...]
    vo_ref[:, half:] = v1_ref[...]

    def head(refs, h):
        blk, off = divmod(h, heads_per_block)
        return refs[blk][:, off * NA_HEAD_DIM:(off + 1) * NA_HEAD_DIM].astype(BF16)

    scores = [_dot_nt(head(q_refs, h), head(k_refs, h)) * scale for h in range(NA_HEADS)]
    probs = []
    for s in scores:
        p = jnp.exp(s - jnp.max(s, axis=-1, keepdims=True))
        probs.append((p / jnp.sum(p, axis=-1, keepdims=True)).astype(BF16))
    for h in range(NA_HEADS):
        o = _dot(probs[h], head(v_refs, h))
        o_ref[:, h * NA_HEAD_DIM:(h + 1) * NA_HEAD_DIM] = o.astype(o_ref.dtype)


def _ctx_attn_call(z, *, n_seq, n_tok, col0):
    w = NA_HEADS * NA_HEAD_DIM
    half = w // 2
    zspec = lambda k: pl.BlockSpec((n_tok, half), lambda s: (s, col0 // half + k))
    return pl.pallas_call(
        functools.partial(_ctx_attn_kernel, scale=NA_HEAD_DIM ** -0.5),
        grid=(n_seq,),
        in_specs=[zspec(k) for k in range(6)],
        out_specs=[pl.BlockSpec((n_tok, w), lambda s: (s, 0))] * 3,
        out_shape=[jax.ShapeDtypeStruct((n_seq * n_tok, w), BF16),
                   jax.ShapeDtypeStruct((n_seq * n_tok, w), F32),
                   jax.ShapeDtypeStruct((n_seq * n_tok, w), F32)],
        compiler_params=_params(("parallel",)),
    )(z, z, z, z, z, z)


def _na_kernel(q_ref, k_ref, v_ref, kc_ref, vc_ref, bias_ref, o_ref, *, scale, rows, kh):
    q = q_ref[...].astype(BF16)
    k = k_ref[...].astype(BF16)
    v = v_ref[...].astype(BF16)
    qrow = lambda r: slice(r * GRID_W, (r + 1) * GRID_W)
    krows = lambda r: slice(min(max(r - kh // 2, 0), rows - kh) * GRID_W,
                            (min(max(r - kh // 2, 0), rows - kh) + kh) * GRID_W)
    s_ctx = _dot_nt(q, kc_ref[...].astype(BF16)) * scale
    s_loc = [_dot_nt(q[qrow(r)], k[krows(r)]) * scale + bias_ref[r] for r in range(rows)]
    p_loc, p_ctx = [], []
    for r in range(rows):
        sc = s_ctx[qrow(r)]
        m = jnp.maximum(jnp.max(s_loc[r], axis=-1, keepdims=True), jnp.max(sc, axis=-1, keepdims=True))
        el = jnp.exp(s_loc[r] - m)
        ec = jnp.exp(sc - m)
        den = jnp.sum(el, axis=-1, keepdims=True) + jnp.sum(ec, axis=-1, keepdims=True)
        p_loc.append((el / den).astype(BF16))
        p_ctx.append((ec / den).astype(BF16))
    o_ctx = _dot(jnp.concatenate(p_ctx, axis=0), vc_ref[...].astype(BF16))
    for r in range(rows):
        o = _dot(p_loc[r], v[krows(r)]) + o_ctx[qrow(r)]
        o_ref[qrow(r), :] = o.astype(o_ref.dtype)


def _na_bias(rpb, rows, kh):
    heads = rpb.shape[0]
    qcol = jnp.arange(GRID_W)
    kcol = jnp.arange(GRID_W)
    win_start = jnp.clip(qcol - NA_KW // 2, 0, GRID_W - NA_KW)
    in_win = (kcol[None, :] >= win_start[:, None]) & (kcol[None, :] < win_start[:, None] + NA_KW)
    lo = GRID_W - NA_KW
    pad = jnp.pad(rpb.astype(F32), ((0, 0), (0, 0), (lo, lo)))
    toep = jnp.stack([pad[:, :, GRID_W - 1 - q:2 * GRID_W - 1 - q] for q in range(GRID_W)], axis=2)
    toep = jnp.where(in_win[None, None], toep, NEG)
    per_row = []
    for r in range(rows):
        dy0 = min(max(r - kh // 2, 0), rows - kh) - r + NA_KH - 1
        per_row.append(toep[:, dy0:dy0 + kh])
    bias = jnp.stack(per_row, axis=1)
    return jnp.transpose(bias, (0, 1, 3, 2, 4)).reshape(heads, rows, GRID_W, kh * GRID_W)


def _na_call(z, cache_k, cache_v, bias, *, layer, n_seq, n_tok, col_block0):
    rows = n_tok // GRID_W
    kh = min(NA_KH, rows)
    dh = NA_HEAD_DIM
    past = cache_k.shape[2]
    zspec = lambda k: pl.BlockSpec((n_tok, dh), lambda b, h: (b, col_block0 + k * NA_HEADS + h))
    cspec = pl.BlockSpec((None, None, past, dh), lambda b, h: (b, layer, 0, h))
    return pl.pallas_call(
        functools.partial(_na_kernel, scale=dh ** -0.5, rows=rows, kh=kh),
        grid=(n_seq, NA_HEADS),
        in_specs=[zspec(0), zspec(1), zspec(2), cspec, cspec,
                  pl.BlockSpec((None, rows, GRID_W, kh * GRID_W), lambda b, h: (h, 0, 0, 0))],
        out_specs=pl.BlockSpec((n_tok, dh), lambda b, h: (b, h)),
        out_shape=jax.ShapeDtypeStruct((n_seq * n_tok, NA_HEADS * dh), BF16),
        compiler_params=_params(("parallel", "parallel")),
    )(z, z, z, cache_k, cache_v, bias)


def _lru_kernel(zx_ref, zg_ref, cw_ref, cb_ref, wa_ref, ba_ref, wx_ref, bx_ref, lam_ref, *rest,
                n_seq, n_tok, has_h0):
    if has_h0:
        h0_ref, y_ref, hfin_ref, a_s, u_s, y_s = rest
    else:
        y_ref, hfin_ref, a_s, u_s, y_s = rest
    tpos = lax.broadcasted_iota(jnp.int32, (n_tok, LANE), 0)
    gpos = tpos & (SUBLANE - 1)
    neg_lam = -lam_ref[...]
    softplus = jnp.maximum(neg_lam, 0.0) + jnp.log1p(jnp.exp(-jnp.abs(neg_lam)))
    left = LRU_CONV // 2

    def gates(g, carry):
        rows = _rows(g, n_tok)
        zx = zx_ref[rows, :]
        x = cb_ref[...] + cw_ref[left:left + 1, :] * zx
        for j in range(LRU_CONV):
            off = j - left
            if off == 0:
                continue
            shifted = pltpu.roll(zx, (-off) % n_tok, 0)
            valid = (tpos + off >= 0) & (tpos + off < n_tok)
            x = x + cw_ref[j:j + 1, :] * jnp.where(valid, shifted, 0.0)
        xb = x.astype(BF16)
        for d in range(2):
            r_gate = jax.nn.sigmoid(_dot(xb, wa_ref[d].astype(BF16)) + ba_ref[d:d + 1, :])
            i_gate = jax.nn.sigmoid(_dot(xb, wx_ref[d].astype(BF16)) + bx_ref[d:d + 1, :])
            log_a = (-LRU_C * r_gate) * softplus[d:d + 1, :]
            a = jnp.exp(log_a)
            u = jnp.sqrt(-_expm1(2.0 * log_a)) * (i_gate * x)
            sh = 1
            while sh < SUBLANE:
                if d == 0:
                    inside = gpos >= sh
                    amt = sh
                else:
                    inside = gpos < SUBLANE - sh
                    amt = n_tok - sh
                a_prev = jnp.where(inside, pltpu.roll(a, amt, 0), 1.0)
                u_prev = jnp.where(inside, pltpu.roll(u, amt, 0), 0.0)
                u = u + a * u_prev
                a = a * a_prev
                sh *= 2
            a_s[d, rows, :] = a
            u_s[d, rows, :] = u
        return carry
    lax.fori_loop(0, n_seq, gates, 0)

    n_groups = n_tok // SUBLANE
    edge = (SUBLANE - 1, 0)

    def bcast(row):
        return jnp.broadcast_to(row, (SUBLANE, LANE))

    if has_h0:
        h_init = tuple(bcast(h0_ref[d, g:g + 1, :]) for g in range(n_seq) for d in range(2))
    else:
        h_init = tuple(jnp.zeros((SUBLANE, LANE), F32) for _ in range(2 * n_seq))

    def step(j, carry):
        out = []
        for g in range(n_seq):
            for d in range(2):
                grp = j if d == 0 else n_groups - 1 - j
                rows = pl.ds(pl.multiple_of(g * n_tok + grp * SUBLANE, SUBLANE), SUBLANE)
                h = u_s[d, rows, :] + a_s[d, rows, :] * carry[2 * g + d]
                y_s[d, rows, :] = h
                out.append(bcast(h[edge[d]:edge[d] + 1, :]))
        return tuple(out)
    h_last = lax.fori_loop(0, n_groups, step, h_init)
    for g in range(n_seq):
        for d in range(2):
            hfin_ref[d, g:g + 1, :] = h_last[2 * g + d][0:1, :]

    def fin(g, carry):
        rows = _rows(g, n_tok)
        y = (y_s[0, rows, :] + y_s[1, rows, :]) * jax.nn.gelu(zg_ref[rows, :], approximate=True)
        y_ref[rows, :] = y.astype(y_ref.dtype)
        return carry
    lax.fori_loop(0, n_seq, fin, 0)


def _block_diag_pairs(w):
    depth, nd, nb, bw, _ = w.shape
    w = w.reshape(depth, nd, nb // 2, 2, bw, bw)
    eye = jnp.eye(2, dtype=w.dtype)
    out = w[:, :, :, :, :, None, :] * eye[None, None, None, :, None, :, None]
    return out.reshape(depth, nd, nb // 2, 2 * bw, 2 * bw)


def _lru_call(z, conv_w, conv_b, wa_bd, b_a, wx_bd, b_x, lam, h0_t, *, layer, n_seq, n_tok, col_block0):
    w = conv_b.shape[1]
    n_cb = w // LANE
    rows = n_seq * n_tok
    zspec = lambda k: pl.BlockSpec((rows, LANE), lambda cb: (0, col_block0 + k * n_cb + cb))
    vec2 = pl.BlockSpec((None, 2, LANE), lambda cb: (layer, 0, cb))
    wspec = pl.BlockSpec((None, 2, None, LANE, LANE), lambda cb: (layer, 0, cb, 0, 0))
    hspec = pl.BlockSpec((2, n_seq, LANE), lambda cb: (0, 0, cb))
    in_specs = [zspec(0), zspec(1),
                pl.BlockSpec((None, LRU_CONV, LANE), lambda cb: (layer, 0, cb)),
                pl.BlockSpec((None, 1, LANE), lambda cb: (layer, 0, cb)),
                wspec, vec2, wspec, vec2, vec2]
    args = [z, z, conv_w, conv_b.reshape(conv_b.shape[0], 1, w), wa_bd, b_a, wx_bd, b_x, lam]
    if h0_t is not None:
        in_specs.append(hspec)
        args.append(h0_t)
    return pl.pallas_call(
        functools.partial(_lru_kernel, n_seq=n_seq, n_tok=n_tok, has_h0=h0_t is not None),
        grid=(n_cb,),
        in_specs=in_specs,
        out_specs=[pl.BlockSpec((rows, LANE), lambda cb: (0, cb)), hspec],
        out_shape=[jax.ShapeDtypeStruct((rows, w), BF16), jax.ShapeDtypeStruct((2, n_seq, w), F32)],
        scratch_shapes=[pltpu.VMEM((2, rows, LANE), F32)] * 3,
        compiler_params=_params(("parallel",)),
    )(*args)


MERGE_GATE_BLOCK = 512


def _cast_rows(w_ref, w_s):
    def body(r, carry):
        sl = _rows(r, ROW_CHUNK)
        w_s[sl, :] = w_ref[sl, :].astype(BF16)
        return carry
    lax.fori_loop(0, w_ref.shape[0] // ROW_CHUNK, body, 0)


def _branch_sum_kernel(oa_ref, ob_ref, oc_ref, *rest, n_gate_blocks):
    gate_refs = rest[:3 * n_gate_blocks]
    wa_ref, wb_ref, wc_ref, m_ref, wa_s, wb_s, wc_s = rest[3 * n_gate_blocks:]

    @pl.when(pl.program_id(0) == 0)
    def _():
        _cast_rows(wa_ref, wa_s)
        _cast_rows(wb_ref, wb_s)
        _cast_rows(wc_ref, wc_s)

    gw = MERGE_GATE_BLOCK
    for n in range(n_gate_blocks):
        cols = slice(n * gw, (n + 1) * gw)
        m = None
        for k, (o_ref, w_s) in enumerate(((oa_ref, wa_s), (ob_ref, wb_s), (oc_ref, wc_s))):
            term = jax.nn.sigmoid(gate_refs[k * n_gate_blocks + n][...]) * _dot(o_ref[...], w_s[:, cols])
            m = term if m is None else m + term
        m_ref[:, cols] = m.astype(m_ref.dtype)


def _out_proj_kernel(x_ref, mods_ref, m_ref, wo_ref, o_ref, wo_s):
    @pl.when(pl.program_id(0) == 0)
    def _():
        _cast_rows(wo_ref, wo_s)

    gate = mods_ref[5:6, :]
    for r in range(x_ref.shape[0] // ROW_CHUNK):
        sl = slice(r * ROW_CHUNK, (r + 1) * ROW_CHUNK)
        o_ref[sl, :] = x_ref[sl, :] + gate * _dot(m_ref[sl, :], wo_s[...])


def _merge_call(x, mods, o_a, o_b, o_c, z, w_proj_a, w_proj_b, w_proj_c, w_out, *, layer, group0,
                rows_per_group, gate_col0):
    t, d = x.shape
    gw = MERGE_GATE_BLOCK
    nb = d // gw
    g0 = gate_col0 // gw
    tm1, tm2 = 256, 512
    widths = (o_a.shape[1], o_b.shape[1], o_c.shape[1])
    gate_specs = [pl.BlockSpec((tm1, gw), lambda i, col=g0 + k * nb + n: (i, col))
                  for k in range(3) for n in range(nb)]
    resident = lambda w: pl.BlockSpec((None, w, d), lambda i: (layer, 0, 0), pipeline_mode=pl.Buffered(1))
    m = pl.pallas_call(
        functools.partial(_branch_sum_kernel, n_gate_blocks=nb),
        grid=(t // tm1,),
        in_specs=([pl.BlockSpec((tm1, w), lambda i: (i, 0)) for w in widths] + gate_specs
                  + [resident(w) for w in widths]),
        out_specs=pl.BlockSpec((tm1, d), lambda i: (i, 0)),
        out_shape=jax.ShapeDtypeStruct((t, d), BF16),
        scratch_shapes=[pltpu.VMEM((w, d), BF16) for w in widths],
        compiler_params=_params(("arbitrary",)),
    )(o_a, o_b, o_c, *([z] * (3 * nb)), w_proj_a, w_proj_b, w_proj_c)
    group = _group_index(tm2, group0, rows_per_group)
    return pl.pallas_call(
        _out_proj_kernel,
        grid=(t // tm2,),
        in_specs=[
            pl.BlockSpec((tm2, d), lambda i: (i, 0)),
            pl.BlockSpec((None, None, N_MOD, d), lambda i: (layer, group(i), 0, 0)),
            pl.BlockSpec((tm2, d), lambda i: (i, 0)),
            resident(d),
        ],
        out_specs=pl.BlockSpec((tm2, d), lambda i: (i, 0)),
        out_shape=jax.ShapeDtypeStruct((t, d), F32),
        scratch_shapes=[pltpu.VMEM((d, d), BF16)],
        compiler_params=_params(("arbitrary",)),
    )(x, mods, m, w_out)


def kernel(x_prompt, x_sample, cache_na_k, cache_na_v, state_hgrn, state_lru, c, c_ctx, mod_w, mod_b, norm_g, ffn1_w_up, ffn1_w_down, ffn2_w_up, ffn2_w_down, w_in, hgrn_lb_logits, hgrn_norm_g, na_rpb, lru_conv_w, lru_conv_b, lru_w_a, lru_b_a, lru_w_x, lru_b_x, lru_lambda, w_proj_a, w_proj_b, w_proj_c, w_out, final_norm_g):
    b_ctx, n_ctx, d = x_prompt.shape
    b_lat, n_lat, _ = x_sample.shape
    depth = mod_w.shape[0]
    t_ctx, t_lat = b_ctx * n_ctx, b_lat * n_lat
    hg_w = HG_HEADS * HG_DK
    na_w = NA_HEADS * NA_HEAD_DIM
    lru_w = lru_conv_b.shape[1]
    na_col0 = 5 * hg_w
    lru_col0 = na_col0 + 3 * na_w
    gate_col0 = lru_col0 + 2 * lru_w

    ctx = dict(group0=0, rows_per_group=t_ctx)
    lat = dict(group0=1, rows_per_group=n_lat)
    xc = x_prompt.reshape(t_ctx, d)
    xl = x_sample.reshape(t_lat, d)
    cond = jnp.concatenate([c_ctx[None], c, jnp.zeros((N_COND_ROWS - 1 - b_lat, d), F32)], axis=0)
    mods = _mods_call(cond, mod_w, mod_b).reshape(depth, N_COND_ROWS, N_MOD, d)

    rows = n_lat // GRID_W
    kh = min(NA_KH, rows)
    wa_bd = _block_diag_pairs(lru_w_a)
    wx_bd = _block_diag_pairs(lru_w_x)
    cache_k = cache_na_k.reshape(b_lat, depth, cache_na_k.shape[2], na_w)
    cache_v = cache_na_v.reshape(b_lat, depth, cache_na_v.shape[2], na_w)
    s0_t = jnp.swapaxes(state_hgrn, -1, -2)
    h0_t = jnp.transpose(state_lru, (1, 2, 0, 3))

    ks, vs, hgs, lrus = [], [], [], []
    for l in range(depth):
        last = l == depth - 1
        final_g = final_norm_g if last else None
        lru_args = (lru_conv_w, lru_conv_b, wa_bd, lru_b_a, wx_bd, lru_b_x, lru_lambda)
        merge_w = (w_proj_a, w_proj_b, w_proj_c, w_out)

        xc = _ffn_call(xc, mods, norm_g, ffn1_w_up, ffn1_w_down, layer=l, slot=0, **ctx)
        zc = _inproj_call(xc, mods, norm_g, w_in, layer=l, **ctx)
        oa, s_c = _hgrn_call(zc, hgrn_lb_logits, hgrn_norm_g, None, layer=l, n_seq=b_ctx, n_tok=n_ctx)
        ob, k_l, v_l = _ctx_attn_call(zc, n_seq=b_ctx, n_tok=n_ctx, col0=na_col0)
        oc, h_c = _lru_call(zc, *lru_args, None, layer=l, n_seq=b_ctx, n_tok=n_ctx, col_block0=lru_col0 // LANE)
        xc = _merge_call(xc, mods, oa, ob, oc, zc, *merge_w, layer=l, gate_col0=gate_col0, **ctx)
        xc = _ffn_call(xc, mods, norm_g, ffn2_w_up, ffn2_w_down, layer=l, slot=2, final_g=final_g, **ctx)

        xl = _ffn_call(xl, mods, norm_g, ffn1_w_up, ffn1_w_down, layer=l, slot=0, **lat)
        zl = _inproj_call(xl, mods, norm_g, w_in, layer=l, **lat)
        oa, _ = _hgrn_call(zl, hgrn_lb_logits, hgrn_norm_g, s0_t[:, l], layer=l, n_seq=b_lat, n_tok=n_lat)
        bias = _na_bias(na_rpb[l], rows, kh)
        ob = _na_call(zl, cache_k, cache_v, bias, layer=l, n_seq=b_lat, n_tok=n_lat,
                      col_block0=na_col0 // NA_HEAD_DIM)
        oc, _ = _lru_call(zl, *lru_args, h0_t[l], layer=l, n_seq=b_lat, n_tok=n_lat, col_block0=lru_col0 // LANE)
        xl = _merge_call(xl, mods, oa, ob, oc, zl, *merge_w, layer=l, gate_col0=gate_col0, **lat)
        xl = _ffn_call(xl, mods, norm_g, ffn2_w_up, ffn2_w_down, layer=l, slot=2, final_g=final_g, **lat)

        ks.append(k_l.reshape(b_ctx, n_ctx, NA_HEADS, NA_HEAD_DIM))
        vs.append(v_l.reshape(b_ctx, n_ctx, NA_HEADS, NA_HEAD_DIM))
        hgs.append(s_c)
        lrus.append(jnp.transpose(h_c, (1, 0, 2)))

    return (xc.reshape(b_ctx, n_ctx, d), xl.reshape(b_lat, n_lat, d), jnp.stack(ks, axis=1),
            jnp.stack(vs, axis=1), jnp.stack(hgs, axis=1), jnp.stack(lrus, axis=1))
```

```python
import functools

import jax
import jax.numpy as jnp
from jax import lax
from jax.experimental import pallas as pl
from jax.experimental.pallas import tpu as pltpu

F32 = jnp.float32
BF16 = jnp.bfloat16

EPS = 1e-6
NEG = -1e30
LOG2E = 1.4426950408889634
N_MOD = 9
N_COND_ROWS = 8
HG_HEADS = 4
HG_DK = 128
HG_CHUNK = 16
HG_BLOCK = 128
HG_CHUNKS_PER_TRIP = 1
NA_HEADS = 8
NA_HEAD_DIM = 128
NA_KH = 8
NA_KW = 16
GRID_W = 64
LRU_CONV = 4
LRU_C = 8.0
LANE = 128
SUBLANE = 8
VMEM_LIMIT = 56 * 1024 * 1024
FFN_VMEM_LIMIT = 60 * 1024 * 1024


def _params(sem, vmem=VMEM_LIMIT):
    return pltpu.CompilerParams(dimension_semantics=sem, vmem_limit_bytes=vmem)


def _silu(x):
    return x * jax.nn.sigmoid(x)


def _expm1(x):
    u = jnp.exp(x)
    um1 = u - 1.0
    y = um1 * x / jnp.where(u == 1.0, 1.0, jnp.log(u))
    return jnp.where(u == 1.0, x, jnp.where(um1 == -1.0, -1.0, y))


def _dot(a, b):
    return jnp.dot(a, b, preferred_element_type=F32)


def _dot_nt(a, b):
    return lax.dot_general(a, b, (((1,), (1,)), ((), ())), preferred_element_type=F32)


def _dot_tn(a, b):
    return lax.dot_general(a, b, (((0,), (0,)), ((), ())), preferred_element_type=F32)


def _group_index(tm, group0, rows_per_group):
    tiles_per_group = rows_per_group // tm
    return lambda i: group0 + i // tiles_per_group


def _mods_kernel(c_ref, w_ref, b_ref, o_ref):
    s = _silu(c_ref[...]).astype(BF16)
    o_ref[...] = _dot(s, w_ref[...].astype(BF16)) + b_ref[...]


def _mods_call(cond, mod_w, mod_b):
    depth, d, n = mod_w.shape
    tn = 1024
    return pl.pallas_call(
        _mods_kernel,
        grid=(depth, n // tn),
        in_specs=[
            pl.BlockSpec((N_COND_ROWS, d), lambda l, j: (0, 0)),
            pl.BlockSpec((None, d, tn), lambda l, j: (l, 0, j)),
            pl.BlockSpec((None, 1, tn), lambda l, j: (l, 0, j)),
        ],
        out_specs=pl.BlockSpec((None, N_COND_ROWS, tn), lambda l, j: (l, 0, j)),
        out_shape=jax.ShapeDtypeStruct((depth, N_COND_ROWS, n), F32),
        compiler_params=_params(("parallel", "parallel")),
    )(cond, mod_w, mod_b.reshape(depth, 1, n))


ROW_CHUNK = 256
EW_CHUNK = 64


def _rows(r, n):
    return pl.ds(pl.multiple_of(r * n, n), n)


def _modulated_norm_tile(h_ref, x_ref, g_ref, shift, scale, rs_ref, zero_ref=None):
    tm = x_ref.shape[0]

    def stats(r, carry):
        sl = _rows(r, EW_CHUNK)
        x = x_ref[sl, :]
        rs_ref[sl, :] = lax.rsqrt(jnp.mean(x * x, axis=-1, keepdims=True) + EPS)
        return carry
    lax.fori_loop(0, tm // EW_CHUNK, stats, 0, unroll=4)

    gain = g_ref[...] * (1.0 + scale)

    def apply(r, carry):
        sl = _rows(r, EW_CHUNK)
        h_ref[sl, :] = ((x_ref[sl, :] * rs_ref[sl, :]) * gain + shift).astype(BF16)
        if zero_ref is not None:
            zero_ref[sl, :] = jnp.zeros((EW_CHUNK, zero_ref.shape[1]), F32)
        return carry
    lax.fori_loop(0, tm // EW_CHUNK, apply, 0, unroll=2)


def _ffn_kernel(x_ref, mods_ref, g_ref, wa_ref, wu_ref, wd_ref, *rest, slot, final):
    if final:
        fg_ref, o_ref, h_ref, rs_ref = rest
    else:
        o_ref, h_ref, rs_ref = rest
    j = pl.program_id(1)
    tm = x_ref.shape[0]
    shift = mods_ref[3 * slot:3 * slot + 1, :]
    scale = mods_ref[3 * slot + 1:3 * slot + 2, :]
    gate = mods_ref[3 * slot + 2:3 * slot + 3, :]

    @pl.when(j == 0)
    def _():
        _modulated_norm_tile(h_ref, x_ref, g_ref, shift, scale, rs_ref, zero_ref=o_ref)

    for r in range(tm // ROW_CHUNK):
        sl = slice(r * ROW_CHUNK, (r + 1) * ROW_CHUNK)
        h = h_ref[sl, :]
        a = _dot(h, wa_ref[...].astype(BF16))
        u = _dot(h, wu_ref[...].astype(BF16))
        act = (_silu(a) * u).astype(BF16)
        o_ref[sl, :] += _dot(act, wd_ref[...].astype(BF16))

    @pl.when(j == pl.num_programs(1) - 1)
    def _():
        def body(r, carry):
            sl = _rows(r, EW_CHUNK)
            y = x_ref[sl, :] + (0.5 * gate) * o_ref[sl, :]
            if final:
                y = y * lax.rsqrt(jnp.mean(y * y, axis=-1, keepdims=True) + EPS) * fg_ref[...]
            o_ref[sl, :] = y
            return carry
        lax.fori_loop(0, tm // EW_CHUNK, body, 0, unroll=2)


def _ffn_call(x, mods, norm_g, w_up, w_down, *, layer, slot, group0, rows_per_group, final_g=None):
    t, d = x.shape
    f = w_down.shape[1]
    tm, tf = 1024, 512
    nf = f // tf
    group = _group_index(tm, group0, rows_per_group)
    norm_slot = slot
    in_specs = [
        pl.BlockSpec((tm, d), lambda i, j: (i, 0), pipeline_mode=pl.Buffered(1)),
        pl.BlockSpec((None, None, N_MOD, d), lambda i, j: (layer, group(i), 0, 0)),
        pl.BlockSpec((None, None, 1, d), lambda i, j: (layer, norm_slot, 0, 0)),
        pl.BlockSpec((None, d, tf), lambda i, j: (layer, 0, j)),
        pl.BlockSpec((None, d, tf), lambda i, j: (layer, 0, nf + j)),
        pl.BlockSpec((None, tf, d), lambda i, j: (layer, j, 0)),
    ]
    args = [x, mods, norm_g.reshape(norm_g.shape[0], norm_g.shape[1], 1, d), w_up, w_up, w_down]
    if final_g is not None:
        in_specs.append(pl.BlockSpec((1, d), lambda i, j: (0, 0)))
        args.append(final_g.reshape(1, d))
    return pl.pallas_call(
        functools.partial(_ffn_kernel, slot=slot, final=final_g is not None),
        grid=(t // tm, nf),
        in_specs=in_specs,
        out_specs=pl.BlockSpec((tm, d), lambda i, j: (i, 0)),
        out_shape=jax.ShapeDtypeStruct((t, d), F32),
        scratch_shapes=[pltpu.VMEM((tm, d), BF16), pltpu.VMEM((tm, 1), F32)],
        compiler_params=_params(("parallel", "arbitrary"), vmem=FFN_VMEM_LIMIT),
    )(*args)


def _inproj_kernel(x_ref, mods_ref, g_ref, w_ref, z_ref, h_ref, w_s, rs_ref):
    j = pl.program_id(1)
    tm = x_ref.shape[0]
    shift = mods_ref[3:4, :]
    scale = mods_ref[4:5, :]

    @pl.when(j == 0)
    def _():
        _modulated_norm_tile(h_ref, x_ref, g_ref, shift, scale, rs_ref)

    w_s[...] = w_ref[...].astype(BF16)

    for r in range(tm // ROW_CHUNK):
        sl = slice(r * ROW_CHUNK, (r + 1) * ROW_CHUNK)
        z_ref[sl, :] = _dot(h_ref[sl, :], w_s[...])


def _inproj_call(x, mods, norm_g, w_in, *, layer, group0, rows_per_group):
    t, d = x.shape
    n = w_in.shape[2]
    tm, tn = 1024, 1280
    group = _group_index(tm, group0, rows_per_group)
    return pl.pallas_call(
        _inproj_kernel,
        grid=(t // tm, n // tn),
        in_specs=[
            pl.BlockSpec((tm, d), lambda i, j: (i, 0), pipeline_mode=pl.Buffered(1)),
            pl.BlockSpec((None, None, N_MOD, d), lambda i, j: (layer, group(i), 0, 0)),
            pl.BlockSpec((None, None, 1, d), lambda i, j: (layer, 1, 0, 0)),
            pl.BlockSpec((None, d, tn), lambda i, j: (layer, 0, j)),
        ],
        out_specs=pl.BlockSpec((tm, tn), lambda i, j: (i, j)),
        out_shape=jax.ShapeDtypeStruct((t, n), F32),
        scratch_shapes=[pltpu.VMEM((tm, d), BF16), pltpu.VMEM((d, tn), BF16), pltpu.VMEM((tm, 1), F32)],
        compiler_params=_params(("parallel", "arbitrary")),
    )(x, mods, norm_g.reshape(norm_g.shape[0], norm_g.shape[1], 1, d), w_in)


def _log_forget_and_key(zf, log_lb, log1m_lb):
    l1p = jnp.log(1.0 + jnp.exp(-jnp.abs(zf)))
    b = log1m_lb + (jnp.minimum(zf, 0.0) - l1p)
    log_k = log1m_lb + (jnp.minimum(-zf, 0.0) - l1p)
    hi = jnp.maximum(log_lb, b)
    return hi + jnp.log(1.0 + jnp.exp(-jnp.abs(log_lb - b))), log_k


def _hgrn_kernel(zq_ref, zff_ref, zfb_ref, zi_ref, zo_ref, lg_ref, ng_ref, *rest, layer, has_s0):
    if has_s0:
        s0_ref, o_ref, sfin_ref, q_s, c_s, b_s, qb_s, kd_s, dec_s, of_s, ob_s, st_s = rest
    else:
        o_ref, sfin_ref, q_s, c_s, b_s, qb_s, kd_s, dec_s, of_s, ob_s, st_s = rest
    n = zq_ref.shape[0]
    c = HG_CHUNK
    n_chunks = n // c
    blk = HG_BLOCK
    sh = c.bit_length() - 1

    logits = lg_ref[...]
    depth = logits.shape[0]
    mx = logits[0]
    for i in range(1, depth):
        mx = jnp.maximum(mx, logits[i])
    ex = [jnp.exp(logits[i] - mx) for i in range(depth)]
    tot = ex[0]
    for i in range(1, depth):
        tot = tot + ex[i]
    lb = jnp.zeros_like(mx)
    for i in range(1, layer + 1):
        lb = lb + ex[i] / tot
    log_lb = jnp.log(lb)
    log1m_lb = jnp.log1p(-lb)

    if has_s0:
        st_s[...] = s0_ref[...]
    else:
        st_s[...] = jnp.zeros(st_s.shape, F32)

    br = lax.broadcasted_iota(jnp.int32, (blk, blk), 0)
    bc = lax.broadcasted_iota(jnp.int32, (blk, blk), 1)
    same = (br >> sh) == (bc >> sh)
    cum_mat = [(same & (bc <= br)).astype(BF16), (same & (bc >= br)).astype(BF16)]
    tot_mat = same.astype(BF16)

    def gates(i, carry):
        rows = _rows(i, blk)
        q = _silu(zq_ref[rows, :])
        q_s[rows, :] = q
        for d in range(2):
            zf = (zff_ref if d == 0 else zfb_ref)[rows, :]
            lf, log_k = _log_forget_and_key(zf, log_lb[d:d + 1, :], log1m_lb[d:d + 1, :])
            k = jnp.exp(log_k)
            hi = lf.astype(BF16)
            lo = (lf - hi.astype(F32)).astype(BF16)
            b = _dot(cum_mat[d], hi) + _dot(cum_mat[d], lo)
            b_tot = _dot(tot_mat, hi) + _dot(tot_mat, lo)
            b2 = b * LOG2E
            b_s[d, rows, :] = b2
            c_s[d, rows, :] = b2 - log_k * LOG2E
            qb_s[d, rows, :] = (q * jnp.exp(b)).astype(BF16)
            kd_s[d, rows, :] = (k * jnp.exp(b_tot - b)).astype(BF16)
            dec_s[d, rows, :] = jnp.exp(b_tot)
        return carry
    lax.fori_loop(0, n // blk, gates, 0)

    nh = HG_HEADS
    hs = [slice(h * HG_DK, (h + 1) * HG_DK) for h in range(nh)]
    s8 = lax.broadcasted_iota(jnp.int32, (SUBLANE, nh * HG_DK), 0)
    pair = 2 * HG_DK
    pr = lax.broadcasted_iota(jnp.int32, (pair, pair), 0)
    pc = lax.broadcasted_iota(jnp.int32, (pair, pair), 1)
    ones_pair = ((pr >= HG_DK) == (pc >= HG_DK)).astype(BF16)
    sel_r = lax.broadcasted_iota(jnp.int32, (c, c * c), 0)
    sel_c = lax.broadcasted_iota(jnp.int32, (c, c * c), 1)
    sel = ((sel_c >= sel_r * c) & (sel_c < sel_r * c + c)).astype(BF16)
    zero8 = jnp.zeros((SUBLANE, nh * HG_DK), F32)

    def scores_of(d, rows):
        q = q_s[rows, :]
        b2 = b_s[d, rows, :]
        c2 = c_s[d, rows, :]
        halves = [c2[:SUBLANE], c2[SUBLANE:]]
        parts = []
        for t in range(c):
            bt, qt = b2[t:t + 1, :], q[t:t + 1, :]
            own = t // SUBLANE
            row = []
            for half, ch in enumerate(halves):
                if half == own:
                    tt = t - own * SUBLANE
                    vis = (s8 <= tt) if d == 0 else (s8 >= tt)
                    row.append(qt * jnp.exp2(jnp.where(vis, bt - ch, NEG)))
                elif (half < own) == (d == 0):
                    row.append(qt * jnp.exp2(bt - ch))
                else:
                    row.append(zero8)
            parts.append(row)
        p = jnp.concatenate([parts[t][half] for t in range(c) for half in range(2)], axis=0).astype(BF16)
        return jnp.concatenate([_dot(p[:, j * pair:(j + 1) * pair], ones_pair) for j in range(nh * HG_DK // pair)],
                               axis=1)

    def state_step(d, h, rows, ci):
        st = st_s[d, h]
        o_inter = _dot_nt(qb_s[d, rows, hs[h]], st.astype(BF16))
        dec8 = dec_s[d, pl.ds(pl.multiple_of(ci * c, c), SUBLANE), hs[h]]
        st_dec = (st.reshape(HG_DK // SUBLANE, SUBLANE, HG_DK) * dec8[None]).reshape(HG_DK, HG_DK)
        st_s[d, h] = st_dec + _dot_tn(zi_ref[rows, hs[h]].astype(BF16), kd_s[d, rows, hs[h]])
        return o_inter

    per_trip = HG_CHUNKS_PER_TRIP

    def body(i, carry):
        cis = [[per_trip * i + u for u in range(per_trip)],
               [n_chunks - 1 - per_trip * i - u for u in range(per_trip)]]
        rows = [[_rows(ci, c) for ci in cis[d]] for d in range(2)]
        scores = [[scores_of(d, rows[d][u]) for u in range(per_trip)] for d in range(2)]
        o_inter = [[[state_step(d, h, rows[d][u], cis[d][u]) for h in range(nh)] for u in range(per_trip)]
                   for d in range(2)]
        for d in range(2):
            for u in range(per_trip):
                v_rep = jnp.concatenate([zi_ref[rows[d][u], :]] * c, axis=0)
                o_intra = _dot(sel, (scores[d][u] * v_rep).astype(BF16))
                for h in range(nh):
                    (of_s if d == 0 else ob_s)[rows[d][u], hs[h]] = o_intra[:, hs[h]] + o_inter[d][u][h]
        return carry
    lax.fori_loop(0, n_chunks // per_trip, body, 0)

    for d in range(2):
        for h in range(nh):
            sfin_ref[d, h] = st_s[d, h].T

    def fin(r, carry):
        rows = _rows(r, EW_CHUNK)
        for h in range(HG_HEADS):
            cols = slice(h * HG_DK, (h + 1) * HG_DK)
            o = of_s[rows, cols] + ob_s[rows, cols]
            o = o * lax.rsqrt(jnp.mean(o * o, axis=-1, keepdims=True) + EPS) * ng_ref[h:h + 1, :]
            o_ref[rows, cols] = (o * _silu(zo_ref[rows, cols])).astype(o_ref.dtype)
        return carry
    lax.fori_loop(0, n // EW_CHUNK, fin, 0, unroll=2)


def _hgrn_call(z, lb_logits, hgrn_norm_g, s0_t, *, layer, n_seq, n_tok):
    w = HG_HEADS * HG_DK
    depth = lb_logits.shape[0]
    zspec = lambda k: pl.BlockSpec((n_tok, w), lambda s: (s, k))
    in_specs = [zspec(0), zspec(1), zspec(2), zspec(3), zspec(4),
                pl.BlockSpec((depth, 2, w), lambda s: (0, 0, 0)),
                pl.BlockSpec((None, HG_HEADS, HG_DK), lambda s: (layer, 0, 0))]
    args = [z, z, z, z, z, lb_logits, hgrn_norm_g]
    st_spec = pl.BlockSpec((None, 2, HG_HEADS, HG_DK, HG_DK), lambda s: (s, 0, 0, 0, 0))
    if s0_t is not None:
        in_specs.append(st_spec)
        args.append(s0_t)
    return pl.pallas_call(
        functools.partial(_hgrn_kernel, layer=layer, has_s0=s0_t is not None),
        grid=(n_seq,),
        in_specs=in_specs,
        out_specs=[pl.BlockSpec((n_tok, w), lambda s: (s, 0)), st_spec],
        out_shape=[jax.ShapeDtypeStruct((n_seq * n_tok, w), BF16),
                   jax.ShapeDtypeStruct((n_seq, 2, HG_HEADS, HG_DK, HG_DK), F32)],
        scratch_shapes=[pltpu.VMEM((n_tok, w), F32),
                        pltpu.VMEM((2, n_tok, w), F32),
                        pltpu.VMEM((2, n_tok, w), F32),
                        pltpu.VMEM((2, n_tok, w), BF16),
                        pltpu.VMEM((2, n_tok, w), BF16),
                        pltpu.VMEM((2, n_tok, w), F32),
                        pltpu.VMEM((n_tok, w), F32), pltpu.VMEM((n_tok, w), F32),
                        pltpu.VMEM((2, HG_HEADS, HG_DK, HG_DK), F32)],
        compiler_params=_params(("parallel",)),
    )(*args)


def _ctx_attn_kernel(q0_ref, q1_ref, k0_ref, k1_ref, v0_ref, v1_ref, o_ref, ko_ref, vo_ref, *, scale):
    q_refs, k_refs, v_refs = (q0_ref, q1_ref), (k0_ref, k1_ref), (v0_ref, v1_ref)
    half = q0_ref.shape[1]
    heads_per_block = half // NA_HEAD_DIM
    ko_ref[:, :half] = k0_ref[...]
    ko_ref[:, half:] = k1_ref[...]
    vo_ref[:, :half] = v0_ref[...]
    vo_ref[:, half:] = v1_ref[...]

    def head(refs, h):
        blk, off = divmod(h, heads_per_block)
        return refs[blk][:, off * NA_HEAD_DIM:(off + 1) * NA_HEAD_DIM].astype(BF16)

    scores = [_dot_nt(head(q_refs, h), head(k_refs, h)) * scale for h in range(NA_HEADS)]
    probs = []
    for s in scores:
        p = jnp.exp(s - jnp.max(s, axis=-1, keepdims=True))
        probs.append((p / jnp.sum(p, axis=-1, keepdims=True)).astype(BF16))
    for h in range(NA_HEADS):
        o = _dot(probs[h], head(v_refs, h))
        o_ref[:, h * NA_HEAD_DIM:(h + 1) * NA_HEAD_DIM] = o.astype(o_ref.dtype)


def _ctx_attn_call(z, *, n_seq, n_tok, col0):
    w = NA_HEADS * NA_HEAD_DIM
    half = w // 2
    zspec = lambda k: pl.BlockSpec((n_tok, half), lambda s: (s, col0 // half + k))
    return pl.pallas_call(
        functools.partial(_ctx_attn_kernel, scale=NA_HEAD_DIM ** -0.5),
        grid=(n_seq,),
        in_specs=[zspec(k) for k in range(6)],
        out_specs=[pl.BlockSpec((n_tok, w), lambda s: (s, 0))] * 3,
        out_shape=[jax.ShapeDtypeStruct((n_seq * n_tok, w), BF16),
                   jax.ShapeDtypeStruct((n_seq * n_tok, w), F32),
                   jax.ShapeDtypeStruct((n_seq * n_tok, w), F32)],
        compiler_params=_params(("parallel",)),
    )(z, z, z, z, z, z)


def _na_kernel(q_ref, k_ref, v_ref, kc_ref, vc_ref, bias_ref, o_ref, *, scale, rows, kh):
    q = q_ref[...].astype(BF16)
    k = k_ref[...].astype(BF16)
    v = v_ref[...].astype(BF16)
    qrow = lambda r: slice(r * GRID_W, (r + 1) * GRID_W)
    krows = lambda r: slice(min(max(r - kh // 2, 0), rows - kh) * GRID_W,
                            (min(max(r - kh // 2, 0), rows - kh) + kh) * GRID_W)
    s_ctx = _dot_nt(q, kc_ref[...].astype(BF16)) * scale
    s_loc = [_dot_nt(q[qrow(r)], k[krows(r)]) * scale + bias_ref[r] for r in range(rows)]
    p_loc, p_ctx = [], []
    for r in range(rows):
        sc = s_ctx[qrow(r)]
        m = jnp.maximum(jnp.max(s_loc[r], axis=-1, keepdims=True), jnp.max(sc, axis=-1, keepdims=True))
        el = jnp.exp(s_loc[r] - m)
        ec = jnp.exp(sc - m)
        den = jnp.sum(el, axis=-1, keepdims=True) + jnp.sum(ec, axis=-1, keepdims=True)
        p_loc.append((el / den).astype(BF16))
        p_ctx.append((ec / den).astype(BF16))
    o_ctx = _dot(jnp.concatenate(p_ctx, axis=0), vc_ref[...].astype(BF16))
    for r in range(rows):
        o = _dot(p_loc[r], v[krows(r)]) + o_ctx[qrow(r)]
        o_ref[qrow(r), :] = o.astype(o_ref.dtype)


def _na_bias(rpb, rows, kh):
    heads = rpb.shape[0]
    qcol = jnp.arange(GRID_W)
    kcol = jnp.arange(GRID_W)
    win_start = jnp.clip(qcol - NA_KW // 2, 0, GRID_W - NA_KW)
    in_win = (kcol[None, :] >= win_start[:, None]) & (kcol[None, :] < win_start[:, None] + NA_KW)
    lo = GRID_W - NA_KW
    pad = jnp.pad(rpb.astype(F32), ((0, 0), (0, 0), (lo, lo)))
    toep = jnp.stack([pad[:, :, GRID_W - 1 - q:2 * GRID_W - 1 - q] for q in range(GRID_W)], axis=2)
    toep = jnp.where(in_win[None, None], toep, NEG)
    per_row = []
    for r in range(rows):
        dy0 = min(max(r - kh // 2, 0), rows - kh) - r + NA_KH - 1
        per_row.append(toep[:, dy0:dy0 + kh])
    bias = jnp.stack(per_row, axis=1)
    return jnp.transpose(bias, (0, 1, 3, 2, 4)).reshape(heads, rows, GRID_W, kh * GRID_W)


def _na_call(z, cache_k, cache_v, bias, *, layer, n_seq, n_tok, col_block0):
    rows = n_tok // GRID_W
    kh = min(NA_KH, rows)
    dh = NA_HEAD_DIM
    past = cache_k.shape[2]
    zspec = lambda k: pl.BlockSpec((n_tok, dh), lambda b, h: (b, col_block0 + k * NA_HEADS + h))
    cspec = pl.BlockSpec((None, None, past, dh), lambda b, h: (b, layer, 0, h))
    return pl.pallas_call(
        functools.partial(_na_kernel, scale=dh ** -0.5, rows=rows, kh=kh),
        grid=(n_seq, NA_HEADS),
        in_specs=[zspec(0), zspec(1), zspec(2), cspec, cspec,
                  pl.BlockSpec((None, rows, GRID_W, kh * GRID_W), lambda b, h: (h, 0, 0, 0))],
        out_specs=pl.BlockSpec((n_tok, dh), lambda b, h: (b, h)),
        out_shape=jax.ShapeDtypeStruct((n_seq * n_tok, NA_HEADS * dh), BF16),
        compiler_params=_params(("parallel", "parallel")),
    )(z, z, z, cache_k, cache_v, bias)


def _lru_kernel(zx_ref, zg_ref, cw_ref, cb_ref, wa_ref, ba_ref, wx_ref, bx_ref, lam_ref, *rest,
                n_seq, n_tok, has_h0):
    if has_h0:
        h0_ref, y_ref, hfin_ref, a_s, u_s, y_s = rest
    else:
        y_ref, hfin_ref, a_s, u_s, y_s = rest
    tpos = lax.broadcasted_iota(jnp.int32, (n_tok, LANE), 0)
    gpos = tpos & (SUBLANE - 1)
    neg_lam = -lam_ref[...]
    softplus = jnp.maximum(neg_lam, 0.0) + jnp.log1p(jnp.exp(-jnp.abs(neg_lam)))
    left = LRU_CONV // 2

    def gates(g, carry):
        rows = _rows(g, n_tok)
        zx = zx_ref[rows, :]
        x = cb_ref[...] + cw_ref[left:left + 1, :] * zx
        for j in range(LRU_CONV):
            off = j - left
            if off == 0:
                continue
            shifted = pltpu.roll(zx, (-off) % n_tok, 0)
            valid = (tpos + off >= 0) & (tpos + off < n_tok)
            x = x + cw_ref[j:j + 1, :] * jnp.where(valid, shifted, 0.0)
        xb = x.astype(BF16)
        for d in range(2):
            r_gate = jax.nn.sigmoid(_dot(xb, wa_ref[d].astype(BF16)) + ba_ref[d:d + 1, :])
            i_gate = jax.nn.sigmoid(_dot(xb, wx_ref[d].astype(BF16)) + bx_ref[d:d + 1, :])
            log_a = (-LRU_C * r_gate) * softplus[d:d + 1, :]
            a = jnp.exp(log_a)
            u = jnp.sqrt(-_expm1(2.0 * log_a)) * (i_gate * x)
            sh = 1
            while sh < SUBLANE:
                if d == 0:
                    inside = gpos >= sh
                    amt = sh
                else:
                    inside = gpos < SUBLANE - sh
                    amt = n_tok - sh
                a_prev = jnp.where(inside, pltpu.roll(a, amt, 0), 1.0)
                u_prev = jnp.where(inside, pltpu.roll(u, amt, 0), 0.0)
                u = u + a * u_prev
                a = a * a_prev
                sh *= 2
            a_s[d, rows, :] = a
            u_s[d, rows, :] = u
        return carry
    lax.fori_loop(0, n_seq, gates, 0)

    n_groups = n_tok // SUBLANE
    edge = (SUBLANE - 1, 0)

    def bcast(row):
        return jnp.broadcast_to(row, (SUBLANE, LANE))

    if has_h0:
        h_init = tuple(bcast(h0_ref[d, g:g + 1, :]) for g in range(n_seq) for d in range(2))
    else:
        h_init = tuple(jnp.zeros((SUBLANE, LANE), F32) for _ in range(2 * n_seq))

    def step(j, carry):
        out = []
        for g in range(n_seq):
            for d in range(2):
                grp = j if d == 0 else n_groups - 1 - j
                rows = pl.ds(pl.multiple_of(g * n_tok + grp * SUBLANE, SUBLANE), SUBLANE)
                h = u_s[d, rows, :] + a_s[d, rows, :] * carry[2 * g + d]
                y_s[d, rows, :] = h
                out.append(bcast(h[edge[d]:edge[d] + 1, :]))
        return tuple(out)
    h_last = lax.fori_loop(0, n_groups, step, h_init)
    for g in range(n_seq):
        for d in range(2):
            hfin_ref[d, g:g + 1, :] = h_last[2 * g + d][0:1, :]

    def fin(g, carry):
        rows = _rows(g, n_tok)
        y = (y_s[0, rows, :] + y_s[1, rows, :]) * jax.nn.gelu(zg_ref[rows, :], approximate=True)
        y_ref[rows, :] = y.astype(y_ref.dtype)
        return carry
    lax.fori_loop(0, n_seq, fin, 0)


def _block_diag_pairs(w):
    depth, nd, nb, bw, _ = w.shape
    w = w.reshape(depth, nd, nb // 2, 2, bw, bw)
    eye = jnp.eye(2, dtype=w.dtype)
    out = w[:, :, :, :, :, None, :] * eye[None, None, None, :, None, :, None]
    return out.reshape(depth, nd, nb // 2, 2 * bw, 2 * bw)


def _lru_call(z, conv_w, conv_b, wa_bd, b_a, wx_bd, b_x, lam, h0_t, *, layer, n_seq, n_tok, col_block0):
    w = conv_b.shape[1]
    n_cb = w // LANE
    rows = n_seq * n_tok
    zspec = lambda k: pl.BlockSpec((rows, LANE), lambda cb: (0, col_block0 + k * n_cb + cb))
    vec2 = pl.BlockSpec((None, 2, LANE), lambda cb: (layer, 0, cb))
    wspec = pl.BlockSpec((None, 2, None, LANE, LANE), lambda cb: (layer, 0, cb, 0, 0))
    hspec = pl.BlockSpec((2, n_seq, LANE), lambda cb: (0, 0, cb))
    in_specs = [zspec(0), zspec(1),
                pl.BlockSpec((None, LRU_CONV, LANE), lambda cb: (layer, 0, cb)),
                pl.BlockSpec((None, 1, LANE), lambda cb: (layer, 0, cb)),
                wspec, vec2, wspec, vec2, vec2]
    args = [z, z, conv_w, conv_b.reshape(conv_b.shape[0], 1, w), wa_bd, b_a, wx_bd, b_x, lam]
    if h0_t is not None:
        in_specs.append(hspec)
        args.append(h0_t)
    return pl.pallas_call(
        functools.partial(_lru_kernel, n_seq=n_seq, n_tok=n_tok, has_h0=h0_t is not None),
        grid=(n_cb,),
        in_specs=in_specs,
        out_specs=[pl.BlockSpec((rows, LANE), lambda cb: (0, cb)), hspec],
        out_shape=[jax.ShapeDtypeStruct((rows, w), BF16), jax.ShapeDtypeStruct((2, n_seq, w), F32)],
        scratch_shapes=[pltpu.VMEM((2, rows, LANE), F32)] * 3,
        compiler_params=_params(("parallel",)),
    )(*args)


MERGE_GATE_BLOCK = 512


def _cast_rows(w_ref, w_s):
    def body(r, carry):
        sl = _rows(r, ROW_CHUNK)
        w_s[sl, :] = w_ref[sl, :].astype(BF16)
        return carry
    lax.fori_loop(0, w_ref.shape[0] // ROW_CHUNK, body, 0)


def _branch_sum_kernel(oa_ref, ob_ref, oc_ref, *rest, n_gate_blocks):
    gate_refs = rest[:3 * n_gate_blocks]
    wa_ref, wb_ref, wc_ref, m_ref, wa_s, wb_s, wc_s = rest[3 * n_gate_blocks:]

    @pl.when(pl.program_id(0) == 0)
    def _():
        _cast_rows(wa_ref, wa_s)
        _cast_rows(wb_ref, wb_s)
        _cast_rows(wc_ref, wc_s)

    gw = MERGE_GATE_BLOCK
    for n in range(n_gate_blocks):
        cols = slice(n * gw, (n + 1) * gw)
        m = None
        for k, (o_ref, w_s) in enumerate(((oa_ref, wa_s), (ob_ref, wb_s), (oc_ref, wc_s))):
            term = jax.nn.sigmoid(gate_refs[k * n_gate_blocks + n][...]) * _dot(o_ref[...], w_s[:, cols])
            m = term if m is None else m + term
        m_ref[:, cols] = m.astype(m_ref.dtype)


def _out_proj_kernel(x_ref, mods_ref, m_ref, wo_ref, o_ref, wo_s):
    @pl.when(pl.program_id(0) == 0)
    def _():
        _cast_rows(wo_ref, wo_s)

    gate = mods_ref[5:6, :]
    for r in range(x_ref.shape[0] // ROW_CHUNK):
        sl = slice(r * ROW_CHUNK, (r + 1) * ROW_CHUNK)
        o_ref[sl, :] = x_ref[sl, :] + gate * _dot(m_ref[sl, :], wo_s[...])


def _merge_call(x, mods, o_a, o_b, o_c, z, w_proj_a, w_proj_b, w_proj_c, w_out, *, layer, group0,
                rows_per_group, gate_col0):
    t, d = x.shape
    gw = MERGE_GATE_BLOCK
    nb = d // gw
    g0 = gate_col0 // gw
    tm1, tm2 = 256, 512
    widths = (o_a.shape[1], o_b.shape[1], o_c.shape[1])
    gate_specs = [pl.BlockSpec((tm1, gw), lambda i, col=g0 + k * nb + n: (i, col))
                  for k in range(3) for n in range(nb)]
    resident = lambda w: pl.BlockSpec((None, w, d), lambda i: (layer, 0, 0), pipeline_mode=pl.Buffered(1))
    m = pl.pallas_call(
        functools.partial(_branch_sum_kernel, n_gate_blocks=nb),
        grid=(t // tm1,),
        in_specs=([pl.BlockSpec((tm1, w), lambda i: (i, 0)) for w in widths] + gate_specs
                  + [resident(w) for w in widths]),
        out_specs=pl.BlockSpec((tm1, d), lambda i: (i, 0)),
        out_shape=jax.ShapeDtypeStruct((t, d), BF16),
        scratch_shapes=[pltpu.VMEM((w, d), BF16) for w in widths],
        compiler_params=_params(("arbitrary",)),
    )(o_a, o_b, o_c, *([z] * (3 * nb)), w_proj_a, w_proj_b, w_proj_c)
    group = _group_index(tm2, group0, rows_per_group)
    return pl.pallas_call(
        _out_proj_kernel,
        grid=(t // tm2,),
        in_specs=[
            pl.BlockSpec((tm2, d), lambda i: (i, 0)),
            pl.BlockSpec((None, None, N_MOD, d), lambda i: (layer, group(i), 0, 0)),
            pl.BlockSpec((tm2, d), lambda i: (i, 0)),
            resident(d),
        ],
        out_specs=pl.BlockSpec((tm2, d), lambda i: (i, 0)),
        out_shape=jax.ShapeDtypeStruct((t, d), F32),
        scratch_shapes=[pltpu.VMEM((d, d), BF16)],
        compiler_params=_params(("arbitrary",)),
    )(x, mods, m, w_out)


def kernel(x_prompt, x_sample, cache_na_k, cache_na_v, state_hgrn, state_lru, c, c_ctx, mod_w, mod_b, norm_g, ffn1_w_up, ffn1_w_down, ffn2_w_up, ffn2_w_down, w_in, hgrn_lb_logits, hgrn_norm_g, na_rpb, lru_conv_w, lru_conv_b, lru_w_a, lru_b_a, lru_w_x, lru_b_x, lru_lambda, w_proj_a, w_proj_b, w_proj_c, w_out, final_norm_g):
    b_ctx, n_ctx, d = x_prompt.shape
    b_lat, n_lat, _ = x_sample.shape
    depth = mod_w.shape[0]
    t_ctx, t_lat = b_ctx * n_ctx, b_lat * n_lat
    hg_w = HG_HEADS * HG_DK
    na_w = NA_HEADS * NA_HEAD_DIM
    lru_w = lru_conv_b.shape[1]
    na_col0 = 5 * hg_w
    lru_col0 = na_col0 + 3 * na_w
    gate_col0 = lru_col0 + 2 * lru_w

    ctx = dict(group0=0, rows_per_group=t_ctx)
    lat = dict(group0=1, rows_per_group=n_lat)
    xc = x_prompt.reshape(t_ctx, d)
    xl = x_sample.reshape(t_lat, d)
    cond = jnp.concatenate([c_ctx[None], c, jnp.zeros((N_COND_ROWS - 1 - b_lat, d), F32)], axis=0)
    mods = _mods_call(cond, mod_w, mod_b).reshape(depth, N_COND_ROWS, N_MOD, d)

    rows = n_lat // GRID_W
    kh = min(NA_KH, rows)
    wa_bd = _block_diag_pairs(lru_w_a)
    wx_bd = _block_diag_pairs(lru_w_x)
    cache_k = cache_na_k.reshape(b_lat, depth, cache_na_k.shape[2], na_w)
    cache_v = cache_na_v.reshape(b_lat, depth, cache_na_v.shape[2], na_w)
    s0_t = jnp.swapaxes(state_hgrn, -1, -2)
    h0_t = jnp.transpose(state_lru, (1, 2, 0, 3))

    ks, vs, hgs, lrus = [], [], [], []
    for l in range(depth):
        last = l == depth - 1
        final_g = final_norm_g if last else None
        lru_args = (lru_conv_w, lru_conv_b, wa_bd, lru_b_a, wx_bd, lru_b_x, lru_lambda)
        merge_w = (w_proj_a, w_proj_b, w_proj_c, w_out)

        xc = _ffn_call(xc, mods, norm_g, ffn1_w_up, ffn1_w_down, layer=l, slot=0, **ctx)
        zc = _inproj_call(xc, mods, norm_g, w_in, layer=l, **ctx)
        oa, s_c = _hgrn_call(zc, hgrn_lb_logits, hgrn_norm_g, None, layer=l, n_seq=b_ctx, n_tok=n_ctx)
        ob, k_l, v_l = _ctx_attn_call(zc, n_seq=b_ctx, n_tok=n_ctx, col0=na_col0)
        oc, h_c = _lru_call(zc, *lru_args, None, layer=l, n_seq=b_ctx, n_tok=n_ctx, col_block0=lru_col0 // LANE)
        xc = _merge_call(xc, mods, oa, ob, oc, zc, *merge_w, layer=l, gate_col0=gate_col0, **ctx)
        xc = _ffn_call(xc, mods, norm_g, ffn2_w_up, ffn2_w_down, layer=l, slot=2, final_g=final_g, **ctx)

        xl = _ffn_call(xl, mods, norm_g, ffn1_w_up, ffn1_w_down, layer=l, slot=0, **lat)
        zl = _inproj_call(xl, mods, norm_g, w_in, layer=l, **lat)
        oa, _ = _hgrn_call(zl, hgrn_lb_logits, hgrn_norm_g, s0_t[:, l], layer=l, n_seq=b_lat, n_tok=n_lat)
        bias = _na_bias(na_rpb[l], rows, kh)
        ob = _na_call(zl, cache_k, cache_v, bias, layer=l, n_seq=b_lat, n_tok=n_lat,
                      col_block0=na_col0 // NA_HEAD_DIM)
        oc, _ = _lru_call(zl, *lru_args, h0_t[l], layer=l, n_seq=b_lat, n_tok=n_lat, col_block0=lru_col0 // LANE)
        xl = _merge_call(xl, mods, oa, ob, oc, zl, *merge_w, layer=l, gate_col0=gate_col0, **lat)
        xl = _ffn_call(xl, mods, norm_g, ffn2_w_up, ffn2_w_down, layer=l, slot=2, final_g=final_g, **lat)

        ks.append(k_l.reshape(b_ctx, n_ctx, NA_HEADS, NA_HEAD_DIM))
        vs.append(v_l.reshape(b_ctx, n_ctx, NA_HEADS, NA_HEAD_DIM))
        hgs.append(s_c)
        lrus.append(jnp.transpose(h_c, (1, 0, 2)))

    return (xc.reshape(b_ctx, n_ctx, d), xl.reshape(b_lat, n_lat, d), jnp.stack(ks, axis=1),
            jnp.stack(vs, axis=1), jnp.stack(hgs, axis=1), jnp.stack(lrus, axis=1))
```
